```python
import math
import jax, jax.numpy as jnp
from jax import lax
import numpy as np

D_MODEL = 1024
BATCH = 1
SEQ = 16384
DEPTH = 1
DEC_BATCH = 128
DEC_SEQ = 4
PAST_LEN = 8192
PAGE_SIZE = 128

N_META = 16
SSM_WIDTH = D_MODEL // 2
SSM_GROUP_CH = 16
SSM_GROUPS = SSM_WIDTH // SSM_GROUP_CH
SSM_STATE = 64
N_HEADS = 8
HEAD_DIM = 64
ATT_WIDTH = N_HEADS * HEAD_DIM
IDX_HEADS = N_HEADS // 2
IDX_DIM = 64
TOPK_MAX = 256
Q_BLOCK = 128
N_EGROUPS = 4
EXPERTS_PER_GROUP = 4
N_EXPERTS = N_EGROUPS * EXPERTS_PER_GROUP
TOP_K_EXPERTS = 2
D_EXPERT = D_MODEL // 4
DN_ALPHA = (2 * DEPTH) ** 0.25
DN_BETA = (8 * DEPTH) ** -0.25
LN_EPS = 1e-5
IN_SPLITS = (SSM_WIDTH, ATT_WIDTH, ATT_WIDTH, ATT_WIDTH, IDX_HEADS * IDX_DIM, IDX_DIM, IDX_HEADS, D_MODEL, D_MODEL)
D_IN = sum(IN_SPLITS)

kernel_name = 'hybrid_s5_dsa_hmoe_step'


def _layernorm(x, g, b):
    f32 = jnp.float32
    xf = x.astype(f32)
    mu = jnp.mean(xf, -1, keepdims=True)
    var = jnp.mean(jnp.square(xf - mu), -1, keepdims=True)
    return ((xf - mu) * lax.rsqrt(var + LN_EPS) * g.astype(f32) + b.astype(f32)).astype(x.dtype)


def _in_proj(h, w_in):
    b, n = h.shape[:2]
    z = jnp.einsum('bnd,de->bne', h, w_in)
    cuts = np.cumsum(IN_SPLITS)[:-1].tolist()
    u, q, k, v, qi, ki, wi, ga, gb = jnp.split(z, cuts, axis=-1)
    q = q.reshape(b, n, N_HEADS, HEAD_DIM)
    k = k.reshape(b, n, N_HEADS, HEAD_DIM)
    v = v.reshape(b, n, N_HEADS, HEAD_DIM)
    qi = qi.reshape(b, n, IDX_HEADS, IDX_DIM)
    wi = wi * IDX_HEADS ** -0.5
    return u, q, k, v, qi, ki, wi, ga, gb


def _cplx_affine_combine(c1, c2):
    a1r, a1i, b1r, b1i = c1
    a2r, a2i, b2r, b2i = c2
    return (a2r * a1r - a2i * a1i, a2r * a1i + a2i * a1r,
            a2r * b1r - a2i * b1i + b2r, a2r * b1i + a2i * b1r + b2i)


def _ssm_branch(u, h0_re, h0_im, a_re, a_im, log_dt, b_re, b_im, c_re, c_im, d_skip, w_glu):
    f32 = jnp.float32
    bsz, n, _ = u.shape
    a_re = a_re.astype(f32)
    a_im = a_im.astype(f32)
    dt = jnp.exp(log_dt.astype(f32))[:, None]
    mag = jnp.exp(a_re * dt)
    ab_re = mag * jnp.cos(a_im * dt)
    ab_im = mag * jnp.sin(a_im * dt)
    den = a_re * a_re + a_im * a_im
    nr = ab_re - 1.0
    f_re = (nr * a_re + ab_im * a_im) / den
    f_im = (ab_im * a_re - nr * a_im) / den
    b_re = b_re.astype(f32)
    b_im = b_im.astype(f32)
    bb_re = f_re[..., None] * b_re - f_im[..., None] * b_im
    bb_im = f_re[..., None] * b_im + f_im[..., None] * b_re
    uf = u.astype(f32)
    ug = uf.reshape(bsz, n, SSM_GROUPS, SSM_GROUP_CH)
    bu_re = jnp.einsum('gpm,bngm->bngp', bb_re, ug)
    bu_im = jnp.einsum('gpm,bngm->bngp', bb_im, ug)
    h0_re = h0_re.astype(f32)
    h0_im = h0_im.astype(f32)
    bu_re = bu_re.at[:, 0].add(ab_re * h0_re - ab_im * h0_im)
    bu_im = bu_im.at[:, 0].add(ab_re * h0_im + ab_im * h0_re)
    shape = bu_re.shape
    elems = (jnp.broadcast_to(ab_re, shape), jnp.broadcast_to(ab_im, shape), bu_re, bu_im)
    _, _, hr, hi = lax.associative_scan(_cplx_affine_combine, elems, axis=1)
    y = (jnp.einsum('gmp,bngp->bngm', c_re.astype(f32), hr)
         - jnp.einsum('gmp,bngp->bngm', c_im.astype(f32), hi))
    y = y.reshape(bsz, n, SSM_WIDTH) + d_skip.astype(f32) * uf
    y = jax.nn.gelu(y).astype(u.dtype)
    y = y * jax.nn.sigmoid(jnp.einsum('bnc,ce->bne', y, w_glu))
    return y, hr[:, -1], hi[:, -1]


def _index_scores(qi, wi, kidx):
    f32 = jnp.float32
    r = jax.nn.relu(jnp.einsum('bqhi,bli->bqhl', qi, kidx).astype(f32) * IDX_DIM ** -0.5)
    return jnp.einsum('bqh,bqhl->bql', wi.astype(f32), r)


def _attend(q, k_sel, v_sel, valid):
    s = jnp.einsum('bqhd,bqkhd->bqhk', q, k_sel).astype(jnp.float32) * HEAD_DIM ** -0.5
    s = jnp.where(valid[:, :, None, :], s, -jnp.inf)
    p = jax.nn.softmax(s, axis=-1)
    return jnp.einsum('bqhk,bqkhd->bqhd', p.astype(v_sel.dtype), v_sel)


def _prompt_attention(q, k, v, qi, ki, wi, topk):
    b, t = q.shape[:2]
    n_blk = -(-t // Q_BLOCK)
    pad = n_blk * Q_BLOCK - t

    def blocks(a):
        a = jnp.pad(a, [(0, 0), (0, pad)] + [(0, 0)] * (a.ndim - 2))
        return jnp.moveaxis(a.reshape((b, n_blk, Q_BLOCK) + a.shape[2:]), 1, 0)

    kpos = jnp.arange(t)
    bidx = jnp.arange(b)[:, None, None]

    def one_block(args):
        q_b, qi_b, wi_b, start = args
        qpos = start + jnp.arange(Q_BLOCK)
        sc = _index_scores(qi_b, wi_b, ki)
        sc = jnp.where(kpos[None, None, :] <= qpos[None, :, None], sc, -jnp.inf)
        _, idx = lax.top_k(sc, topk)
        valid = idx <= qpos[None, :, None]
        return _attend(q_b, k[bidx, idx], v[bidx, idx], valid)

    o = lax.map(one_block, (blocks(q), blocks(qi), blocks(wi), jnp.arange(n_blk) * Q_BLOCK))
    return jnp.moveaxis(o, 0, 1).reshape(b, n_blk * Q_BLOCK, ATT_WIDTH)[:, :t]


def _sample_attention(q, k_new, v_new, qi, ki_new, wi, cache_k, cache_v, cache_kidx, page_table, layer, topk):
    db, s = q.shape[:2]
    past = page_table.shape[1] * PAGE_SIZE
    ki_past = cache_kidx[layer, page_table].reshape(db, past, IDX_DIM).astype(ki_new.dtype)
    ki_all = jnp.concatenate([ki_past, ki_new], axis=1)
    qpos = past + jnp.arange(s)
    kpos = jnp.arange(past + s)
    sc = _index_scores(qi, wi, ki_all)
    sc = jnp.where(kpos[None, None, :] <= qpos[None, :, None], sc, -jnp.inf)
    _, idx = lax.top_k(sc, topk)
    valid = idx <= qpos[None, :, None]
    bidx = jnp.arange(db)[:, None, None]
    pidx = jnp.minimum(idx, past - 1)
    phys = page_table[bidx, pidx // PAGE_SIZE]
    off = pidx % PAGE_SIZE
    nidx = jnp.clip(idx - past, 0, s - 1)
    from_past = (idx < past)[..., None, None]
    k_sel = jnp.where(from_past, cache_k[layer, phys, off].astype(k_new.dtype), k_new[bidx, nidx])
    v_sel = jnp.where(from_past, cache_v[layer, phys, off].astype(v_new.dtype), v_new[bidx, nidx])
    return _attend(q, k_sel, v_sel, valid).reshape(db, s, ATT_WIDTH)


def _mix_out(y_ssm, o_att, ga, gb, w_ssm_out, w_att_out, w_o):
    m = (jax.nn.sigmoid(ga) * jnp.einsum('bnc,cd->bnd', y_ssm, w_ssm_out)
         + jax.nn.sigmoid(gb) * jnp.einsum('bnc,cd->bnd', o_att, w_att_out))
    return jnp.einsum('bnd,de->bne', m, w_o)


def _hier_moe(h, w_rg, b_rg, w_re, b_re, w_gate, w_up, w_down):
    f32 = jnp.float32
    bsz, n, _ = h.shape
    g_logits = (jnp.einsum('bnd,dg->bng', h, w_rg) + b_rg).astype(f32)
    g_sel = jnp.argmax(g_logits, axis=-1)
    p_g = jnp.take_along_axis(jax.nn.softmax(g_logits, -1), g_sel[..., None], axis=-1)
    e_logits = (jnp.einsum('bnd,de->bne', h, w_re) + b_re).astype(f32)
    e_logits = e_logits.reshape(bsz, n, N_EGROUPS, EXPERTS_PER_GROUP)
    e_logits = jnp.take_along_axis(e_logits, g_sel[..., None, None], axis=2)[:, :, 0]
    top_p, top_i = lax.top_k(jax.nn.softmax(e_logits, -1), TOP_K_EXPERTS)
    top_p = top_p / jnp.sum(top_p, -1, keepdims=True) * p_g
    expert_id = g_sel[..., None] * EXPERTS_PER_GROUP + top_i
    combine = jnp.sum(jax.nn.one_hot(expert_id, N_EXPERTS, dtype=f32) * top_p[..., None], axis=-2)
    act = jax.nn.silu(jnp.einsum('bnd,edf->bnef', h, w_gate)) * jnp.einsum('bnd,edf->bnef', h, w_up)
    act = act * combine[..., None].astype(act.dtype)
    return jnp.einsum('bnef,efd->bnd', act, w_down)


def _block_tail(h, mix, ln1_g, ln1_b, w_rg, b_rg, w_re, b_re, w_gate, w_up, w_down, ln2_g, ln2_b):
    h1 = _layernorm(DN_ALPHA * h + mix, ln1_g, ln1_b)
    return _layernorm(DN_ALPHA * h1 + _hier_moe(h1, w_rg, b_rg, w_re, b_re, w_gate, w_up, w_down), ln2_g, ln2_b)


def setup_inputs(seed: int = 0) -> dict:
    key = jax.random.key(seed)
    ks = iter(jax.random.split(key, 48))
    f32 = jnp.float32

    def nrm(shape, scale):
        return jax.random.normal(next(ks), shape, f32) * scale

    n_pages = PAST_LEN // PAGE_SIZE
    n_phys = (DEC_BATCH * n_pages * 5) // 4
    nl = DEPTH
    col_scale = jnp.concatenate([jnp.ones((sum(IN_SPLITS[:3]),), f32),
                                 jnp.full((IN_SPLITS[3],), DN_BETA, f32),
                                 jnp.ones((sum(IN_SPLITS[4:]),), f32)])
    inputs = {
        'x_prompt': nrm((BATCH, SEQ, D_MODEL), 1.0),
        'x_sample': nrm((DEC_BATCH, DEC_SEQ, D_MODEL), 1.0),
        'cache_k': nrm((nl, n_phys, PAGE_SIZE, N_HEADS, HEAD_DIM), 1.0),
        'cache_v': nrm((nl, n_phys, PAGE_SIZE, N_HEADS, HEAD_DIM), 1.0),
        'cache_kidx': nrm((nl, n_phys, PAGE_SIZE, IDX_DIM), 1.0),
        'state_ssm_re': nrm((nl, DEC_BATCH, SSM_GROUPS, SSM_STATE), 0.3),
        'state_ssm_im': nrm((nl, DEC_BATCH, SSM_GROUPS, SSM_STATE), 0.3),
        'page_table': jax.random.permutation(next(ks), n_phys)[:DEC_BATCH * n_pages].reshape(DEC_BATCH, n_pages).astype(jnp.int32),
        'meta_tokens': nrm((N_META, D_MODEL), 1.0),
        'w_in': nrm((nl, D_MODEL, D_IN), D_MODEL ** -0.5) * col_scale,
        'ssm_a_re': -0.5 + nrm((nl, SSM_GROUPS, SSM_STATE), 0.01),
        'ssm_a_im': math.pi * jnp.arange(SSM_STATE, dtype=f32)[None, None, :] + nrm((nl, SSM_GROUPS, SSM_STATE), 0.01),
        'ssm_log_dt': jax.random.uniform(next(ks), (nl, SSM_GROUPS), f32, math.log(1e-3), math.log(1e-1)),
        'ssm_b_re': nrm((nl, SSM_GROUPS, SSM_STATE, SSM_GROUP_CH), (2 * SSM_GROUP_CH) ** -0.5),
        'ssm_b_im': nrm((nl, SSM_GROUPS, SSM_STATE, SSM_GROUP_CH), (2 * SSM_GROUP_CH) ** -0.5),
        'ssm_c_re': nrm((nl, SSM_GROUPS, SSM_GROUP_CH, SSM_STATE), SSM_STATE ** -0.5),
        'ssm_c_im': nrm((nl, SSM_GROUPS, SSM_GROUP_CH, SSM_STATE), SSM_STATE ** -0.5),
        'ssm_d': nrm((nl, SSM_WIDTH), 1.0),
        'w_glu': nrm((nl, SSM_WIDTH, SSM_WIDTH), SSM_WIDTH ** -0.5),
        'w_ssm_out': nrm((nl, SSM_WIDTH, D_MODEL), SSM_WIDTH ** -0.5 * DN_BETA),
        'w_att_out': nrm((nl, ATT_WIDTH, D_MODEL), ATT_WIDTH ** -0.5 * DN_BETA),
        'w_o': nrm((nl, D_MODEL, D_MODEL), D_MODEL ** -0.5 * DN_BETA),
        'ln1_g': 1.0 + nrm((nl, D_MODEL), 0.02),
        'ln1_b': nrm((nl, D_MODEL), 0.02),
        'w_route_group': nrm((nl, D_MODEL, N_EGROUPS), D_MODEL ** -0.5),
        'b_route_group': nrm((nl, N_EGROUPS), 0.01),
        'w_route_expert': nrm((nl, D_MODEL, N_EXPERTS), D_MODEL ** -0.5),
        'b_route_expert': nrm((nl, N_EXPERTS), 0.01),
        'w_exp_gate': nrm((nl, N_EXPERTS, D_MODEL, D_EXPERT), D_MODEL ** -0.5),
        'w_exp_up': nrm((nl, N_EXPERTS, D_MODEL, D_EXPERT), D_MODEL ** -0.5),
        'w_exp_down': nrm((nl, N_EXPERTS, D_EXPERT, D_MODEL), D_EXPERT ** -0.5 * DN_BETA),
        'ln2_g': 1.0 + nrm((nl, D_MODEL), 0.02),
        'ln2_b': nrm((nl, D_MODEL), 0.02),
    }
    return inputs


def reference(x_prompt, x_sample, cache_k, cache_v, cache_kidx, state_ssm_re, state_ssm_im, page_table,
              meta_tokens, w_in, ssm_a_re, ssm_a_im, ssm_log_dt, ssm_b_re, ssm_b_im, ssm_c_re, ssm_c_im,
              ssm_d, w_glu, w_ssm_out, w_att_out, w_o, ln1_g, ln1_b, w_route_group, b_route_group,
              w_route_expert, b_route_expert, w_exp_gate, w_exp_up, w_exp_down, ln2_g, ln2_b):
    bsz = x_prompt.shape[0]
    past = page_table.shape[1] * PAGE_SIZE
    topk_p = min(TOPK_MAX, x_prompt.shape[1] // 4)
    topk_s = min(TOPK_MAX, (past + x_sample.shape[1]) // 4)
    meta = jnp.broadcast_to(meta_tokens[None].astype(x_prompt.dtype), (bsz, N_META, D_MODEL))
    hp = jnp.concatenate([meta, x_prompt], axis=1)
    hs = x_sample
    zeros_state = jnp.zeros((bsz, SSM_GROUPS, SSM_STATE), jnp.float32)
    kp, vp, kip, srp, sip = [], [], [], [], []
    ks_, vs_, kis, srs, sis = [], [], [], [], []
    for l in range(DEPTH):
        ssm_par = (ssm_a_re[l], ssm_a_im[l], ssm_log_dt[l], ssm_b_re[l], ssm_b_im[l],
                   ssm_c_re[l], ssm_c_im[l], ssm_d[l], w_glu[l])
        tail_par = (ln1_g[l], ln1_b[l], w_route_group[l], b_route_group[l], w_route_expert[l],
                    b_route_expert[l], w_exp_gate[l], w_exp_up[l], w_exp_down[l], ln2_g[l], ln2_b[l])
        u, q, k, v, qi, ki, wi, ga, gb = _in_proj(hp, w_in[l])
        y_ssm, hr, hi = _ssm_branch(u, zeros_state, zeros_state, *ssm_par)
        o_att = _prompt_attention(q, k, v, qi, ki, wi, topk_p)
        hp = _block_tail(hp, _mix_out(y_ssm, o_att, ga, gb, w_ssm_out[l], w_att_out[l], w_o[l]), *tail_par)
        kp.append(k)
        vp.append(v)
        kip.append(ki)
        srp.append(hr.astype(state_ssm_re.dtype))
        sip.append(hi.astype(state_ssm_im.dtype))
        u, q, k, v, qi, ki, wi, ga, gb = _in_proj(hs, w_in[l])
        y_ssm, hr, hi = _ssm_branch(u, state_ssm_re[l], state_ssm_im[l], *ssm_par)
        o_att = _sample_attention(q, k, v, qi, ki, wi, cache_k, cache_v, cache_kidx, page_table, l, topk_s)
        hs = _block_tail(hs, _mix_out(y_ssm, o_att, ga, gb, w_ssm_out[l], w_att_out[l], w_o[l]), *tail_par)
        ks_.append(k)
        vs_.append(v)
        kis.append(ki)
        srs.append(hr.astype(state_ssm_re.dtype))
        sis.append(hi.astype(state_ssm_im.dtype))
    y_prompt = hp[:, N_META:]
    y_sample = hs
    k_prompt = jnp.stack(kp)
    v_prompt = jnp.stack(vp)
    kidx_prompt = jnp.stack(kip)
    ssm_re_prompt = jnp.stack(srp)
    ssm_im_prompt = jnp.stack(sip)
    k_sample = jnp.stack(ks_)
    v_sample = jnp.stack(vs_)
    kidx_sample = jnp.stack(kis)
    ssm_re_sample = jnp.stack(srs)
    ssm_im_sample = jnp.stack(sis)
    return (y_prompt, y_sample, k_prompt, v_prompt, kidx_prompt, ssm_re_prompt, ssm_im_prompt,
            k_sample, v_sample, kidx_sample, ssm_re_sample, ssm_im_sample)
```

```python
import functools
import math

import jax
import jax.numpy as jnp
from jax import lax
from jax.experimental import pallas as pl
from jax.experimental.pallas import tpu as pltpu

F32 = jnp.float32
BF16 = jnp.bfloat16
I32 = jnp.int32

D_MODEL = 1024
N_META = 16
SSM_WIDTH = 512
SSM_GROUP_CH = 16
SSM_GROUPS = 32
SSM_STATE = 64
N_STATE = SSM_GROUPS * SSM_STATE
N_HEADS = 8
HEAD_DIM = 64
ATT_WIDTH = N_HEADS * HEAD_DIM
IDX_HEADS = 4
IDX_DIM = 64
TOPK = 256
PAGE = 128
N_EGROUPS = 4
EXPERTS_PER_GROUP = 4
N_EXPERTS = 16
D_EXPERT = 256
DN_ALPHA = 2.0 ** 0.25
LN_EPS = 1e-5
NEG_INF = float("-inf")

C_U, C_Q, C_K, C_V, C_QI, C_KIW, C_GA, C_GB, C_END = 0, 512, 1024, 1536, 2048, 2304, 2432, 3456, 4480
IN_SPLITS = (512, 512, 512, 512, 256, 64, 4, 1024, 1024)

ROW_TILE = 512
PADF = ROW_TILE - N_META
TQ = 128
TK = 512
VMEM_LIMIT = 56 * 1024 * 1024

KEY_NEG_INF = -2139095041
KEY_POS_INF = 2139095040


def _cparams(n_axes, vmem=None):
    return pltpu.CompilerParams(dimension_semantics=("arbitrary",) * n_axes, vmem_limit_bytes=vmem)


def _const_spec(shape):
    nd = len(shape)
    return pl.BlockSpec(shape, lambda *_: (0,) * nd)


def _resident_spec(shape):
    nd = len(shape)
    return pl.BlockSpec(shape, lambda *_: (0,) * nd, pipeline_mode=pl.Buffered(1))


def _sigmoid(x):
    return 1.0 / (1.0 + jnp.exp(-x))


def _gelu_tanh(x):
    return 0.5 * x * (1.0 + jnp.tanh(math.sqrt(2.0 / math.pi) * (x + 0.044715 * (x * x * x))))


def _layernorm(z, g, b):
    mu = jnp.mean(z, axis=-1, keepdims=True)
    zc = z - mu
    var = jnp.mean(zc * zc, axis=-1, keepdims=True)
    return zc * lax.rsqrt(var + LN_EPS) * g + b


def _inproj_body(x_ref, h0_ref, w_ref, u_ref, q_ref, k_ref, v_ref, kb_ref, vb_ref, qi_ref, kiw_ref,
                 ga_ref, gb_ref, *, n_head_tiles):
    if n_head_tiles:
        h = jnp.where(pl.program_id(0) < n_head_tiles, h0_ref[...], x_ref[...])
    else:
        h = x_ref[...]
    hb = h.astype(BF16)

    def seg(a, b):
        return jnp.dot(hb, w_ref[:, a:b], preferred_element_type=F32)

    u_ref[...] = seg(C_U, C_Q)
    q_ref[...] = (seg(C_Q, C_K) * HEAD_DIM ** -0.5).astype(BF16)
    k = seg(C_K, C_V)
    k_ref[...] = k
    kb_ref[...] = k.astype(BF16)
    v = seg(C_V, C_QI)
    v_ref[...] = v
    vb_ref[...] = v.astype(BF16)
    qi_ref[...] = seg(C_QI, C_KIW).astype(BF16)
    kiw_ref[...] = seg(C_KIW, C_GA)
    ga_ref[...] = seg(C_GA, C_GB)
    gb_ref[...] = seg(C_GB, C_END)


def _in_proj(x, h0, wp, *, tm, head_rows):
    n_head_tiles = head_rows // tm
    n_rows = head_rows + x.shape[0]
    grid = (n_rows // tm,)
    if n_head_tiles:
        x_spec = pl.BlockSpec((tm, D_MODEL), lambda i: (jnp.maximum(i - n_head_tiles, 0), 0))
        h_spec = pl.BlockSpec((tm, D_MODEL), lambda i: (jnp.minimum(i, n_head_tiles - 1), 0))
    else:
        h0 = x
        x_spec = pl.BlockSpec((tm, D_MODEL), lambda i: (i, 0))
        h_spec = pl.BlockSpec((tm, D_MODEL), lambda i: (0, 0))

    def row(width):
        return pl.BlockSpec((tm, width), lambda i: (i, 0))

    widths = (512, 512, 512, 512, 512, 512, 256, 128, 1024, 1024)
    dtypes = (F32, BF16, F32, F32, BF16, BF16, BF16, F32, F32, F32)
    return pl.pallas_call(
        functools.partial(_inproj_body, n_head_tiles=n_head_tiles),
        grid=grid,
        in_specs=[x_spec, h_spec, _resident_spec((D_MODEL, C_END))],
        out_specs=[row(w) for w in widths],
        out_shape=[jax.ShapeDtypeStruct((n_rows, w), d) for w, d in zip(widths, dtypes)],
        compiler_params=_cparams(1, VMEM_LIMIT),
        name="in_proj",
    )(x, h0, wp)


SSM_CHUNK = 256


def _ssm_out(h_cat, u, cbig_ref, d_ref, wglu_ref):
    y_lin = jnp.dot(h_cat.astype(BF16), cbig_ref[...], preferred_element_type=F32)
    y = _gelu_tanh(y_lin + d_ref[...] * u)
    z = jnp.dot(y.astype(BF16), wglu_ref[...], preferred_element_type=F32)
    return y * _sigmoid(z)


def _ssm_prompt_body(u_ref, bbig_ref, apw_ref, cbig_ref, d_ref, wglu_ref, y_ref, st_ref,
                     bu_ref, hs_ref, hc_ref):
    @pl.when(pl.program_id(0) == 0)
    def _():
        hc_ref[...] = jnp.zeros_like(hc_ref)

    u = u_ref[...]
    bu_ref[...] = jnp.dot(u.astype(BF16), bbig_ref[...], preferred_element_type=F32)

    def step(r, carry):
        h_re, h_im = carry
        r8 = pl.multiple_of(r * 8, 8)
        x_re = bu_ref[pl.ds(r8, 8), 0:N_STATE]
        x_im = bu_ref[pl.ds(r8, 8), N_STATE:2 * N_STATE]
        for lvl, sh in enumerate((1, 2, 4)):
            a_re = apw_ref[2 * lvl]
            a_im = apw_ref[2 * lvl + 1]
            s_re = pltpu.roll(x_re, sh, 0)
            s_im = pltpu.roll(x_im, sh, 0)
            x_re, x_im = (x_re + (a_re * s_re - a_im * s_im), x_im + (a_re * s_im + a_im * s_re))
        c_re = apw_ref[6]
        c_im = apw_ref[7]
        x_re, x_im = (x_re + (c_re * h_re - c_im * h_im), x_im + (c_re * h_im + c_im * h_re))
        hs_ref[pl.ds(r8, 8), 0:N_STATE] = x_re
        hs_ref[pl.ds(r8, 8), N_STATE:2 * N_STATE] = x_im
        return x_re[7:8, :], x_im[7:8, :]

    h_re, h_im = lax.fori_loop(0, SSM_CHUNK // 8, step, (hc_ref[0:1, :], hc_ref[1:2, :]))
    hc_ref[0:1, :] = h_re
    hc_ref[1:2, :] = h_im
    st_ref[...] = hc_ref[...]
    y_ref[...] = _ssm_out(hs_ref[...], u, cbig_ref, d_ref, wglu_ref).astype(BF16)


def _ssm_prompt(u, bbig, apw, cbig, dskip, wglu):
    n_rows = u.shape[0]
    return pl.pallas_call(
        _ssm_prompt_body,
        grid=(n_rows // SSM_CHUNK,),
        in_specs=[pl.BlockSpec((SSM_CHUNK, SSM_WIDTH), lambda i: (i, 0)),
                  _const_spec((SSM_WIDTH, 2 * N_STATE)), _const_spec((8, 8, N_STATE)),
                  _const_spec((2 * N_STATE, SSM_WIDTH)), _const_spec((1, SSM_WIDTH)),
                  _const_spec((SSM_WIDTH, SSM_WIDTH))],
        out_specs=[pl.BlockSpec((SSM_CHUNK, SSM_WIDTH), lambda i: (i, 0)), _const_spec((2, N_STATE))],
        out_shape=[jax.ShapeDtypeStruct((n_rows, SSM_WIDTH), BF16),
                   jax.ShapeDtypeStruct((2, N_STATE), F32)],
        scratch_shapes=[pltpu.VMEM((SSM_CHUNK, 2 * N_STATE), F32), pltpu.VMEM((SSM_CHUNK, 2 * N_STATE), F32),
                        pltpu.VMEM((2, N_STATE), F32)],
        compiler_params=_cparams(1, VMEM_LIMIT),
        name="ssm_prompt",
    )(u, bbig, apw, cbig, dskip, wglu)


def _ssm_sample_body(u_ref, h0re_ref, h0im_ref, bhi_ref, blo_ref, a_ref, cbig_ref, d_ref, wglu_ref,
                     y_ref, hre_ref, him_ref):
    a_re = a_ref[0:1, :]
    a_im = a_ref[1:2, :]
    h_re = h0re_ref[...]
    h_im = h0im_ref[...]
    for s in range(u_ref.shape[0]):
        u = u_ref[s]
        u_hi = u.astype(BF16)
        u_lo = (u - u_hi.astype(F32)).astype(BF16)
        bu = (jnp.dot(u_hi, bhi_ref[...], preferred_element_type=F32)
              + (jnp.dot(u_lo, bhi_ref[...], preferred_element_type=F32)
                 + jnp.dot(u_hi, blo_ref[...], preferred_element_type=F32)))
        h_re, h_im = (a_re * h_re - a_im * h_im + bu[:, 0:N_STATE],
                      a_re * h_im + a_im * h_re + bu[:, N_STATE:2 * N_STATE])
        h_cat = jnp.concatenate([h_re, h_im], axis=1)
        y_ref[s] = _ssm_out(h_cat, u, cbig_ref, d_ref, wglu_ref).astype(BF16)
    hre_ref[...] = h_re
    him_ref[...] = h_im


def _ssm_sample(u_s, h0_re, h0_im, bhi, blo, a_pair, cbig, dskip, wglu):
    n_s, n_b, _ = u_s.shape
    return pl.pallas_call(
        _ssm_sample_body,
        grid=(1,),
        in_specs=[_const_spec(u_s.shape), _const_spec(h0_re.shape), _const_spec(h0_im.shape),
                  _const_spec(bhi.shape), _const_spec(blo.shape), _const_spec(a_pair.shape),
                  _const_spec(cbig.shape), _const_spec(dskip.shape), _const_spec(wglu.shape)],
        out_specs=[_const_spec((n_s, n_b, SSM_WIDTH)), _const_spec((n_b, N_STATE)), _const_spec((n_b, N_STATE))],
        out_shape=[jax.ShapeDtypeStruct((n_s, n_b, SSM_WIDTH), BF16),
                   jax.ShapeDtypeStruct((n_b, N_STATE), F32), jax.ShapeDtypeStruct((n_b, N_STATE), F32)],
        compiler_params=_cparams(1, VMEM_LIMIT),
        name="ssm_sample",
    )(u_s, h0_re, h0_im, bhi, blo, a_pair, cbig, dskip, wglu)


def _key_to_float(key):
    bits = key ^ ((key >> 31) & 0x7FFFFFFF)
    return lax.bitcast_convert_type(bits, F32)


def _kth_largest(sc_ref, n_tiles, tile_w, rows):
    lanes = 128

    def one_pass(_, carry):
        lo, hi, c_hi = carry
        mid = (lo >> 1) + (hi >> 1) + (lo & hi & 1)
        th = jnp.broadcast_to(_key_to_float(mid), (rows, lanes))

        def count_tile(j, cnt):
            c0 = pl.multiple_of(j * tile_w, lanes)
            for a in range(tile_w // lanes):
                x = sc_ref[:, pl.ds(c0 + a * lanes, lanes)]
                cnt = cnt + jnp.where(x >= th, 1.0, 0.0)
            return cnt

        cnt = lax.fori_loop(0, n_tiles, count_tile, jnp.zeros((rows, lanes), F32))
        c = jnp.sum(cnt, axis=1, keepdims=True)
        ge = c >= float(TOPK)
        return jnp.where(ge, mid, lo), jnp.where(ge, hi, mid), jnp.where(ge, c_hi, c)

    lo0 = jnp.full((rows, 1), KEY_NEG_INF, I32)
    hi0 = jnp.full((rows, 1), KEY_POS_INF, I32)
    lo, _, c_gt = lax.fori_loop(0, 32, one_pass, (lo0, hi0, jnp.zeros((rows, 1), F32)))
    v = _key_to_float(lo)
    need = jnp.where(v == NEG_INF, 0.0, float(TOPK) - c_gt)
    return v, need


def _topk_mask(x, v, need, off, tri):
    eq = x == v
    cum = jnp.dot(jnp.where(eq, 1.0, 0.0).astype(tri.dtype), tri, preferred_element_type=F32)
    sel = (x > v) | (eq & ((cum + off) <= need))
    w = x.shape[1]
    return sel, off + cum[:, w - 1:w]


def _pattn_body(qi_ref, kiw_ref, q_ref, kit_ref, kt_ref, v_ref, tri_ref, o_ref,
                sc_ref, m_ref, l_ref, acc_ref, *, first_block):
    i = pl.program_id(0)
    row0 = i * TQ

    @pl.when(i < first_block)
    def _():
        o_ref[...] = jnp.zeros_like(o_ref)

    @pl.when(i >= first_block)
    def _():
        n_kt = (row0 + TQ + TK - 1) // TK
        w = kiw_ref[:, IDX_DIM:IDX_DIM + IDX_HEADS] * (IDX_HEADS ** -0.5 * IDX_DIM ** -0.5)
        qi_h = [qi_ref[:, IDX_DIM * h:IDX_DIM * (h + 1)] for h in range(IDX_HEADS)]

        def score_tile(j, carry):
            c0 = pl.multiple_of(j * TK, TK)
            kit = kit_ref[:, pl.ds(c0, TK)]
            tot = None
            for h in range(IDX_HEADS):
                z = jnp.dot(qi_h[h], kit, preferred_element_type=F32)
                r = jnp.maximum(z, 0.0) * w[:, h:h + 1]
                tot = r if tot is None else tot + r
            s_idx = c0 + lax.broadcasted_iota(I32, (TQ, TK), 1)
            t_idx = row0 + lax.broadcasted_iota(I32, (TQ, TK), 0)
            valid = (s_idx <= t_idx) & (s_idx >= PADF)
            sc_ref[:, pl.ds(c0, TK)] = jnp.where(valid, tot, NEG_INF)
            return carry

        lax.fori_loop(0, n_kt, score_tile, 0)

        v, need = _kth_largest(sc_ref, n_kt, TK, TQ)

        m_ref[...] = jnp.full_like(m_ref, NEG_INF)
        l_ref[...] = jnp.zeros_like(l_ref)
        acc_ref[...] = jnp.zeros_like(acc_ref)

        def attend_tile(j, off):
            c0 = pl.multiple_of(j * TK, TK)
            sel, off = _topk_mask(sc_ref[:, pl.ds(c0, TK)], v, need, off, tri_ref[...])
            bias = jnp.where(sel, 0.0, NEG_INF)
            for h in range(N_HEADS):
                hs = slice(HEAD_DIM * h, HEAD_DIM * (h + 1))
                s = jnp.dot(q_ref[:, hs], kt_ref[hs, pl.ds(c0, TK)], preferred_element_type=F32) + bias
                m_old = m_ref[h]
                m_new = jnp.maximum(m_old, jnp.max(s, axis=1, keepdims=True))
                m_safe = jnp.where(m_new == NEG_INF, 0.0, m_new)
                p = jnp.exp(s - m_safe)
                alpha = jnp.exp(m_old - m_safe)
                l_ref[h] = alpha * l_ref[h] + jnp.sum(p, axis=1, keepdims=True)
                pv = jnp.dot(p.astype(BF16), v_ref[pl.ds(c0, TK), hs], preferred_element_type=F32)
                acc_ref[:, hs] = alpha * acc_ref[:, hs] + pv
                m_ref[h] = m_new
            return off

        lax.fori_loop(0, n_kt, attend_tile, jnp.zeros((TQ, 1), F32))

        for h in range(N_HEADS):
            hs = slice(HEAD_DIM * h, HEAD_DIM * (h + 1))
            l = l_ref[h]
            o_ref[:, hs] = (acc_ref[:, hs] * (1.0 / jnp.where(l > 0.0, l, 1.0))).astype(o_ref.dtype)


def _prompt_attention(qi, kiw, q, kit, kt, vb, tri):
    n_rows = q.shape[0]

    def row(width):
        return pl.BlockSpec((TQ, width), lambda i: (i, 0))

    return pl.pallas_call(
        functools.partial(_pattn_body, first_block=PADF // TQ),
        grid=(n_rows // TQ,),
        in_specs=[row(IDX_HEADS * IDX_DIM), row(128), row(ATT_WIDTH),
                  _resident_spec(kit.shape), _resident_spec(kt.shape), _resident_spec(vb.shape),
                  _resident_spec(tri.shape)],
        out_specs=row(ATT_WIDTH),
        out_shape=jax.ShapeDtypeStruct((n_rows, ATT_WIDTH), BF16),
        scratch_shapes=[pltpu.VMEM((TQ, n_rows), F32), pltpu.VMEM((N_HEADS, TQ, 1), F32),
                        pltpu.VMEM((N_HEADS, TQ, 1), F32), pltpu.VMEM((TQ, ATT_WIDTH), F32)],
        compiler_params=_cparams(1, VMEM_LIMIT),
        name="prompt_attention",
    )(qi, kiw, q, kit, kt, vb, tri)


PAGES_PER_STEP = 8
S_CHUNK = PAGES_PER_STEP * PAGE
S_ROWS = 8


def _idx_scores_rows(qi, w, keys_bf16):
    z = lax.dot_general(qi, keys_bf16, (((1,), (1,)), ((), ())), preferred_element_type=F32)
    r = jnp.maximum(z, 0.0) * w
    tot = r[0:S_ROWS]
    for h in range(1, IDX_HEADS):
        tot = tot + r[S_ROWS * h:S_ROWS * (h + 1)]
    return tot


def _sscore_body(pt_ref, qi_ref, w_ref, kinew_ref, *refs, n_chunks, past, sel_tile):
    page_refs = refs[:PAGES_PER_STEP]
    sc_out_ref, par_ref, sc_ref = refs[PAGES_PER_STEP:]
    c = pl.program_id(1)
    qi = qi_ref[...]
    w = w_ref[...]
    for r in range(PAGES_PER_STEP):
        tot = _idx_scores_rows(qi, w, page_refs[r][...].astype(BF16))
        sc_ref[:, pl.ds(pl.multiple_of(c * S_CHUNK + r * PAGE, PAGE), PAGE)] = tot

    @pl.when(c == n_chunks - 1)
    def _():
        tot = _idx_scores_rows(qi, w, kinew_ref[...])
        lane = lax.broadcasted_iota(I32, (S_ROWS, PAGE), 1)
        row = lax.broadcasted_iota(I32, (S_ROWS, PAGE), 0)
        sc_ref[:, past:past + PAGE] = jnp.where(lane <= row, tot, NEG_INF)
        v, need = _kth_largest(sc_ref, (past + PAGE) // sel_tile, sel_tile, S_ROWS)
        sc_out_ref[...] = sc_ref[...]
        par_ref[...] = jnp.where(lane == 0, v, jnp.where(lane == 1, need, 0.0))


def _sample_scores(page_table, qi_s, w_s, kinew, cache_kidx):
    n_b, n_pages = page_table.shape
    n_chunks = n_pages // PAGES_PER_STEP
    past = n_pages * PAGE
    width = past + PAGE
    sel_tile = max(t for t in range(PAGE, 6 * PAGE, PAGE) if width % t == 0)

    def page_spec(r):
        return pl.BlockSpec((None, PAGE, IDX_DIM), lambda b, c, pt: (pt[b, c * PAGES_PER_STEP + r], 0, 0))

    def per_b(shape):
        return pl.BlockSpec((None,) + shape, lambda b, c, pt: (b, 0, 0))

    grid_spec = pltpu.PrefetchScalarGridSpec(
        num_scalar_prefetch=1,
        grid=(n_b, n_chunks),
        in_specs=[per_b((IDX_HEADS * S_ROWS, IDX_DIM)), per_b((IDX_HEADS * S_ROWS, 1)), per_b((PAGE, IDX_DIM))]
                 + [page_spec(r) for r in range(PAGES_PER_STEP)],
        out_specs=[per_b((S_ROWS, width)), per_b((S_ROWS, PAGE))],
        scratch_shapes=[pltpu.VMEM((S_ROWS, width), F32)],
    )
    return pl.pallas_call(
        functools.partial(_sscore_body, n_chunks=n_chunks, past=past, sel_tile=sel_tile),
        grid_spec=grid_spec,
        out_shape=[jax.ShapeDtypeStruct((n_b, S_ROWS, width), F32),
                   jax.ShapeDtypeStruct((n_b, S_ROWS, PAGE), F32)],
        compiler_params=_cparams(2, VMEM_LIMIT),
        name="sample_scores",
    )(page_table, qi_s, w_s, kinew, *([cache_kidx] * PAGES_PER_STEP))


S_TIE = 256


def _sattn_body(pt_ref, q_ref, sc_ref, scn_ref, par_ref, knew_ref, vnew_ref, tri_ref, *refs, n_chunks):
    k_refs = refs[:PAGES_PER_STEP]
    v_refs = refs[PAGES_PER_STEP:2 * PAGES_PER_STEP]
    o_ref, qbd_ref, kc_ref, vc_ref, m_ref, l_ref, acc_ref, off_ref = refs[2 * PAGES_PER_STEP:]
    c = pl.program_id(1)
    n_q = 4
    n_rows = n_q * N_HEADS
    head_of_col = lax.broadcasted_iota(I32, (N_HEADS, ATT_WIDTH), 1) // HEAD_DIM
    own_head = head_of_col == lax.broadcasted_iota(I32, (N_HEADS, ATT_WIDTH), 0)

    @pl.when(c == 0)
    def _():
        for s in range(n_q):
            qs = jnp.broadcast_to(q_ref[s:s + 1, :], (N_HEADS, ATT_WIDTH))
            qbd_ref[N_HEADS * s:N_HEADS * (s + 1), :] = jnp.where(own_head, qs, 0.0)
        m_ref[...] = jnp.full_like(m_ref, NEG_INF)
        l_ref[...] = jnp.zeros_like(l_ref)
        acc_ref[...] = jnp.zeros_like(acc_ref)
        off_ref[...] = jnp.zeros_like(off_ref)

    v_thr = par_ref[:, 0:1]
    need = par_ref[:, 1:2]
    qbd = qbd_ref[...].astype(BF16)

    def attend(kmat, vmat, x):
        width = x.shape[1]
        off = off_ref[...]
        parts = []
        for a in range(0, width, S_TIE):
            wa = min(S_TIE, width - a)
            sel, off = _topk_mask(x[:, a:a + wa], v_thr, need, off, tri_ref[0:wa, 0:wa])
            parts.append(jnp.where(sel, 0.0, NEG_INF))
        off_ref[...] = off
        bias8 = parts[0] if len(parts) == 1 else jnp.concatenate(parts, axis=1)
        bias = jnp.concatenate([jnp.broadcast_to(bias8[s:s + 1, :], (N_HEADS, width)) for s in range(n_q)], axis=0)
        s_mat = lax.dot_general(qbd, kmat, (((1,), (1,)), ((), ())), preferred_element_type=F32) + bias
        m_old = m_ref[...]
        m_new = jnp.maximum(m_old, jnp.max(s_mat, axis=1, keepdims=True))
        m_safe = jnp.where(m_new == NEG_INF, 0.0, m_new)
        p = jnp.exp(s_mat - m_safe)
        alpha = jnp.exp(m_old - m_safe)
        l_ref[...] = alpha * l_ref[...] + jnp.sum(p, axis=1, keepdims=True)
        acc_ref[...] = alpha * acc_ref[...] + jnp.dot(p.astype(BF16), vmat, preferred_element_type=F32)
        m_ref[...] = m_new

    for r in range(PAGES_PER_STEP):
        kc_ref[r * PAGE:(r + 1) * PAGE, :] = k_refs[r][...].astype(BF16)
        vc_ref[r * PAGE:(r + 1) * PAGE, :] = v_refs[r][...].astype(BF16)
    attend(kc_ref[...], vc_ref[...], sc_ref[...])

    @pl.when(c == n_chunks - 1)
    def _():
        pad = jnp.zeros((PAGE - S_ROWS, ATT_WIDTH), F32)
        attend(jnp.concatenate([knew_ref[...], pad], axis=0).astype(BF16),
               jnp.concatenate([vnew_ref[...], pad], axis=0).astype(BF16), scn_ref[...])
        l = l_ref[...]
        o_all = acc_ref[...] * (1.0 / jnp.where(l > 0.0, l, 1.0))
        row = lax.broadcasted_iota(I32, (S_ROWS, ATT_WIDTH), 0)
        out = jnp.zeros((S_ROWS, ATT_WIDTH), F32)
        for s in range(n_q):
            o_s = jnp.sum(jnp.where(own_head, o_all[N_HEADS * s:N_HEADS * (s + 1), :], 0.0), axis=0, keepdims=True)
            out = jnp.where(row == s, jnp.broadcast_to(o_s, (S_ROWS, ATT_WIDTH)), out)
        o_ref[...] = out


def _sample_attention(page_table, q_s, scores, par, knew, vnew, tri, cache_k, cache_v):
    n_b, n_pages = page_table.shape
    n_chunks = n_pages // PAGES_PER_STEP
    n_rows = 4 * N_HEADS

    def page_spec(r):
        return pl.BlockSpec((None, PAGE, ATT_WIDTH), lambda b, c, pt: (pt[b, c * PAGES_PER_STEP + r], 0, 0))

    def per_b(shape):
        return pl.BlockSpec((None,) + shape, lambda b, c, pt: (b, 0, 0))

    grid_spec = pltpu.PrefetchScalarGridSpec(
        num_scalar_prefetch=1,
        grid=(n_b, n_chunks),
        in_specs=[per_b((S_ROWS, ATT_WIDTH)),
                  pl.BlockSpec((None, S_ROWS, S_CHUNK), lambda b, c, pt: (b, 0, c)),
                  pl.BlockSpec((None, S_ROWS, PAGE), lambda b, c, pt: (b, 0, n_pages)),
                  per_b((S_ROWS, PAGE)), per_b((S_ROWS, ATT_WIDTH)), per_b((S_ROWS, ATT_WIDTH)),
                  pl.BlockSpec((S_TIE, S_TIE), lambda b, c, pt: (0, 0))]
                 + [page_spec(r) for r in range(PAGES_PER_STEP)] * 2,
        out_specs=per_b((S_ROWS, ATT_WIDTH)),
        scratch_shapes=[pltpu.VMEM((n_rows, ATT_WIDTH), F32),
                        pltpu.VMEM((S_CHUNK, ATT_WIDTH), BF16), pltpu.VMEM((S_CHUNK, ATT_WIDTH), BF16),
                        pltpu.VMEM((n_rows, 1), F32), pltpu.VMEM((n_rows, 1), F32),
                        pltpu.VMEM((n_rows, ATT_WIDTH), F32), pltpu.VMEM((S_ROWS, 1), F32)],
    )
    return pl.pallas_call(
        functools.partial(_sattn_body, n_chunks=n_chunks),
        grid_spec=grid_spec,
        out_shape=jax.ShapeDtypeStruct((n_b, S_ROWS, ATT_WIDTH), F32),
        compiler_params=_cparams(2, VMEM_LIMIT),
        name="sample_attention",
    )(page_table, q_s, scores, scores, par, knew, vnew, tri,
      *([cache_k] * PAGES_PER_STEP), *([cache_v] * PAGES_PER_STEP))


R_LANES = 128
R_E0 = N_EGROUPS


def _route(logits):
    lane = lax.broadcasted_iota(I32, logits.shape, 1)
    lane_f = lane.astype(F32)
    big = float(R_LANES)
    is_g = lane < N_EGROUPS
    gl = jnp.where(is_g, logits, NEG_INF)
    g_max = jnp.max(gl, axis=1, keepdims=True)
    g_sel = jnp.min(jnp.where(gl == g_max, lane_f, big), axis=1, keepdims=True)
    p_g = 1.0 / jnp.sum(jnp.exp(gl - g_max), axis=1, keepdims=True)
    member = (lane >= R_E0) & (lane < R_E0 + N_EXPERTS) & (((lane - R_E0) >> 2).astype(F32) == g_sel)
    el = jnp.where(member, logits, NEG_INF)
    e_max = jnp.max(el, axis=1, keepdims=True)
    pe = jnp.exp(el - e_max)
    pe = pe / jnp.sum(pe, axis=1, keepdims=True)
    p1 = jnp.max(pe, axis=1, keepdims=True)
    i1 = jnp.min(jnp.where(member & (pe == p1), lane_f, big), axis=1, keepdims=True)
    rest = member & (lane_f != i1)
    p2 = jnp.max(jnp.where(rest, pe, -1.0), axis=1, keepdims=True)
    i2 = jnp.min(jnp.where(rest & (pe == p2), lane_f, big), axis=1, keepdims=True)
    tot = p1 + p2
    return jnp.where(lane_f == i1, p1 / tot * p_g, 0.0) + jnp.where(lane_f == i2, p2 / tot * p_g, 0.0)


def _mix_body(x_ref, h0_ref, ys_ref, oa_ref, ga_ref, gb_ref, wso_ref, wao_ref, wo_ref, g1_ref, b1_ref,
              wrh_ref, wrl_ref, br_ref, h1_ref, h1b_ref, comb_ref, *, n_head_tiles):
    if n_head_tiles:
        h = jnp.where(pl.program_id(0) < n_head_tiles, h0_ref[...], x_ref[...])
    else:
        h = x_ref[...]
    a = jnp.dot(ys_ref[...], wso_ref[...], preferred_element_type=F32)
    b = jnp.dot(oa_ref[...], wao_ref[...], preferred_element_type=F32)
    m = _sigmoid(ga_ref[...]) * a + _sigmoid(gb_ref[...]) * b
    mix = jnp.dot(m.astype(BF16), wo_ref[...], preferred_element_type=F32)
    h1 = _layernorm(DN_ALPHA * h + mix, g1_ref[...], b1_ref[...])
    h1_ref[...] = h1
    h_hi = h1.astype(BF16)
    h1b_ref[...] = h_hi
    h_lo = (h1 - h_hi.astype(F32)).astype(BF16)
    logits = (jnp.dot(h_hi, wrh_ref[...], preferred_element_type=F32)
              + (jnp.dot(h_lo, wrh_ref[...], preferred_element_type=F32)
                 + jnp.dot(h_hi, wrl_ref[...], preferred_element_type=F32))) + br_ref[...]
    comb_ref[...] = _route(logits)


def _mix_ln_route(x, h0, ys, oa, ga, gb, wso, wao, wo, g1, b1, wrh, wrl, br, *, tm, head_rows):
    n_head_tiles = head_rows // tm
    n_rows = head_rows + x.shape[0]
    if n_head_tiles:
        x_spec = pl.BlockSpec((tm, D_MODEL), lambda i: (jnp.maximum(i - n_head_tiles, 0), 0))
        h_spec = pl.BlockSpec((tm, D_MODEL), lambda i: (jnp.minimum(i, n_head_tiles - 1), 0))
    else:
        h0 = x
        x_spec = pl.BlockSpec((tm, D_MODEL), lambda i: (i, 0))
        h_spec = pl.BlockSpec((tm, D_MODEL), lambda i: (0, 0))

    def row(width):
        return pl.BlockSpec((tm, width), lambda i: (i, 0))

    consts = (wso, wao, wo, g1, b1, wrh, wrl, br)
    return pl.pallas_call(
        functools.partial(_mix_body, n_head_tiles=n_head_tiles),
        grid=(n_rows // tm,),
        in_specs=[x_spec, h_spec, row(SSM_WIDTH), row(ATT_WIDTH), row(D_MODEL), row(D_MODEL)]
                 + [_const_spec(c.shape) for c in consts],
        out_specs=[row(D_MODEL), row(D_MODEL), row(R_LANES)],
        out_shape=[jax.ShapeDtypeStruct((n_rows, D_MODEL), F32), jax.ShapeDtypeStruct((n_rows, D_MODEL), BF16),
                   jax.ShapeDtypeStruct((n_rows, R_LANES), F32)],
        compiler_params=_cparams(1, VMEM_LIMIT),
        name="mix_ln_route",
    )(x, h0, ys, oa, ga, gb, *consts)


def _moe_body(h1_ref, h1b_ref, comb_ref, wgu_ref, wd_ref, g2_ref, b2_ref, y_ref, acc_ref):
    e = pl.program_id(1)

    @pl.when(e == 0)
    def _():
        acc_ref[...] = jnp.zeros_like(acc_ref)

    gu = jnp.dot(h1b_ref[...], wgu_ref[...], preferred_element_type=F32)
    gate = gu[:, 0:D_EXPERT]
    up = gu[:, D_EXPERT:2 * D_EXPERT]
    act = gate * _sigmoid(gate) * up
    comb = comb_ref[...]
    lane = lax.broadcasted_iota(I32, comb.shape, 1)
    c_e = jnp.sum(jnp.where(lane == e + R_E0, comb, 0.0), axis=1, keepdims=True)
    acc_ref[...] += jnp.dot((act * c_e).astype(BF16), wd_ref[...], preferred_element_type=F32)

    @pl.when(e == N_EXPERTS - 1)
    def _():
        y_ref[...] = _layernorm(DN_ALPHA * h1_ref[...] + acc_ref[...], g2_ref[...], b2_ref[...])


def _moe_ln(h1, h1b, comb, wgu, wd, g2, b2, *, tm, head_rows):
    n_head_tiles = head_rows // tm
    n_rows = h1.shape[0]
    out_rows = n_rows - head_rows

    def row(width):
        return pl.BlockSpec((tm, width), lambda i, e: (i, 0))

    return pl.pallas_call(
        _moe_body,
        grid=(n_rows // tm, N_EXPERTS),
        in_specs=[row(D_MODEL), row(D_MODEL), row(R_LANES),
                  pl.BlockSpec((None, D_MODEL, 2 * D_EXPERT), lambda i, e: (e, 0, 0)),
                  pl.BlockSpec((None, D_EXPERT, D_MODEL), lambda i, e: (e, 0, 0)),
                  pl.BlockSpec((1, D_MODEL), lambda i, e: (0, 0)), pl.BlockSpec((1, D_MODEL), lambda i, e: (0, 0))],
        out_specs=pl.BlockSpec((tm, D_MODEL), lambda i, e: (jnp.maximum(i - n_head_tiles, 0), 0)),
        out_shape=jax.ShapeDtypeStruct((out_rows, D_MODEL), F32),
        scratch_shapes=[pltpu.VMEM((tm, D_MODEL), F32)],
        compiler_params=_cparams(2, VMEM_LIMIT),
        name="moe_ln",
    )(h1, h1b, comb, wgu, wd, g2, b2)


def _ssm_tables(a_re, a_im, log_dt, b_re, b_im, c_re, c_im):
    dt = jnp.exp(log_dt)[:, None]
    mag = jnp.exp(a_re * dt)
    ab_re = mag * jnp.cos(a_im * dt)
    ab_im = mag * jnp.sin(a_im * dt)
    den = a_re * a_re + a_im * a_im
    nr = ab_re - 1.0
    f_re = (nr * a_re + ab_im * a_im) / den
    f_im = (ab_im * a_re - nr * a_im) / den
    bb_re = f_re[..., None] * b_re - f_im[..., None] * b_im
    bb_im = f_re[..., None] * b_im + f_im[..., None] * b_re
    eye = jnp.eye(SSM_GROUPS, dtype=F32)

    def in_mat(bb):
        return jnp.einsum("gpm,gh->gmhp", bb, eye).reshape(SSM_WIDTH, N_STATE)

    def out_mat(cc):
        return jnp.einsum("gmp,gh->gphm", cc, eye).reshape(N_STATE, SSM_WIDTH)

    bbig = jnp.concatenate([in_mat(bb_re), in_mat(bb_im)], axis=1)
    cbig = jnp.concatenate([out_mat(c_re), -out_mat(c_im)], axis=0)
    ar = ab_re.reshape(1, N_STATE)
    ai = ab_im.reshape(1, N_STATE)

    def cmul(x, y):
        return x[0] * y[0] - x[1] * y[1], x[0] * y[1] + x[1] * y[0]

    a1 = (ar, ai)
    pows = [a1]
    for _ in range(7):
        pows.append(cmul(pows[-1], a1))
    row = jnp.arange(8)[:, None]
    tabs = []
    for sh, pw in ((1, pows[0]), (2, pows[1]), (4, pows[3])):
        keep = (row >= sh).astype(F32)
        tabs += [keep * pw[0], keep * pw[1]]
    tabs += [jnp.concatenate([p[0] for p in pows], axis=0), jnp.concatenate([p[1] for p in pows], axis=0)]
    apw = jnp.stack(tabs)
    a_pair = jnp.concatenate([ar, ai], axis=0)
    return bbig, cbig, apw, a_pair


def _split_bf16(x):
    hi = x.astype(BF16)
    return hi, (x - hi.astype(F32)).astype(BF16)


def kernel(x_prompt, x_sample, cache_k, cache_v, cache_kidx, state_ssm_re, state_ssm_im, page_table,
           meta_tokens, w_in, ssm_a_re, ssm_a_im, ssm_log_dt, ssm_b_re, ssm_b_im, ssm_c_re, ssm_c_im,
           ssm_d, w_glu, w_ssm_out, w_att_out, w_o, ln1_g, ln1_b, w_route_group, b_route_group,
           w_route_expert, b_route_expert, w_exp_gate, w_exp_up, w_exp_down, ln2_g, ln2_b):
    depth = w_in.shape[0]
    assert depth == 1 and x_prompt.shape[0] == 1
    n_b, n_s, _ = x_sample.shape
    seq = x_prompt.shape[1]
    t_len = seq + N_META
    lyr = 0

    cuts = [0]
    for c in IN_SPLITS:
        cuts.append(cuts[-1] + c)
    w = w_in[lyr]
    wp = jnp.concatenate([w[:, :cuts[7]], jnp.zeros((D_MODEL, C_GA - C_KIW - IDX_DIM - IDX_HEADS), F32),
                          w[:, cuts[7]:]], axis=1).astype(BF16)
    bbig, cbig, apw, a_pair = _ssm_tables(ssm_a_re[lyr], ssm_a_im[lyr], ssm_log_dt[lyr], ssm_b_re[lyr],
                                          ssm_b_im[lyr], ssm_c_re[lyr], ssm_c_im[lyr])
    bbig_hi, bbig_lo = _split_bf16(bbig)
    cbig_b = cbig.astype(BF16)
    dskip = ssm_d[lyr].reshape(1, SSM_WIDTH)
    wglu_b = w_glu[lyr].astype(BF16)
    wso = w_ssm_out[lyr].astype(BF16)
    wao = w_att_out[lyr].astype(BF16)
    wo = w_o[lyr].astype(BF16)
    g1 = ln1_g[lyr].reshape(1, D_MODEL)
    b1 = ln1_b[lyr].reshape(1, D_MODEL)
    g2 = ln2_g[lyr].reshape(1, D_MODEL)
    b2 = ln2_b[lyr].reshape(1, D_MODEL)
    r_pad = R_LANES - N_EGROUPS - N_EXPERTS
    wr = jnp.concatenate([w_route_group[lyr], w_route_expert[lyr], jnp.zeros((D_MODEL, r_pad), F32)], axis=1)
    wrh, wrl = _split_bf16(wr)
    br = jnp.concatenate([b_route_group[lyr], b_route_expert[lyr], jnp.zeros((r_pad,), F32)]).reshape(1, R_LANES)
    wgu = jnp.concatenate([w_exp_gate[lyr], w_exp_up[lyr]], axis=2).astype(BF16)
    wd = w_exp_down[lyr].astype(BF16)
    tri = (jnp.arange(TK)[:, None] <= jnp.arange(TK)[None, :])

    xp = x_prompt[0]
    head = jnp.concatenate([jnp.zeros((PADF, D_MODEL), F32), meta_tokens.astype(F32)], axis=0)
    u, q, k, v, kb, vb, qi, kiw, ga, gb = _in_proj(xp, head, wp, tm=256, head_rows=ROW_TILE)
    y_ssm, st = _ssm_prompt(u, bbig_hi, apw, cbig_b, dskip, wglu_b)
    kit = kiw[:, :IDX_DIM].astype(BF16).T
    o_att = _prompt_attention(qi, kiw, q, kit, kb.T, vb, tri.astype(BF16))
    h1, h1b, comb = _mix_ln_route(xp, head, y_ssm, o_att, ga, gb, wso, wao, wo, g1, b1, wrh, wrl, br,
                                  tm=ROW_TILE, head_rows=ROW_TILE)
    y_prompt = _moe_ln(h1, h1b, comb, wgu, wd, g2, b2, tm=ROW_TILE, head_rows=ROW_TILE)

    n_tok = n_b * n_s
    xs = x_sample.reshape(n_tok, D_MODEL)
    us, qs, ks, vs, _, _, qis, kiws, gas, gbs = _in_proj(xs, None, wp, tm=n_tok, head_rows=0)
    u_s = us.reshape(n_b, n_s, SSM_WIDTH).transpose(1, 0, 2)
    ys_s, hre_s, him_s = _ssm_sample(u_s, state_ssm_re[lyr].reshape(n_b, N_STATE),
                                     state_ssm_im[lyr].reshape(n_b, N_STATE),
                                     bbig_hi, bbig_lo, a_pair, cbig_b, dskip, wglu_b)
    ys_s = ys_s.transpose(1, 0, 2).reshape(n_tok, SSM_WIDTH)

    def pad_rows(a):
        return jnp.pad(a, [(0, 0), (0, S_ROWS - n_s)] + [(0, 0)] * (a.ndim - 2))

    qi4 = pad_rows(qis.reshape(n_b, n_s, IDX_HEADS, IDX_DIM)).transpose(0, 2, 1, 3)
    qi_s = qi4.reshape(n_b, IDX_HEADS * S_ROWS, IDX_DIM)
    w4 = kiws[:, IDX_DIM:IDX_DIM + IDX_HEADS] * (IDX_HEADS ** -0.5 * IDX_DIM ** -0.5)
    w_s = pad_rows(w4.reshape(n_b, n_s, IDX_HEADS)).transpose(0, 2, 1).reshape(n_b, IDX_HEADS * S_ROWS, 1)
    ki_new = kiws[:, :IDX_DIM].reshape(n_b, n_s, IDX_DIM)
    kinew = jnp.pad(ki_new, [(0, 0), (0, PAGE - n_s), (0, 0)]).astype(BF16)
    n_phys = cache_k.shape[1]
    scores, par = _sample_scores(page_table, qi_s, w_s, kinew, cache_kidx[lyr])
    o_s = _sample_attention(page_table, pad_rows(qs.astype(F32).reshape(n_b, n_s, ATT_WIDTH)), scores, par,
                            pad_rows(ks.reshape(n_b, n_s, ATT_WIDTH)), pad_rows(vs.reshape(n_b, n_s, ATT_WIDTH)),
                            tri[:S_TIE, :S_TIE].astype(F32),
                            cache_k[lyr].reshape(n_phys, PAGE, ATT_WIDTH), cache_v[lyr].reshape(n_phys, PAGE, ATT_WIDTH))
    o_s = o_s[:, :n_s].reshape(n_tok, ATT_WIDTH).astype(BF16)
    h1s, h1bs, combs = _mix_ln_route(xs, None, ys_s, o_s, gas, gbs, wso, wao, wo, g1, b1, wrh, wrl, br,
                                     tm=n_tok, head_rows=0)
    y_sample = _moe_ln(h1s, h1bs, combs, wgu, wd, g2, b2, tm=n_tok, head_rows=0)

    def heads(a, lead):
        return a.reshape((depth,) + lead + (N_HEADS, HEAD_DIM))

    return (y_prompt.reshape(1, seq, D_MODEL),
            y_sample.reshape(n_b, n_s, D_MODEL),
            heads(k[PADF:], (1, t_len)), heads(v[PADF:], (1, t_len)),
            kiw[PADF:, :IDX_DIM].reshape(depth, 1, t_len, IDX_DIM),
            st[0].reshape(depth, 1, SSM_GROUPS, SSM_STATE), st[1].reshape(depth, 1, SSM_GROUPS, SSM_STATE),
            heads(ks, (n_b, n_s)), heads(vs, (n_b, n_s)),
            kiws[:, :IDX_DIM].reshape(depth, n_b, n_s, IDX_DIM),
            hre_s.reshape(depth, n_b, SSM_GROUPS, SSM_STATE), him_s.reshape(depth, n_b, SSM_GROUPS, SSM_STATE))
```

```python
import functools
import math

import jax
import jax.numpy as jnp
from jax import lax
from jax.experimental import pallas as pl
from jax.experimental.pallas import tpu as pltpu

F32 = jnp.float32
BF16 = jnp.bfloat16
I32 = jnp.int32

D_MODEL = 1024
N_META = 16
SSM_WIDTH = 512
SSM_GROUP_CH = 16
SSM_GROUPS = 32
SSM_STATE = 64
N_STATE = SSM_GROUPS * SSM_STATE
N_HEADS = 8
HEAD_DIM = 64
ATT_WIDTH = N_HEADS * HEAD_DIM
IDX_HEADS = 4
IDX_DIM = 64
TOPK = 256
PAGE = 128
N_EGROUPS = 4
EXPERTS_PER_GROUP = 4
N_EXPERTS = 16
D_EXPERT = 256
DN_ALPHA = 2.0 ** 0.25
LN_EPS = 1e-5
NEG_INF = float("-inf")

C_U, C_Q, C_K, C_V, C_QI, C_KIW, C_GA, C_GB, C_END = 0, 512, 1024, 1536, 2048, 2304, 2432, 3456, 4480
IN_SPLITS = (512, 512, 512, 512, 256, 64, 4, 1024, 1024)

ROW_TILE = 512
HEAD_ROWS = 1024
PADF = HEAD_ROWS - N_META
TQ = 128
TK = 1024
LANES = 128
HEAD_KEY0 = HEAD_ROWS - LANES
TIE_W = 256
M_INIT = -1e30
VMEM_LIMIT = 56 * 1024 * 1024

KEY_NEG_INF = -2139095041
KEY_POS_INF = 2139095040


def _cparams(n_axes, vmem=None):
    return pltpu.CompilerParams(dimension_semantics=("arbitrary",) * n_axes, vmem_limit_bytes=vmem)


def _const_spec(shape):
    nd = len(shape)
    return pl.BlockSpec(shape, lambda *_: (0,) * nd)


def _resident_spec(shape):
    nd = len(shape)
    return pl.BlockSpec(shape, lambda *_: (0,) * nd, pipeline_mode=pl.Buffered(1))


def _sigmoid(x):
    return 1.0 / (1.0 + jnp.exp(-x))


def _gelu_tanh(x):
    return 0.5 * x * (1.0 + jnp.tanh(math.sqrt(2.0 / math.pi) * (x + 0.044715 * (x * x * x))))


def _layernorm(z, g, b):
    mu = jnp.mean(z, axis=-1, keepdims=True)
    zc = z - mu
    var = jnp.mean(zc * zc, axis=-1, keepdims=True)
    return zc * lax.rsqrt(var + LN_EPS) * g + b


def _inproj_body(x_ref, h0_ref, w_ref, u_ref, q_ref, k_ref, v_ref, kb_ref, vb_ref, qi_ref, kiw_ref,
                 ga_ref, gb_ref, *, n_head_tiles):
    if n_head_tiles:
        h = jnp.where(pl.program_id(0) < n_head_tiles, h0_ref[...], x_ref[...])
    else:
        h = x_ref[...]
    hb = h.astype(BF16)

    def seg(a, b):
        return jnp.dot(hb, w_ref[:, a:b], preferred_element_type=F32)

    u_ref[...] = seg(C_U, C_Q)
    q_ref[...] = (seg(C_Q, C_K) * HEAD_DIM ** -0.5).astype(BF16)
    k = seg(C_K, C_V)
    k_ref[...] = k
    kb_ref[...] = k.astype(BF16)
    v = seg(C_V, C_QI)
    v_ref[...] = v
    vb_ref[...] = v.astype(BF16)
    qi_ref[...] = seg(C_QI, C_KIW).astype(BF16)
    kiw_ref[...] = seg(C_KIW, C_GA)
    ga_ref[...] = seg(C_GA, C_GB)
    gb_ref[...] = seg(C_GB, C_END)


def _in_proj(x, h0, wp, *, tm, head_rows):
    n_head_tiles = head_rows // tm
    n_rows = head_rows + x.shape[0]
    grid = (n_rows // tm,)
    if n_head_tiles:
        x_spec = pl.BlockSpec((tm, D_MODEL), lambda i: (jnp.maximum(i - n_head_tiles, 0), 0))
        h_spec = pl.BlockSpec((tm, D_MODEL), lambda i: (jnp.minimum(i, n_head_tiles - 1), 0))
    else:
        h0 = x
        x_spec = pl.BlockSpec((tm, D_MODEL), lambda i: (i, 0))
        h_spec = pl.BlockSpec((tm, D_MODEL), lambda i: (0, 0))

    def row(width):
        return pl.BlockSpec((tm, width), lambda i: (i, 0))

    widths = (512, 512, 512, 512, 512, 512, 256, 128, 1024, 1024)
    dtypes = (F32, BF16, F32, F32, BF16, BF16, BF16, F32, F32, F32)
    return pl.pallas_call(
        functools.partial(_inproj_body, n_head_tiles=n_head_tiles),
        grid=grid,
        in_specs=[x_spec, h_spec, _resident_spec((D_MODEL, C_END))],
        out_specs=[row(w) for w in widths],
        out_shape=[jax.ShapeDtypeStruct((n_rows, w), d) for w, d in zip(widths, dtypes)],
        compiler_params=_cparams(1, VMEM_LIMIT),
        name="in_proj",
    )(x, h0, wp)


SSM_CHUNK = 256


def _ssm_out(h_cat, u, cbig_ref, d_ref, wglu_ref):
    y_lin = jnp.dot(h_cat.astype(BF16), cbig_ref[...], preferred_element_type=F32)
    y = _gelu_tanh(y_lin + d_ref[...] * u)
    z = jnp.dot(y.astype(BF16), wglu_ref[...], preferred_element_type=F32)
    return y * _sigmoid(z)


def _ssm_prompt_body(u_ref, bbig_ref, apw_ref, cbig_ref, d_ref, wglu_ref, y_ref, st_ref,
                     bu_ref, hs_ref, hc_ref):
    @pl.when(pl.program_id(0) == 0)
    def _():
        hc_ref[...] = jnp.zeros_like(hc_ref)

    u = u_ref[...]
    bu_ref[...] = jnp.dot(u.astype(BF16), bbig_ref[...], preferred_element_type=F32)

    def step(r, carry):
        h_re, h_im = carry
        r8 = pl.multiple_of(r * 8, 8)
        x_re = bu_ref[pl.ds(r8, 8), 0:N_STATE]
        x_im = bu_ref[pl.ds(r8, 8), N_STATE:2 * N_STATE]
        for lvl, sh in enumerate((1, 2, 4)):
            a_re = apw_ref[2 * lvl]
            a_im = apw_ref[2 * lvl + 1]
            s_re = pltpu.roll(x_re, sh, 0)
            s_im = pltpu.roll(x_im, sh, 0)
            x_re, x_im = (x_re + (a_re * s_re - a_im * s_im), x_im + (a_re * s_im + a_im * s_re))
        c_re = apw_ref[6]
        c_im = apw_ref[7]
        x_re, x_im = (x_re + (c_re * h_re - c_im * h_im), x_im + (c_re * h_im + c_im * h_re))
        hs_ref[pl.ds(r8, 8), 0:N_STATE] = x_re
        hs_ref[pl.ds(r8, 8), N_STATE:2 * N_STATE] = x_im
        return x_re[7:8, :], x_im[7:8, :]

    h_re, h_im = lax.fori_loop(0, SSM_CHUNK // 8, step, (hc_ref[0:1, :], hc_ref[1:2, :]))
    hc_ref[0:1, :] = h_re
    hc_ref[1:2, :] = h_im
    st_ref[...] = hc_ref[...]
    y_ref[...] = _ssm_out(hs_ref[...], u, cbig_ref, d_ref, wglu_ref).astype(BF16)


def _ssm_prompt(u, bbig, apw, cbig, dskip, wglu):
    n_rows = u.shape[0]
    return pl.pallas_call(
        _ssm_prompt_body,
        grid=(n_rows // SSM_CHUNK,),
        in_specs=[pl.BlockSpec((SSM_CHUNK, SSM_WIDTH), lambda i: (i, 0)),
                  _const_spec((SSM_WIDTH, 2 * N_STATE)), _const_spec((8, 8, N_STATE)),
                  _const_spec((2 * N_STATE, SSM_WIDTH)), _const_spec((1, SSM_WIDTH)),
                  _const_spec((SSM_WIDTH, SSM_WIDTH))],
        out_specs=[pl.BlockSpec((SSM_CHUNK, SSM_WIDTH), lambda i: (i, 0)), _const_spec((2, N_STATE))],
        out_shape=[jax.ShapeDtypeStruct((n_rows, SSM_WIDTH), BF16),
                   jax.ShapeDtypeStruct((2, N_STATE), F32)],
        scratch_shapes=[pltpu.VMEM((SSM_CHUNK, 2 * N_STATE), F32), pltpu.VMEM((SSM_CHUNK, 2 * N_STATE), F32),
                        pltpu.VMEM((2, N_STATE), F32)],
        compiler_params=_cparams(1, VMEM_LIMIT),
        name="ssm_prompt",
    )(u, bbig, apw, cbig, dskip, wglu)


def _ssm_sample_body(u_ref, h0re_ref, h0im_ref, bhi_ref, blo_ref, a_ref, cbig_ref, d_ref, wglu_ref,
                     y_ref, hre_ref, him_ref):
    a_re = a_ref[0:1, :]
    a_im = a_ref[1:2, :]
    h_re = h0re_ref[...]
    h_im = h0im_ref[...]
    for s in range(u_ref.shape[0]):
        u = u_ref[s]
        u_hi = u.astype(BF16)
        u_lo = (u - u_hi.astype(F32)).astype(BF16)
        bu = (jnp.dot(u_hi, bhi_ref[...], preferred_element_type=F32)
              + (jnp.dot(u_lo, bhi_ref[...], preferred_element_type=F32)
                 + jnp.dot(u_hi, blo_ref[...], preferred_element_type=F32)))
        h_re, h_im = (a_re * h_re - a_im * h_im + bu[:, 0:N_STATE],
                      a_re * h_im + a_im * h_re + bu[:, N_STATE:2 * N_STATE])
        h_cat = jnp.concatenate([h_re, h_im], axis=1)
        y_ref[s] = _ssm_out(h_cat, u, cbig_ref, d_ref, wglu_ref).astype(BF16)
    hre_ref[...] = h_re
    him_ref[...] = h_im


def _ssm_sample(u_s, h0_re, h0_im, bhi, blo, a_pair, cbig, dskip, wglu):
    n_s, n_b, _ = u_s.shape
    return pl.pallas_call(
        _ssm_sample_body,
        grid=(1,),
        in_specs=[_const_spec(u_s.shape), _const_spec(h0_re.shape), _const_spec(h0_im.shape),
                  _const_spec(bhi.shape), _const_spec(blo.shape), _const_spec(a_pair.shape),
                  _const_spec(cbig.shape), _const_spec(dskip.shape), _const_spec(wglu.shape)],
        out_specs=[_const_spec((n_s, n_b, SSM_WIDTH)), _const_spec((n_b, N_STATE)), _const_spec((n_b, N_STATE))],
        out_shape=[jax.ShapeDtypeStruct((n_s, n_b, SSM_WIDTH), BF16),
                   jax.ShapeDtypeStruct((n_b, N_STATE), F32), jax.ShapeDtypeStruct((n_b, N_STATE), F32)],
        compiler_params=_cparams(1, VMEM_LIMIT),
        name="ssm_sample",
    )(u_s, h0_re, h0_im, bhi, blo, a_pair, cbig, dskip, wglu)


def _key_to_float(key):
    bits = key ^ ((key >> 31) & 0x7FFFFFFF)
    return lax.bitcast_convert_type(bits, F32)


def _float_to_key(x):
    bits = lax.bitcast_convert_type(x, I32)
    return bits ^ ((bits >> 31) & 0x7FFFFFFF)


BISECT_EVERY = 8
SELECT_MAX_PASSES = 32 * BISECT_EVERY + 8


def _kth_largest(count_fn, n_all, rows, row_max):
    kk = float(TOPK)

    def total(th_col, strict):
        return jnp.sum(count_fn(jnp.broadcast_to(th_col, (rows, LANES)), strict), axis=1, keepdims=True)

    zero = jnp.zeros((rows, 1), F32)
    c_ge0 = total(zero, False)
    c_gt0 = total(zero, True)
    pos = c_gt0 > kk
    at0 = (c_ge0 >= kk) & (c_gt0 <= kk)
    lo0 = jnp.where(pos | at0, 0, KEY_NEG_INF).astype(I32)
    hi0 = jnp.where(pos, _float_to_key(row_max) + 1, jnp.where(at0, 0, -1)).astype(I32)
    c_lo0 = jnp.where(pos, c_gt0, jnp.where(at0, c_ge0, n_all))
    c_hi0 = jnp.where(pos, 0.0, jnp.where(at0, c_gt0, c_ge0))
    one = jnp.ones((rows, 1), F32)

    def mid_key(lo, hi):
        return (lo >> 1) + (hi >> 1) + (lo & hi & 1)

    def finished(lo, hi, c_lo):
        return (c_lo == kk) | (mid_key(lo, hi) == lo)

    def cond(st):
        it, lo, hi, c_lo = st[0], st[1], st[2], st[3]
        open_rows = jnp.where(finished(lo, hi, c_lo), 0.0, 1.0)
        return (it < SELECT_MAX_PASSES) & (jnp.max(open_rows) > 0.0)

    def body(st):
        it, lo, hi, c_lo, c_hi, w_lo, w_hi, side = st
        done = finished(lo, hi, c_lo)
        lo_v = _key_to_float(lo)
        hi_v = _key_to_float(hi)
        g_lo = jnp.log(c_lo * (1.0 / (kk - 0.5))) * w_lo
        g_hi = jnp.log(jnp.maximum(c_hi, 0.5) * (1.0 / (kk - 0.5))) * w_hi
        th_i = lo_v + (g_lo / (g_lo - g_hi)) * (hi_v - lo_v)
        bounded = (lo_v > NEG_INF) & (hi_v < float("inf"))
        k_i = jnp.minimum(jnp.maximum(_float_to_key(th_i), lo + 1), hi - 1)
        k_mid = mid_key(lo, hi)
        probe = jnp.where((it & (BISECT_EVERY - 1)) == BISECT_EVERY - 1, k_mid, jnp.where(bounded, k_i, k_mid))
        c = total(_key_to_float(probe), False)
        ge = c >= kk
        mv_lo = (~done) & ge
        mv_hi = (~done) & (~ge)
        w_hi = jnp.where(mv_lo, jnp.where(side > 0.0, w_hi * 0.5, one), jnp.where(mv_hi, one, w_hi))
        w_lo = jnp.where(mv_hi, jnp.where(side < 0.0, w_lo * 0.5, one), jnp.where(mv_lo, one, w_lo))
        side = jnp.where(mv_lo, 1.0, jnp.where(mv_hi, -1.0, side))
        return (it + 1, jnp.where(mv_lo, probe, lo), jnp.where(mv_hi, probe, hi),
                jnp.where(mv_lo, c, c_lo), jnp.where(mv_hi, c, c_hi), w_lo, w_hi, side)

    st = lax.while_loop(cond, body, (jnp.int32(0), lo0, hi0, c_lo0, c_hi0, one, one, zero))
    lo, c_lo, c_hi = st[1], st[3], st[4]
    v = _key_to_float(lo)
    need = jnp.where(v == NEG_INF, 0.0, jnp.where(c_lo == kk, kk, kk - c_hi))
    return v, need


def _topk_mask(x, v, need, off, tri):
    eq = x == v
    cum = jnp.dot(jnp.where(eq, 1.0, 0.0).astype(tri.dtype), tri, preferred_element_type=F32)
    sel = (x > v) | (eq & ((cum + off) <= need))
    w = x.shape[1]
    return sel, off + cum[:, w - 1:w]


def _pattn_body(qi_ref, kiw_ref, q_ref, kit_ref, kt_ref, v_ref, tri_ref, o_ref,
                sc_ref, m_ref, l_ref, acc_ref, mx_ref, *, first_block):
    i = pl.program_id(0)
    row0 = i * TQ

    @pl.when(i < first_block)
    def _():
        o_ref[...] = jnp.zeros_like(o_ref)

    @pl.when(i >= first_block)
    def _():
        jd = row0 // TK
        w = kiw_ref[:, IDX_DIM:IDX_DIM + IDX_HEADS] * (IDX_HEADS ** -0.5 * IDX_DIM ** -0.5)
        qi_h = [qi_ref[:, IDX_DIM * h:IDX_DIM * (h + 1)] for h in range(IDX_HEADS)]
        q_h = [q_ref[:, HEAD_DIM * h:HEAD_DIM * (h + 1)] for h in range(N_HEADS)]

        def score_tile(c0, width, masked, mx):
            kit = kit_ref[:, pl.ds(c0, width)]
            tot = None
            for h in range(IDX_HEADS):
                z = jnp.dot(qi_h[h], kit, preferred_element_type=F32)
                r = jnp.maximum(z, 0.0) * w[:, h:h + 1]
                tot = r if tot is None else tot + r
            if masked:
                s_idx = c0 + lax.broadcasted_iota(I32, (TQ, width), 1)
                t_idx = row0 + lax.broadcasted_iota(I32, (TQ, width), 0)
                tot = jnp.where((s_idx <= t_idx) & (s_idx >= PADF), tot, NEG_INF)
            sc_ref[:, pl.ds(c0, width)] = tot
            for a in range(width // LANES):
                mx = jnp.maximum(mx, tot[:, LANES * a:LANES * (a + 1)])
            return mx

        def main_start(j):
            return pl.multiple_of(j * TK, TK)

        mx = score_tile(HEAD_KEY0, LANES, True, jnp.full((TQ, LANES), NEG_INF, F32))
        mx = lax.fori_loop(1, jd, lambda j, mx: score_tile(main_start(j), TK, False, mx), mx)
        mx_ref[...] = mx

        @pl.when(jd >= 1)
        def _():
            mx_ref[...] = score_tile(main_start(jd), TK, True, mx)

        row_max = jnp.max(mx_ref[...], axis=1, keepdims=True)

        def count_fn(th, strict):
            def add(cnt, x):
                return cnt + jnp.where((x > th) if strict else (x >= th), 1.0, 0.0)

            def count_main(j, cnt):
                c0 = main_start(j)
                for a in range(TK // LANES):
                    cnt = add(cnt, sc_ref[:, pl.ds(c0 + a * LANES, LANES)])
                return cnt

            cnt = add(jnp.zeros((TQ, LANES), F32), sc_ref[:, HEAD_KEY0:HEAD_ROWS])
            return lax.fori_loop(1, jd + 1, count_main, cnt)

        v, need = _kth_largest(count_fn, (LANES + TK * jd).astype(F32), TQ, row_max)

        m_ref[...] = jnp.full_like(m_ref, M_INIT)
        l_ref[...] = jnp.zeros_like(l_ref)
        acc_ref[...] = jnp.zeros_like(acc_ref)
        first_half = lax.broadcasted_iota(I32, (TQ, LANES), 1) < HEAD_DIM

        def attend_tile(c0, width, off):
            x = sc_ref[:, pl.ds(c0, width)]
            parts = []
            for a in range(0, width, TIE_W):
                wa = min(TIE_W, width - a)
                sel, off = _topk_mask(x[:, a:a + wa], v, need, off, tri_ref[0:wa, 0:wa])
                parts.append(jnp.where(sel, 0.0, NEG_INF))
            bias = parts[0] if len(parts) == 1 else jnp.concatenate(parts, axis=1)
            n_sl = width // LANES

            def probs(h):
                kt = kt_ref[HEAD_DIM * h:HEAD_DIM * (h + 1), pl.ds(c0, width)]
                s = jnp.dot(q_h[h], kt, preferred_element_type=F32) + bias
                sl = [s[:, LANES * a:LANES * (a + 1)] for a in range(n_sl)]
                pm = sl[0]
                for a in range(1, n_sl):
                    pm = jnp.maximum(pm, sl[a])
                m_old = m_ref[h]
                m_new = jnp.maximum(m_old, jnp.max(pm, axis=1, keepdims=True))
                alpha = jnp.exp(m_old - m_new)
                ps = [jnp.exp(sl[a] - m_new) for a in range(n_sl)]
                psum = ps[0]
                for a in range(1, n_sl):
                    psum = psum + ps[a]
                l_ref[h] = alpha * l_ref[h] + psum
                m_ref[h] = m_new
                return (ps[0] if n_sl == 1 else jnp.concatenate(ps, axis=1)).astype(BF16), alpha

            for j in range(N_HEADS // 2):
                p_e, alpha_e = probs(2 * j)
                p_o, alpha_o = probs(2 * j + 1)
                cols = slice(LANES * j, LANES * (j + 1))
                pv = jnp.dot(jnp.concatenate([p_e, p_o], axis=0), v_ref[pl.ds(c0, width), cols],
                             preferred_element_type=F32)
                a_pair = jnp.where(first_half, alpha_e, alpha_o)
                pv_pair = jnp.where(first_half, pv[0:TQ], pv[TQ:2 * TQ])
                acc_ref[:, cols] = a_pair * acc_ref[:, cols] + pv_pair
            return off

        off = attend_tile(HEAD_KEY0, LANES, jnp.zeros((TQ, 1), F32))
        lax.fori_loop(1, jd + 1, lambda j, off: attend_tile(main_start(j), TK, off), off)

        for j in range(N_HEADS // 2):
            l_e = jnp.sum(l_ref[2 * j], axis=1, keepdims=True)
            l_o = jnp.sum(l_ref[2 * j + 1], axis=1, keepdims=True)
            l_pair = jnp.where(first_half, l_e, l_o)
            inv = 1.0 / jnp.where(l_pair > 0.0, l_pair, 1.0)
            o_ref[:, LANES * j:LANES * (j + 1)] = (acc_ref[:, LANES * j:LANES * (j + 1)] * inv).astype(o_ref.dtype)


def _prompt_attention(qi, kiw, q, kit, kt, vb, tri):
    n_rows = q.shape[0]

    def row(width):
        return pl.BlockSpec((TQ, width), lambda i: (i, 0))

    return pl.pallas_call(
        functools.partial(_pattn_body, first_block=HEAD_KEY0 // TQ),
        grid=(n_rows // TQ,),
        in_specs=[row(IDX_HEADS * IDX_DIM), row(128), row(ATT_WIDTH),
                  _resident_spec(kit.shape), _resident_spec(kt.shape), _resident_spec(vb.shape),
                  _resident_spec(tri.shape)],
        out_specs=row(ATT_WIDTH),
        out_shape=jax.ShapeDtypeStruct((n_rows, ATT_WIDTH), BF16),
        scratch_shapes=[pltpu.VMEM((TQ, n_rows), F32), pltpu.VMEM((N_HEADS, TQ, LANES), F32),
                        pltpu.VMEM((N_HEADS, TQ, LANES), F32), pltpu.VMEM((TQ, ATT_WIDTH), F32),
                        pltpu.VMEM((TQ, LANES), F32)],
        compiler_params=_cparams(1, VMEM_LIMIT),
        name="prompt_attention",
    )(qi, kiw, q, kit, kt, vb, tri)


PAGES_PER_STEP = 16
S_CHUNK = PAGES_PER_STEP * PAGE
S_ROWS = 8


def _idx_scores_rows(qi, w, keys_bf16):
    z = lax.dot_general(qi, keys_bf16, (((1,), (1,)), ((), ())), preferred_element_type=F32)
    r = jnp.maximum(z, 0.0) * w
    tot = r[0:S_ROWS]
    for h in range(1, IDX_HEADS):
        tot = tot + r[S_ROWS * h:S_ROWS * (h + 1)]
    return tot


def _sscore_body(pt_ref, qi_ref, w_ref, kinew_ref, *refs, n_chunks, past, sel_tile):
    page_refs = refs[:PAGES_PER_STEP]
    sc_out_ref, par_ref, sc_ref, kic_ref = refs[PAGES_PER_STEP:]
    c = pl.program_id(1)
    qi = qi_ref[...]
    w = w_ref[...]
    for r in range(PAGES_PER_STEP):
        kic_ref[:, r * PAGE:(r + 1) * PAGE] = page_refs[r][...].astype(BF16)
    rr = jnp.maximum(jnp.dot(qi, kic_ref[...], preferred_element_type=F32), 0.0) * w
    tot = rr[0:S_ROWS]
    for h in range(1, IDX_HEADS):
        tot = tot + rr[S_ROWS * h:S_ROWS * (h + 1)]
    sc_ref[:, pl.ds(pl.multiple_of(c * S_CHUNK, S_CHUNK), S_CHUNK)] = tot

    @pl.when(c == n_chunks - 1)
    def _():
        tot = _idx_scores_rows(qi, w, kinew_ref[...])
        lane = lax.broadcasted_iota(I32, (S_ROWS, PAGE), 1)
        row = lax.broadcasted_iota(I32, (S_ROWS, PAGE), 0)
        sc_ref[:, past:past + PAGE] = jnp.where(lane <= row, tot, NEG_INF)

        def count_fn(th, strict):
            def count_tile(j, cnt):
                c0 = pl.multiple_of(j * sel_tile, LANES)
                for a in range(sel_tile // LANES):
                    x = sc_ref[:, pl.ds(c0 + a * LANES, LANES)]
                    cnt = cnt + jnp.where((x > th) if strict else (x >= th), 1.0, 0.0)
                return cnt

            return lax.fori_loop(0, (past + PAGE) // sel_tile, count_tile, jnp.zeros((S_ROWS, LANES), F32))

        def max_tile(j, mx):
            c0 = pl.multiple_of(j * sel_tile, LANES)
            for a in range(sel_tile // LANES):
                mx = jnp.maximum(mx, sc_ref[:, pl.ds(c0 + a * LANES, LANES)])
            return mx

        mx = lax.fori_loop(0, (past + PAGE) // sel_tile, max_tile, jnp.full((S_ROWS, LANES), NEG_INF, F32))
        v, need = _kth_largest(count_fn, float(past + PAGE), S_ROWS, jnp.max(mx, axis=1, keepdims=True))
        sc_out_ref[...] = sc_ref[...]
        par_ref[...] = jnp.where(lane == 0, v, jnp.where(lane == 1, need, 0.0))


def _sample_scores(page_table, qi_s, w_s, kinew, cache_kidx):
    n_b, n_pages = page_table.shape
    n_chunks = n_pages // PAGES_PER_STEP
    past = n_pages * PAGE
    width = past + PAGE
    sel_tile = max(t for t in range(PAGE, 6 * PAGE, PAGE) if width % t == 0)

    def page_spec(r):
        return pl.BlockSpec((None, IDX_DIM, PAGE), lambda b, c, pt: (pt[b, c * PAGES_PER_STEP + r], 0, 0))

    def per_b(shape):
        return pl.BlockSpec((None,) + shape, lambda b, c, pt: (b, 0, 0))

    grid_spec = pltpu.PrefetchScalarGridSpec(
        num_scalar_prefetch=1,
        grid=(n_b, n_chunks),
        in_specs=[per_b((IDX_HEADS * S_ROWS, IDX_DIM)), per_b((IDX_HEADS * S_ROWS, 1)), per_b((PAGE, IDX_DIM))]
                 + [page_spec(r) for r in range(PAGES_PER_STEP)],
        out_specs=[per_b((S_ROWS, width)), per_b((S_ROWS, PAGE))],
        scratch_shapes=[pltpu.VMEM((S_ROWS, width), F32), pltpu.VMEM((IDX_DIM, S_CHUNK), BF16)],
    )
    return pl.pallas_call(
        functools.partial(_sscore_body, n_chunks=n_chunks, past=past, sel_tile=sel_tile),
        grid_spec=grid_spec,
        out_shape=[jax.ShapeDtypeStruct((n_b, S_ROWS, width), F32),
                   jax.ShapeDtypeStruct((n_b, S_ROWS, PAGE), F32)],
        compiler_params=_cparams(2, VMEM_LIMIT),
        name="sample_scores",
    )(page_table, qi_s, w_s, kinew, *([cache_kidx] * PAGES_PER_STEP))


S_TIE = 256


def _sattn_body(pt_ref, q_ref, sc_ref, scn_ref, par_ref, knew_ref, vnew_ref, tri_ref, *refs, n_chunks):
    k_refs = refs[:PAGES_PER_STEP]
    v_refs = refs[PAGES_PER_STEP:2 * PAGES_PER_STEP]
    o_ref, qbd_ref, kc_ref, vc_ref, m_ref, l_ref, acc_ref, off_ref = refs[2 * PAGES_PER_STEP:]
    c = pl.program_id(1)
    n_q = 4
    n_rows = n_q * N_HEADS
    head_of_col = lax.broadcasted_iota(I32, (N_HEADS, ATT_WIDTH), 1) // HEAD_DIM
    own_head = head_of_col == lax.broadcasted_iota(I32, (N_HEADS, ATT_WIDTH), 0)

    @pl.when(c == 0)
    def _():
        for s in range(n_q):
            qs = jnp.broadcast_to(q_ref[s:s + 1, :], (N_HEADS, ATT_WIDTH))
            qbd_ref[N_HEADS * s:N_HEADS * (s + 1), :] = jnp.where(own_head, qs, 0.0)
        m_ref[...] = jnp.full_like(m_ref, M_INIT)
        l_ref[...] = jnp.zeros_like(l_ref)
        acc_ref[...] = jnp.zeros_like(acc_ref)
        off_ref[...] = jnp.zeros_like(off_ref)

    v_thr = par_ref[:, 0:1]
    need = par_ref[:, 1:2]
    qbd = qbd_ref[...].astype(BF16)

    nt_dims = (((1,), (1,)), ((), ()))

    def attend(s_raw, pv_fn, x):
        width = x.shape[1]
        off = off_ref[...]
        parts = []
        for a in range(0, width, S_TIE):
            wa = min(S_TIE, width - a)
            sel, off = _topk_mask(x[:, a:a + wa], v_thr, need, off, tri_ref[0:wa, 0:wa])
            parts.append(jnp.where(sel, 0.0, NEG_INF))
        off_ref[...] = off
        bias8 = parts[0] if len(parts) == 1 else jnp.concatenate(parts, axis=1)
        bias = jnp.concatenate([jnp.broadcast_to(bias8[s:s + 1, :], (N_HEADS, width)) for s in range(n_q)], axis=0)
        s_mat = s_raw + bias
        m_old = m_ref[...]
        m_new = jnp.maximum(m_old, jnp.max(s_mat, axis=1, keepdims=True))
        p = jnp.exp(s_mat - m_new)
        alpha = jnp.exp(m_old - m_new)
        l_ref[...] = alpha * l_ref[...] + jnp.sum(p, axis=1, keepdims=True)
        acc_ref[...] = alpha * acc_ref[...] + pv_fn(p.astype(BF16))
        m_ref[...] = m_new

    for r in range(PAGES_PER_STEP):
        kc_ref[:, r * PAGE:(r + 1) * PAGE] = k_refs[r][...].astype(BF16)
        vc_ref[:, r * PAGE:(r + 1) * PAGE] = v_refs[r][...].astype(BF16)
    attend(jnp.dot(qbd, kc_ref[...], preferred_element_type=F32),
           lambda p: lax.dot_general(p, vc_ref[...], nt_dims, preferred_element_type=F32), sc_ref[...])

    @pl.when(c == n_chunks - 1)
    def _():
        pad = jnp.zeros((PAGE - S_ROWS, ATT_WIDTH), F32)
        k_new = jnp.concatenate([knew_ref[...], pad], axis=0).astype(BF16)
        v_new = jnp.concatenate([vnew_ref[...], pad], axis=0).astype(BF16)
        attend(lax.dot_general(qbd, k_new, nt_dims, preferred_element_type=F32),
               lambda p: jnp.dot(p, v_new, preferred_element_type=F32), scn_ref[...])
        l = l_ref[...]
        o_all = acc_ref[...] * (1.0 / jnp.where(l > 0.0, l, 1.0))
        row = lax.broadcasted_iota(I32, (S_ROWS, ATT_WIDTH), 0)
        out = jnp.zeros((S_ROWS, ATT_WIDTH), F32)
        for s in range(n_q):
            o_s = jnp.sum(jnp.where(own_head, o_all[N_HEADS * s:N_HEADS * (s + 1), :], 0.0), axis=0, keepdims=True)
            out = jnp.where(row == s, jnp.broadcast_to(o_s, (S_ROWS, ATT_WIDTH)), out)
        o_ref[...] = out


def _sample_attention(page_table, q_s, scores, par, knew, vnew, tri, cache_k, cache_v):
    n_b, n_pages = page_table.shape
    n_chunks = n_pages // PAGES_PER_STEP
    n_rows = 4 * N_HEADS

    def page_spec(r):
        return pl.BlockSpec((None, ATT_WIDTH, PAGE), lambda b, c, pt: (pt[b, c * PAGES_PER_STEP + r], 0, 0))

    def per_b(shape):
        return pl.BlockSpec((None,) + shape, lambda b, c, pt: (b, 0, 0))

    grid_spec = pltpu.PrefetchScalarGridSpec(
        num_scalar_prefetch=1,
        grid=(n_b, n_chunks),
        in_specs=[per_b((S_ROWS, ATT_WIDTH)),
                  pl.BlockSpec((None, S_ROWS, S_CHUNK), lambda b, c, pt: (b, 0, c)),
                  pl.BlockSpec((None, S_ROWS, PAGE), lambda b, c, pt: (b, 0, n_pages)),
                  per_b((S_ROWS, PAGE)), per_b((S_ROWS, ATT_WIDTH)), per_b((S_ROWS, ATT_WIDTH)),
                  pl.BlockSpec((S_TIE, S_TIE), lambda b, c, pt: (0, 0))]
                 + [page_spec(r) for r in range(PAGES_PER_STEP)] * 2,
        out_specs=per_b((S_ROWS, ATT_WIDTH)),
        scratch_shapes=[pltpu.VMEM((n_rows, ATT_WIDTH), F32),
                        pltpu.VMEM((ATT_WIDTH, S_CHUNK), BF16), pltpu.VMEM((ATT_WIDTH, S_CHUNK), BF16),
                        pltpu.VMEM((n_rows, 1), F32), pltpu.VMEM((n_rows, 1), F32),
                        pltpu.VMEM((n_rows, ATT_WIDTH), F32), pltpu.VMEM((S_ROWS, 1), F32)],
    )
    return pl.pallas_call(
        functools.partial(_sattn_body, n_chunks=n_chunks),
        grid_spec=grid_spec,
        out_shape=jax.ShapeDtypeStruct((n_b, S_ROWS, ATT_WIDTH), F32),
        compiler_params=_cparams(2, VMEM_LIMIT),
        name="sample_attention",
    )(page_table, q_s, scores, scores, par, knew, vnew, tri,
      *([cache_k] * PAGES_PER_STEP), *([cache_v] * PAGES_PER_STEP))


R_LANES = 128
R_E0 = N_EGROUPS


def _route(logits):
    lane = lax.broadcasted_iota(I32, logits.shape, 1)
    lane_f = lane.astype(F32)
    big = float(R_LANES)
    is_g = lane < N_EGROUPS
    gl = jnp.where(is_g, logits, NEG_INF)
    g_max = jnp.max(gl, axis=1, keepdims=True)
    g_sel = jnp.min(jnp.where(gl == g_max, lane_f, big), axis=1, keepdims=True)
    p_g = 1.0 / jnp.sum(jnp.exp(gl - g_max), axis=1, keepdims=True)
    member = (lane >= R_E0) & (lane < R_E0 + N_EXPERTS) & (((lane - R_E0) >> 2).astype(F32) == g_sel)
    el = jnp.where(member, logits, NEG_INF)
    e_max = jnp.max(el, axis=1, keepdims=True)
    pe = jnp.exp(el - e_max)
    pe = pe / jnp.sum(pe, axis=1, keepdims=True)
    p1 = jnp.max(pe, axis=1, keepdims=True)
    i1 = jnp.min(jnp.where(member & (pe == p1), lane_f, big), axis=1, keepdims=True)
    rest = member & (lane_f != i1)
    p2 = jnp.max(jnp.where(rest, pe, -1.0), axis=1, keepdims=True)
    i2 = jnp.min(jnp.where(rest & (pe == p2), lane_f, big), axis=1, keepdims=True)
    tot = p1 + p2
    return jnp.where(lane_f == i1, p1 / tot * p_g, 0.0) + jnp.where(lane_f == i2, p2 / tot * p_g, 0.0)


def _mix_body(x_ref, h0_ref, ys_ref, oa_ref, ga_ref, gb_ref, wso_ref, wao_ref, wo_ref, g1_ref, b1_ref,
              wrh_ref, wrl_ref, br_ref, h1_ref, h1b_ref, comb_ref, *, n_head_tiles):
    if n_head_tiles:
        h = jnp.where(pl.program_id(0) < n_head_tiles, h0_ref[...], x_ref[...])
    else:
        h = x_ref[...]
    a = jnp.dot(ys_ref[...], wso_ref[...], preferred_element_type=F32)
    b = jnp.dot(oa_ref[...], wao_ref[...], preferred_element_type=F32)
    m = _sigmoid(ga_ref[...]) * a + _sigmoid(gb_ref[...]) * b
    mix = jnp.dot(m.astype(BF16), wo_ref[...], preferred_element_type=F32)
    h1 = _layernorm(DN_ALPHA * h + mix, g1_ref[...], b1_ref[...])
    h1_ref[...] = h1
    h_hi = h1.astype(BF16)
    h1b_ref[...] = h_hi
    h_lo = (h1 - h_hi.astype(F32)).astype(BF16)
    logits = (jnp.dot(h_hi, wrh_ref[...], preferred_element_type=F32)
              + (jnp.dot(h_lo, wrh_ref[...], preferred_element_type=F32)
                 + jnp.dot(h_hi, wrl_ref[...], preferred_element_type=F32))) + br_ref[...]
    comb_ref[...] = _route(logits)


def _mix_ln_route(x, h0, ys, oa, ga, gb, wso, wao, wo, g1, b1, wrh, wrl, br, *, tm, head_rows):
    n_head_tiles = head_rows // tm
    n_rows = head_rows + x.shape[0]
    if n_head_tiles:
        x_spec = pl.BlockSpec((tm, D_MODEL), lambda i: (jnp.maximum(i - n_head_tiles, 0), 0))
        h_spec = pl.BlockSpec((tm, D_MODEL), lambda i: (jnp.minimum(i, n_head_tiles - 1), 0))
    else:
        h0 = x
        x_spec = pl.BlockSpec((tm, D_MODEL), lambda i: (i, 0))
        h_spec = pl.BlockSpec((tm, D_MODEL), lambda i: (0, 0))

    def row(width):
        return pl.BlockSpec((tm, width), lambda i: (i, 0))

    consts = (wso, wao, wo, g1, b1, wrh, wrl, br)
    return pl.pallas_call(
        functools.partial(_mix_body, n_head_tiles=n_head_tiles),
        grid=(n_rows // tm,),
        in_specs=[x_spec, h_spec, row(SSM_WIDTH), row(ATT_WIDTH), row(D_MODEL), row(D_MODEL)]
                 + [_const_spec(c.shape) for c in consts],
        out_specs=[row(D_MODEL), row(D_MODEL), row(R_LANES)],
        out_shape=[jax.ShapeDtypeStruct((n_rows, D_MODEL), F32), jax.ShapeDtypeStruct((n_rows, D_MODEL), BF16),
                   jax.ShapeDtypeStruct((n_rows, R_LANES), F32)],
        compiler_params=_cparams(1, VMEM_LIMIT),
        name="mix_ln_route",
    )(x, h0, ys, oa, ga, gb, *consts)


def _moe_body(h1_ref, h1b_ref, comb_ref, wgu_ref, wd_ref, g2_ref, b2_ref, y_ref, acc_ref):
    e = pl.program_id(1)

    @pl.when(e == 0)
    def _():
        acc_ref[...] = jnp.zeros_like(acc_ref)

    gu = jnp.dot(h1b_ref[...], wgu_ref[...], preferred_element_type=F32)
    gate = gu[:, 0:D_EXPERT]
    up = gu[:, D_EXPERT:2 * D_EXPERT]
    act = gate * _sigmoid(gate) * up
    comb = comb_ref[...]
    lane = lax.broadcasted_iota(I32, comb.shape, 1)
    c_e = jnp.sum(jnp.where(lane == e + R_E0, comb, 0.0), axis=1, keepdims=True)
    acc_ref[...] += jnp.dot((act * c_e).astype(BF16), wd_ref[...], preferred_element_type=F32)

    @pl.when(e == N_EXPERTS - 1)
    def _():
        y_ref[...] = _layernorm(DN_ALPHA * h1_ref[...] + acc_ref[...], g2_ref[...], b2_ref[...])


def _moe_ln(h1, h1b, comb, wgu, wd, g2, b2, *, tm, head_rows):
    n_head_tiles = head_rows // tm
    n_rows = h1.shape[0]
    out_rows = n_rows - head_rows

    def row(width):
        return pl.BlockSpec((tm, width), lambda i, e: (i, 0))

    return pl.pallas_call(
        _moe_body,
        grid=(n_rows // tm, N_EXPERTS),
        in_specs=[row(D_MODEL), row(D_MODEL), row(R_LANES),
                  pl.BlockSpec((None, D_MODEL, 2 * D_EXPERT), lambda i, e: (e, 0, 0)),
                  pl.BlockSpec((None, D_EXPERT, D_MODEL), lambda i, e: (e, 0, 0)),
                  pl.BlockSpec((1, D_MODEL), lambda i, e: (0, 0)), pl.BlockSpec((1, D_MODEL), lambda i, e: (0, 0))],
        out_specs=pl.BlockSpec((tm, D_MODEL), lambda i, e: (jnp.maximum(i - n_head_tiles, 0), 0)),
        out_shape=jax.ShapeDtypeStruct((out_rows, D_MODEL), F32),
        scratch_shapes=[pltpu.VMEM((tm, D_MODEL), F32)],
        compiler_params=_cparams(2, VMEM_LIMIT),
        name="moe_ln",
    )(h1, h1b, comb, wgu, wd, g2, b2)


def _ssm_tables(a_re, a_im, log_dt, b_re, b_im, c_re, c_im):
    dt = jnp.exp(log_dt)[:, None]
    mag = jnp.exp(a_re * dt)
    ab_re = mag * jnp.cos(a_im * dt)
    ab_im = mag * jnp.sin(a_im * dt)
    den = a_re * a_re + a_im * a_im
    nr = ab_re - 1.0
    f_re = (nr * a_re + ab_im * a_im) / den
    f_im = (ab_im * a_re - nr * a_im) / den
    bb_re = f_re[..., None] * b_re - f_im[..., None] * b_im
    bb_im = f_re[..., None] * b_im + f_im[..., None] * b_re
    eye = jnp.eye(SSM_GROUPS, dtype=F32)

    def in_mat(bb):
        return jnp.einsum("gpm,gh->gmhp", bb, eye).reshape(SSM_WIDTH, N_STATE)

    def out_mat(cc):
        return jnp.einsum("gmp,gh->gphm", cc, eye).reshape(N_STATE, SSM_WIDTH)

    bbig = jnp.concatenate([in_mat(bb_re), in_mat(bb_im)], axis=1)
    cbig = jnp.concatenate([out_mat(c_re), -out_mat(c_im)], axis=0)
    ar = ab_re.reshape(1, N_STATE)
    ai = ab_im.reshape(1, N_STATE)

    def cmul(x, y):
        return x[0] * y[0] - x[1] * y[1], x[0] * y[1] + x[1] * y[0]

    a1 = (ar, ai)
    pows = [a1]
    for _ in range(7):
        pows.append(cmul(pows[-1], a1))
    row = jnp.arange(8)[:, None]
    tabs = []
    for sh, pw in ((1, pows[0]), (2, pows[1]), (4, pows[3])):
        keep = (row >= sh).astype(F32)
        tabs += [keep * pw[0], keep * pw[1]]
    tabs += [jnp.concatenate([p[0] for p in pows], axis=0), jnp.concatenate([p[1] for p in pows], axis=0)]
    apw = jnp.stack(tabs)
    a_pair = jnp.concatenate([ar, ai], axis=0)
    return bbig, cbig, apw, a_pair


def _split_bf16(x):
    hi = x.astype(BF16)
    return hi, (x - hi.astype(F32)).astype(BF16)


def kernel(x_prompt, x_sample, cache_k, cache_v, cache_kidx, state_ssm_re, state_ssm_im, page_table,
           meta_tokens, w_in, ssm_a_re, ssm_a_im, ssm_log_dt, ssm_b_re, ssm_b_im, ssm_c_re, ssm_c_im,
           ssm_d, w_glu, w_ssm_out, w_att_out, w_o, ln1_g, ln1_b, w_route_group, b_route_group,
           w_route_expert, b_route_expert, w_exp_gate, w_exp_up, w_exp_down, ln2_g, ln2_b):
    depth = w_in.shape[0]
    assert depth == 1 and x_prompt.shape[0] == 1
    n_b, n_s, _ = x_sample.shape
    seq = x_prompt.shape[1]
    t_len = seq + N_META
    lyr = 0

    cuts = [0]
    for c in IN_SPLITS:
        cuts.append(cuts[-1] + c)
    w = w_in[lyr]
    wp = jnp.concatenate([w[:, :cuts[7]], jnp.zeros((D_MODEL, C_GA - C_KIW - IDX_DIM - IDX_HEADS), F32),
                          w[:, cuts[7]:]], axis=1).astype(BF16)
    bbig, cbig, apw, a_pair = _ssm_tables(ssm_a_re[lyr], ssm_a_im[lyr], ssm_log_dt[lyr], ssm_b_re[lyr],
                                          ssm_b_im[lyr], ssm_c_re[lyr], ssm_c_im[lyr])
    bbig_hi, bbig_lo = _split_bf16(bbig)
    cbig_b = cbig.astype(BF16)
    dskip = ssm_d[lyr].reshape(1, SSM_WIDTH)
    wglu_b = w_glu[lyr].astype(BF16)
    wso = w_ssm_out[lyr].astype(BF16)
    wao = w_att_out[lyr].astype(BF16)
    wo = w_o[lyr].astype(BF16)
    g1 = ln1_g[lyr].reshape(1, D_MODEL)
    b1 = ln1_b[lyr].reshape(1, D_MODEL)
    g2 = ln2_g[lyr].reshape(1, D_MODEL)
    b2 = ln2_b[lyr].reshape(1, D_MODEL)
    r_pad = R_LANES - N_EGROUPS - N_EXPERTS
    wr = jnp.concatenate([w_route_group[lyr], w_route_expert[lyr], jnp.zeros((D_MODEL, r_pad), F32)], axis=1)
    wrh, wrl = _split_bf16(wr)
    br = jnp.concatenate([b_route_group[lyr], b_route_expert[lyr], jnp.zeros((r_pad,), F32)]).reshape(1, R_LANES)
    wgu = jnp.concatenate([w_exp_gate[lyr], w_exp_up[lyr]], axis=2).astype(BF16)
    wd = w_exp_down[lyr].astype(BF16)
    tri = (jnp.arange(TIE_W)[:, None] <= jnp.arange(TIE_W)[None, :])

    xp = x_prompt[0]
    head = jnp.concatenate([jnp.zeros((PADF, D_MODEL), F32), meta_tokens.astype(F32)], axis=0)
    u, q, k, v, kb, vb, qi, kiw, ga, gb = _in_proj(xp, head, wp, tm=256, head_rows=HEAD_ROWS)
    y_ssm, st = _ssm_prompt(u, bbig_hi, apw, cbig_b, dskip, wglu_b)
    kit = kiw[:, :IDX_DIM].astype(BF16).T
    o_att = _prompt_attention(qi, kiw, q, kit, kb.T, vb, tri.astype(BF16))
    h1, h1b, comb = _mix_ln_route(xp, head, y_ssm, o_att, ga, gb, wso, wao, wo, g1, b1, wrh, wrl, br,
                                  tm=ROW_TILE, head_rows=HEAD_ROWS)
    y_prompt = _moe_ln(h1, h1b, comb, wgu, wd, g2, b2, tm=ROW_TILE, head_rows=HEAD_ROWS)

    n_tok = n_b * n_s
    xs = x_sample.reshape(n_tok, D_MODEL)
    us, qs, ks, vs, _, _, qis, kiws, gas, gbs = _in_proj(xs, None, wp, tm=n_tok, head_rows=0)
    u_s = us.reshape(n_b, n_s, SSM_WIDTH).transpose(1, 0, 2)
    ys_s, hre_s, him_s = _ssm_sample(u_s, state_ssm_re[lyr].reshape(n_b, N_STATE),
                                     state_ssm_im[lyr].reshape(n_b, N_STATE),
                                     bbig_hi, bbig_lo, a_pair, cbig_b, dskip, wglu_b)
    ys_s = ys_s.transpose(1, 0, 2).reshape(n_tok, SSM_WIDTH)

    def pad_rows(a):
        return jnp.pad(a, [(0, 0), (0, S_ROWS - n_s)] + [(0, 0)] * (a.ndim - 2))

    qi4 = pad_rows(qis.reshape(n_b, n_s, IDX_HEADS, IDX_DIM)).transpose(0, 2, 1, 3)
    qi_s = qi4.reshape(n_b, IDX_HEADS * S_ROWS, IDX_DIM)
    w4 = kiws[:, IDX_DIM:IDX_DIM + IDX_HEADS] * (IDX_HEADS ** -0.5 * IDX_DIM ** -0.5)
    w_s = pad_rows(w4.reshape(n_b, n_s, IDX_HEADS)).transpose(0, 2, 1).reshape(n_b, IDX_HEADS * S_ROWS, 1)
    ki_new = kiws[:, :IDX_DIM].reshape(n_b, n_s, IDX_DIM)
    kinew = jnp.pad(ki_new, [(0, 0), (0, PAGE - n_s), (0, 0)]).astype(BF16)
    n_phys = cache_k.shape[1]
    kidx_t = cache_kidx[lyr].transpose(0, 2, 1)
    ck_t = cache_k[lyr].transpose(0, 2, 3, 1).reshape(n_phys, ATT_WIDTH, PAGE)
    cv_t = cache_v[lyr].transpose(0, 2, 3, 1).reshape(n_phys, ATT_WIDTH, PAGE)
    scores, par = _sample_scores(page_table, qi_s, w_s, kinew, kidx_t)
    o_s = _sample_attention(page_table, pad_rows(qs.astype(F32).reshape(n_b, n_s, ATT_WIDTH)), scores, par,
                            pad_rows(ks.reshape(n_b, n_s, ATT_WIDTH)), pad_rows(vs.reshape(n_b, n_s, ATT_WIDTH)),
                            tri[:S_TIE, :S_TIE].astype(F32), ck_t, cv_t)
    o_s = o_s[:, :n_s].reshape(n_tok, ATT_WIDTH).astype(BF16)
    h1s, h1bs, combs = _mix_ln_route(xs, None, ys_s, o_s, gas, gbs, wso, wao, wo, g1, b1, wrh, wrl, br,
                                     tm=n_tok, head_rows=0)
    y_sample = _moe_ln(h1s, h1bs, combs, wgu, wd, g2, b2, tm=n_tok, head_rows=0)

    def heads(a, lead):
        return a.reshape((depth,) + lead + (N_HEADS, HEAD_DIM))

    return (y_prompt.reshape(1, seq, D_MODEL),
            y_sample.reshape(n_b, n_s, D_MODEL),
            heads(k[PADF:], (1, t_len)), heads(v[PADF:], (1, t_len)),
            kiw[PADF:, :IDX_DIM].reshape(depth, 1, t_len, IDX_DIM),
            st[0].reshape(depth, 1, SSM_GROUPS, SSM_STATE), st[1].reshape(depth, 1, SSM_GROUPS, SSM_STATE),
            heads(ks, (n_b, n_s)), heads(vs, (n_b, n_s)),
            kiws[:, :IDX_DIM].reshape(depth, n_b, n_s, IDX_DIM),
            hre_s.reshape(depth, n_b, SSM_GROUPS, SSM_STATE), him_s.reshape(depth, n_b, SSM_GROUPS, SSM_STATE))
```

```python
import functools
import math

import jax
import jax.numpy as jnp
from jax import lax
from jax.experimental import pallas as pl
from jax.experimental.pallas import tpu as pltpu

F32 = jnp.float32
BF16 = jnp.bfloat16
I32 = jnp.int32

D_MODEL = 1024
N_META = 16
SSM_WIDTH = 512
SSM_GROUP_CH = 16
SSM_GROUPS = 32
SSM_STATE = 64
N_STATE = SSM_GROUPS * SSM_STATE
N_HEADS = 8
HEAD_DIM = 64
ATT_WIDTH = N_HEADS * HEAD_DIM
IDX_HEADS = 4
IDX_DIM = 64
TOPK = 256
PAGE = 128
N_EGROUPS = 4
EXPERTS_PER_GROUP = 4
N_EXPERTS = 16
D_EXPERT = 256
DN_ALPHA = 2.0 ** 0.25
LN_EPS = 1e-5
NEG_INF = float("-inf")

C_U, C_Q, C_K, C_V, C_QI, C_KIW, C_GA, C_GB, C_END = 0, 512, 1024, 1536, 2048, 2304, 2432, 3456, 4480
IN_SPLITS = (512, 512, 512, 512, 256, 64, 4, 1024, 1024)

ROW_TILE = 512
HEAD_ROWS = 1024
PADF = HEAD_ROWS - N_META
TQ = 128
TK = 1024
LANES = 128
HEAD_KEY0 = HEAD_ROWS - LANES
TIE_W = 256
M_INIT = -1e30
VMEM_LIMIT = 56 * 1024 * 1024

KEY_NEG_INF = -2139095041
KEY_POS_INF = 2139095040


def _cparams(n_axes, vmem=None):
    return pltpu.CompilerParams(dimension_semantics=("arbitrary",) * n_axes, vmem_limit_bytes=vmem)


def _const_spec(shape):
    nd = len(shape)
    return pl.BlockSpec(shape, lambda *_: (0,) * nd)


def _resident_spec(shape):
    nd = len(shape)
    return pl.BlockSpec(shape, lambda *_: (0,) * nd, pipeline_mode=pl.Buffered(1))


def _sigmoid(x):
    return 1.0 / (1.0 + jnp.exp(-x))


def _gelu_tanh(x):
    return 0.5 * x * (1.0 + jnp.tanh(math.sqrt(2.0 / math.pi) * (x + 0.044715 * (x * x * x))))


def _layernorm(z, g, b):
    mu = jnp.mean(z, axis=-1, keepdims=True)
    zc = z - mu
    var = jnp.mean(zc * zc, axis=-1, keepdims=True)
    return zc * lax.rsqrt(var + LN_EPS) * g + b


def _inproj_body(x_ref, h0_ref, w_ref, u_ref, q_ref, k_ref, v_ref, kb_ref, vb_ref, qi_ref, kiw_ref,
                 ga_ref, gb_ref, *, n_head_tiles):
    if n_head_tiles:
        h = jnp.where(pl.program_id(0) < n_head_tiles, h0_ref[...], x_ref[...])
    else:
        h = x_ref[...]
    hb = h.astype(BF16)

    def seg(a, b):
        return jnp.dot(hb, w_ref[:, a:b], preferred_element_type=F32)

    u_ref[...] = seg(C_U, C_Q)
    q_ref[...] = (seg(C_Q, C_K) * HEAD_DIM ** -0.5).astype(BF16)
    k = seg(C_K, C_V)
    k_ref[...] = k
    kb_ref[...] = k.astype(BF16)
    v = seg(C_V, C_QI)
    v_ref[...] = v
    vb_ref[...] = v.astype(BF16)
    qi_ref[...] = seg(C_QI, C_KIW).astype(BF16)
    kiw_ref[...] = seg(C_KIW, C_GA)
    ga_ref[...] = seg(C_GA, C_GB)
    gb_ref[...] = seg(C_GB, C_END)


def _in_proj(x, h0, wp, *, tm, head_rows):
    n_head_tiles = head_rows // tm
    n_rows = head_rows + x.shape[0]
    grid = (n_rows // tm,)
    if n_head_tiles:
        x_spec = pl.BlockSpec((tm, D_MODEL), lambda i: (jnp.maximum(i - n_head_tiles, 0), 0))
        h_spec = pl.BlockSpec((tm, D_MODEL), lambda i: (jnp.minimum(i, n_head_tiles - 1), 0))
    else:
        h0 = x
        x_spec = pl.BlockSpec((tm, D_MODEL), lambda i: (i, 0))
        h_spec = pl.BlockSpec((tm, D_MODEL), lambda i: (0, 0))

    def row(width):
        return pl.BlockSpec((tm, width), lambda i: (i, 0))

    widths = (512, 512, 512, 512, 512, 512, 256, 128, 1024, 1024)
    dtypes = (F32, BF16, F32, F32, BF16, BF16, BF16, F32, F32, F32)
    return pl.pallas_call(
        functools.partial(_inproj_body, n_head_tiles=n_head_tiles),
        grid=grid,
        in_specs=[x_spec, h_spec, _resident_spec((D_MODEL, C_END))],
        out_specs=[row(w) for w in widths],
        out_shape=[jax.ShapeDtypeStruct((n_rows, w), d) for w, d in zip(widths, dtypes)],
        compiler_params=_cparams(1, VMEM_LIMIT),
        name="in_proj",
    )(x, h0, wp)


SSM_CHUNK = 256


def _ssm_out(h_cat, u, cbig_ref, d_ref, wglu_ref):
    y_lin = jnp.dot(h_cat.astype(BF16), cbig_ref[...], preferred_element_type=F32)
    y = _gelu_tanh(y_lin + d_ref[...] * u)
    z = jnp.dot(y.astype(BF16), wglu_ref[...], preferred_element_type=F32)
    return y * _sigmoid(z)


def _ssm_prompt_body(u_ref, bbig_ref, apw_ref, cbig_ref, d_ref, wglu_ref, y_ref, st_ref,
                     bu_ref, hs_ref, hc_ref):
    @pl.when(pl.program_id(0) == 0)
    def _():
        hc_ref[...] = jnp.zeros_like(hc_ref)

    u = u_ref[...]
    bu_ref[...] = jnp.dot(u.astype(BF16), bbig_ref[...], preferred_element_type=F32)

    def step(r, carry):
        h_re, h_im = carry
        r8 = pl.multiple_of(r * 8, 8)
        x_re = bu_ref[pl.ds(r8, 8), 0:N_STATE]
        x_im = bu_ref[pl.ds(r8, 8), N_STATE:2 * N_STATE]
        for lvl, sh in enumerate((1, 2, 4)):
            a_re = apw_ref[2 * lvl]
            a_im = apw_ref[2 * lvl + 1]
            s_re = pltpu.roll(x_re, sh, 0)
            s_im = pltpu.roll(x_im, sh, 0)
            x_re, x_im = (x_re + (a_re * s_re - a_im * s_im), x_im + (a_re * s_im + a_im * s_re))
        c_re = apw_ref[6]
        c_im = apw_ref[7]
        x_re, x_im = (x_re + (c_re * h_re - c_im * h_im), x_im + (c_re * h_im + c_im * h_re))
        hs_ref[pl.ds(r8, 8), 0:N_STATE] = x_re
        hs_ref[pl.ds(r8, 8), N_STATE:2 * N_STATE] = x_im
        return x_re[7:8, :], x_im[7:8, :]

    h_re, h_im = lax.fori_loop(0, SSM_CHUNK // 8, step, (hc_ref[0:1, :], hc_ref[1:2, :]))
    hc_ref[0:1, :] = h_re
    hc_ref[1:2, :] = h_im
    st_ref[...] = hc_ref[...]
    y_ref[...] = _ssm_out(hs_ref[...], u, cbig_ref, d_ref, wglu_ref).astype(BF16)


def _ssm_prompt(u, bbig, apw, cbig, dskip, wglu):
    n_rows = u.shape[0]
    return pl.pallas_call(
        _ssm_prompt_body,
        grid=(n_rows // SSM_CHUNK,),
        in_specs=[pl.BlockSpec((SSM_CHUNK, SSM_WIDTH), lambda i: (i, 0)),
                  _const_spec((SSM_WIDTH, 2 * N_STATE)), _const_spec((8, 8, N_STATE)),
                  _const_spec((2 * N_STATE, SSM_WIDTH)), _const_spec((1, SSM_WIDTH)),
                  _const_spec((SSM_WIDTH, SSM_WIDTH))],
        out_specs=[pl.BlockSpec((SSM_CHUNK, SSM_WIDTH), lambda i: (i, 0)), _const_spec((2, N_STATE))],
        out_shape=[jax.ShapeDtypeStruct((n_rows, SSM_WIDTH), BF16),
                   jax.ShapeDtypeStruct((2, N_STATE), F32)],
        scratch_shapes=[pltpu.VMEM((SSM_CHUNK, 2 * N_STATE), F32), pltpu.VMEM((SSM_CHUNK, 2 * N_STATE), F32),
                        pltpu.VMEM((2, N_STATE), F32)],
        compiler_params=_cparams(1, VMEM_LIMIT),
        name="ssm_prompt",
    )(u, bbig, apw, cbig, dskip, wglu)


def _ssm_sample_body(u_ref, h0re_ref, h0im_ref, bhi_ref, blo_ref, a_ref, cbig_ref, d_ref, wglu_ref,
                     y_ref, hre_ref, him_ref):
    a_re = a_ref[0:1, :]
    a_im = a_ref[1:2, :]
    h_re = h0re_ref[...]
    h_im = h0im_ref[...]
    for s in range(u_ref.shape[0]):
        u = u_ref[s]
        u_hi = u.astype(BF16)
        u_lo = (u - u_hi.astype(F32)).astype(BF16)
        bu = (jnp.dot(u_hi, bhi_ref[...], preferred_element_type=F32)
              + (jnp.dot(u_lo, bhi_ref[...], preferred_element_type=F32)
                 + jnp.dot(u_hi, blo_ref[...], preferred_element_type=F32)))
        h_re, h_im = (a_re * h_re - a_im * h_im + bu[:, 0:N_STATE],
                      a_re * h_im + a_im * h_re + bu[:, N_STATE:2 * N_STATE])
        h_cat = jnp.concatenate([h_re, h_im], axis=1)
        y_ref[s] = _ssm_out(h_cat, u, cbig_ref, d_ref, wglu_ref).astype(BF16)
    hre_ref[...] = h_re
    him_ref[...] = h_im


def _ssm_sample(u_s, h0_re, h0_im, bhi, blo, a_pair, cbig, dskip, wglu):
    n_s, n_b, _ = u_s.shape
    return pl.pallas_call(
        _ssm_sample_body,
        grid=(1,),
        in_specs=[_const_spec(u_s.shape), _const_spec(h0_re.shape), _const_spec(h0_im.shape),
                  _const_spec(bhi.shape), _const_spec(blo.shape), _const_spec(a_pair.shape),
                  _const_spec(cbig.shape), _const_spec(dskip.shape), _const_spec(wglu.shape)],
        out_specs=[_const_spec((n_s, n_b, SSM_WIDTH)), _const_spec((n_b, N_STATE)), _const_spec((n_b, N_STATE))],
        out_shape=[jax.ShapeDtypeStruct((n_s, n_b, SSM_WIDTH), BF16),
                   jax.ShapeDtypeStruct((n_b, N_STATE), F32), jax.ShapeDtypeStruct((n_b, N_STATE), F32)],
        compiler_params=_cparams(1, VMEM_LIMIT),
        name="ssm_sample",
    )(u_s, h0_re, h0_im, bhi, blo, a_pair, cbig, dskip, wglu)


def _key_to_float(key):
    bits = key ^ ((key >> 31) & 0x7FFFFFFF)
    return lax.bitcast_convert_type(bits, F32)


def _float_to_key(x):
    bits = lax.bitcast_convert_type(x, I32)
    return bits ^ ((bits >> 31) & 0x7FFFFFFF)


BISECT_EVERY = 8
SELECT_MAX_PASSES = 32 * BISECT_EVERY + 8


def _kth_largest(count_fn, n_all, lane_max):
    kk = float(TOPK)
    rows = LANES

    def to_cols(vec):
        return jnp.broadcast_to(vec, (LANES, rows)).T

    def total(th_vec, strict):
        return jnp.sum(count_fn(to_cols(th_vec), strict).T, axis=0, keepdims=True)

    row_max = jnp.max(lane_max.T, axis=0, keepdims=True)
    zero = jnp.zeros((1, rows), F32)
    c_ge0 = total(zero, False)
    c_gt0 = total(zero, True)
    pos = c_gt0 > kk
    at0 = (c_ge0 >= kk) & (c_gt0 <= kk)
    lo0 = jnp.where(pos | at0, 0, KEY_NEG_INF).astype(I32)
    hi0 = jnp.where(pos, _float_to_key(row_max) + 1, jnp.where(at0, 0, -1)).astype(I32)
    c_lo0 = jnp.where(pos, c_gt0, jnp.where(at0, c_ge0, n_all))
    c_hi0 = jnp.where(pos, 0.0, jnp.where(at0, c_gt0, c_ge0))
    one = jnp.ones((1, rows), F32)

    def mid_key(lo, hi):
        return (lo >> 1) + (hi >> 1) + (lo & hi & 1)

    def finished(lo, hi, c_lo):
        return (c_lo == kk) | (mid_key(lo, hi) == lo)

    def cond(st):
        it, lo, hi, c_lo = st[0], st[1], st[2], st[3]
        open_rows = jnp.where(finished(lo, hi, c_lo), 0.0, 1.0)
        return (it < SELECT_MAX_PASSES) & (jnp.max(open_rows) > 0.0)

    def body(st):
        it, lo, hi, c_lo, c_hi, w_lo, w_hi, side = st
        done = finished(lo, hi, c_lo)
        lo_v = _key_to_float(lo)
        hi_v = _key_to_float(hi)
        g_lo = jnp.log(c_lo * (1.0 / (kk - 0.5))) * w_lo
        g_hi = jnp.log(jnp.maximum(c_hi, 0.5) * (1.0 / (kk - 0.5))) * w_hi
        th_i = lo_v + (g_lo / (g_lo - g_hi)) * (hi_v - lo_v)
        bounded = (lo_v > NEG_INF) & (hi_v < float("inf"))
        k_i = jnp.minimum(jnp.maximum(_float_to_key(th_i), lo + 1), hi - 1)
        k_mid = mid_key(lo, hi)
        probe = jnp.where((it & (BISECT_EVERY - 1)) == BISECT_EVERY - 1, k_mid, jnp.where(bounded, k_i, k_mid))
        c = total(_key_to_float(probe), False)
        ge = c >= kk
        mv_lo = (~done) & ge
        mv_hi = (~done) & (~ge)
        w_hi = jnp.where(mv_lo, jnp.where(side > 0.0, w_hi * 0.5, one), jnp.where(mv_hi, one, w_hi))
        w_lo = jnp.where(mv_hi, jnp.where(side < 0.0, w_lo * 0.5, one), jnp.where(mv_lo, one, w_lo))
        side = jnp.where(mv_lo, 1.0, jnp.where(mv_hi, -1.0, side))
        return (it + 1, jnp.where(mv_lo, probe, lo), jnp.where(mv_hi, probe, hi),
                jnp.where(mv_lo, c, c_lo), jnp.where(mv_hi, c, c_hi), w_lo, w_hi, side)

    st = lax.while_loop(cond, lambda s: body(body(s)), (jnp.int32(0), lo0, hi0, c_lo0, c_hi0, one, one, zero))
    lo, c_lo, c_hi = st[1], st[3], st[4]
    v = _key_to_float(lo)
    need = jnp.where(v == NEG_INF, 0.0, jnp.where(c_lo == kk, kk, kk - c_hi))
    return to_cols(v)[:, 0:1], to_cols(need)[:, 0:1]


def _topk_mask(x, v, need, off, tri):
    eq = x == v
    cum = jnp.dot(jnp.where(eq, 1.0, 0.0).astype(tri.dtype), tri, preferred_element_type=F32)
    sel = (x > v) | (eq & ((cum + off) <= need))
    w = x.shape[1]
    return sel, off + cum[:, w - 1:w]


def _pattn_body(qi_ref, kiw_ref, q_ref, kit_ref, kt_ref, v_ref, tri_ref, o_ref,
                sc_ref, m_ref, l_ref, acc_ref, mx_ref, *, first_block):
    i = pl.program_id(0)
    row0 = i * TQ

    @pl.when(i < first_block)
    def _():
        o_ref[...] = jnp.zeros_like(o_ref)

    @pl.when(i >= first_block)
    def _():
        jd = row0 // TK
        w = kiw_ref[:, IDX_DIM:IDX_DIM + IDX_HEADS] * (IDX_HEADS ** -0.5 * IDX_DIM ** -0.5)
        qi_h = [qi_ref[:, IDX_DIM * h:IDX_DIM * (h + 1)] for h in range(IDX_HEADS)]
        q_h = [q_ref[:, HEAD_DIM * h:HEAD_DIM * (h + 1)] for h in range(N_HEADS)]

        def score_tile(c0, width, masked, mx):
            kit = kit_ref[:, pl.ds(c0, width)]
            tot = None
            for h in range(IDX_HEADS):
                z = jnp.dot(qi_h[h], kit, preferred_element_type=F32)
                r = jnp.maximum(z, 0.0) * w[:, h:h + 1]
                tot = r if tot is None else tot + r
            if masked:
                s_idx = c0 + lax.broadcasted_iota(I32, (TQ, width), 1)
                t_idx = row0 + lax.broadcasted_iota(I32, (TQ, width), 0)
                tot = jnp.where((s_idx <= t_idx) & (s_idx >= PADF), tot, NEG_INF)
            sc_ref[:, pl.ds(c0, width)] = tot
            for a in range(width // LANES):
                mx = jnp.maximum(mx, tot[:, LANES * a:LANES * (a + 1)])
            return mx

        def main_start(j):
            return pl.multiple_of(j * TK, TK)

        mx = score_tile(HEAD_KEY0, LANES, True, jnp.full((TQ, LANES), NEG_INF, F32))
        mx = lax.fori_loop(1, jd, lambda j, mx: score_tile(main_start(j), TK, False, mx), mx)
        mx_ref[...] = mx

        @pl.when(jd >= 1)
        def _():
            mx_ref[...] = score_tile(main_start(jd), TK, True, mx)


        def count_fn(th, strict):
            def add(cnt, x):
                return cnt + jnp.where((x > th) if strict else (x >= th), 1.0, 0.0)

            def count_main(j, cnt):
                c0 = main_start(j)
                for a in range(TK // LANES):
                    cnt = add(cnt, sc_ref[:, pl.ds(c0 + a * LANES, LANES)])
                return cnt

            cnt = add(jnp.zeros((TQ, LANES), F32), sc_ref[:, HEAD_KEY0:HEAD_ROWS])
            return lax.fori_loop(1, jd + 1, count_main, cnt)

        v, need = _kth_largest(count_fn, (LANES + TK * jd).astype(F32), mx_ref[...])

        m_ref[...] = jnp.full_like(m_ref, M_INIT)
        l_ref[...] = jnp.zeros_like(l_ref)
        acc_ref[...] = jnp.zeros_like(acc_ref)
        first_half = lax.broadcasted_iota(I32, (TQ, LANES), 1) < HEAD_DIM

        def attend_tile(c0, width, off):
            x = sc_ref[:, pl.ds(c0, width)]
            parts = []
            for a in range(0, width, TIE_W):
                wa = min(TIE_W, width - a)
                sel, off = _topk_mask(x[:, a:a + wa], v, need, off, tri_ref[0:wa, 0:wa])
                parts.append(jnp.where(sel, 0.0, NEG_INF))
            bias = parts[0] if len(parts) == 1 else jnp.concatenate(parts, axis=1)
            n_sl = width // LANES

            def probs(h):
                kt = kt_ref[HEAD_DIM * h:HEAD_DIM * (h + 1), pl.ds(c0, width)]
                s = jnp.dot(q_h[h], kt, preferred_element_type=F32) + bias
                sl = [s[:, LANES * a:LANES * (a + 1)] for a in range(n_sl)]
                pm = sl[0]
                for a in range(1, n_sl):
                    pm = jnp.maximum(pm, sl[a])
                m_old = m_ref[h]
                m_new = jnp.maximum(m_old, jnp.max(pm, axis=1, keepdims=True))
                alpha = jnp.exp(m_old - m_new)
                ps = [jnp.exp(sl[a] - m_new) for a in range(n_sl)]
                psum = ps[0]
                for a in range(1, n_sl):
                    psum = psum + ps[a]
                l_ref[h] = alpha * l_ref[h] + psum
                m_ref[h] = m_new
                return (ps[0] if n_sl == 1 else jnp.concatenate(ps, axis=1)).astype(BF16), alpha

            for j in range(N_HEADS // 2):
                p_e, alpha_e = probs(2 * j)
                p_o, alpha_o = probs(2 * j + 1)
                cols = slice(LANES * j, LANES * (j + 1))
                pv = jnp.dot(jnp.concatenate([p_e, p_o], axis=0), v_ref[pl.ds(c0, width), cols],
                             preferred_element_type=F32)
                a_pair = jnp.where(first_half, alpha_e, alpha_o)
                pv_pair = jnp.where(first_half, pv[0:TQ], pv[TQ:2 * TQ])
                acc_ref[:, cols] = a_pair * acc_ref[:, cols] + pv_pair
            return off

        off = attend_tile(HEAD_KEY0, LANES, jnp.zeros((TQ, 1), F32))
        lax.fori_loop(1, jd + 1, lambda j, off: attend_tile(main_start(j), TK, off), off)

        for j in range(N_HEADS // 2):
            l_e = jnp.sum(l_ref[2 * j], axis=1, keepdims=True)
            l_o = jnp.sum(l_ref[2 * j + 1], axis=1, keepdims=True)
            l_pair = jnp.where(first_half, l_e, l_o)
            inv = 1.0 / jnp.where(l_pair > 0.0, l_pair, 1.0)
            o_ref[:, LANES * j:LANES * (j + 1)] = (acc_ref[:, LANES * j:LANES * (j + 1)] * inv).astype(o_ref.dtype)


def _prompt_attention(qi, kiw, q, kit, kt, vb, tri):
    n_rows = q.shape[0]

    def row(width):
        return pl.BlockSpec((TQ, width), lambda i: (i, 0))

    return pl.pallas_call(
        functools.partial(_pattn_body, first_block=HEAD_KEY0 // TQ),
        grid=(n_rows // TQ,),
        in_specs=[row(IDX_HEADS * IDX_DIM), row(128), row(ATT_WIDTH),
                  _resident_spec(kit.shape), _resident_spec(kt.shape), _resident_spec(vb.shape),
                  _resident_spec(tri.shape)],
        out_specs=row(ATT_WIDTH),
        out_shape=jax.ShapeDtypeStruct((n_rows, ATT_WIDTH), BF16),
        scratch_shapes=[pltpu.VMEM((TQ, n_rows), F32), pltpu.VMEM((N_HEADS, TQ, LANES), F32),
                        pltpu.VMEM((N_HEADS, TQ, LANES), F32), pltpu.VMEM((TQ, ATT_WIDTH), F32),
                        pltpu.VMEM((TQ, LANES), F32)],
        compiler_params=_cparams(1, VMEM_LIMIT),
        name="prompt_attention",
    )(qi, kiw, q, kit, kt, vb, tri)


PAGES_PER_STEP = 16
S_CHUNK = PAGES_PER_STEP * PAGE
S_ROWS = 8


def _idx_scores_rows(qi, w, keys_bf16):
    z = lax.dot_general(qi, keys_bf16, (((1,), (1,)), ((), ())), preferred_element_type=F32)
    r = jnp.maximum(z, 0.0) * w
    tot = r[0:S_ROWS]
    for h in range(1, IDX_HEADS):
        tot = tot + r[S_ROWS * h:S_ROWS * (h + 1)]
    return tot


def _sscore_body(pt_ref, qi_ref, w_ref, kinew_ref, *refs, n_chunks, past):
    page_refs = refs[:PAGES_PER_STEP]
    sc_ref, kic_ref = refs[PAGES_PER_STEP:]
    c = pl.program_id(1)
    qi = qi_ref[...]
    w = w_ref[...]
    for r in range(PAGES_PER_STEP):
        kic_ref[:, r * PAGE:(r + 1) * PAGE] = page_refs[r][...].astype(BF16)
    rr = jnp.maximum(jnp.dot(qi, kic_ref[...], preferred_element_type=F32), 0.0) * w
    tot = rr[0:S_ROWS]
    for h in range(1, IDX_HEADS):
        tot = tot + rr[S_ROWS * h:S_ROWS * (h + 1)]
    sc_ref[:, pl.ds(pl.multiple_of(c * S_CHUNK, S_CHUNK), S_CHUNK)] = tot

    @pl.when(c == n_chunks - 1)
    def _():
        tot = _idx_scores_rows(qi, w, kinew_ref[...])
        lane = lax.broadcasted_iota(I32, (S_ROWS, PAGE), 1)
        row = lax.broadcasted_iota(I32, (S_ROWS, PAGE), 0)
        sc_ref[:, past:past + PAGE] = jnp.where(lane <= row, tot, NEG_INF)


def _sselect_body(sc_ref, par_ref, *, sel_tile):
    n_tiles = sc_ref.shape[1] // sel_tile

    def tiles(fn, init):
        def body(j, acc):
            c0 = pl.multiple_of(j * sel_tile, LANES)
            for a in range(sel_tile // LANES):
                acc = fn(acc, sc_ref[:, pl.ds(c0 + a * LANES, LANES)])
            return acc

        return lax.fori_loop(0, n_tiles, body, init)

    def count_fn(th, strict):
        return tiles(lambda cnt, x: cnt + jnp.where((x > th) if strict else (x >= th), 1.0, 0.0),
                     jnp.zeros((LANES, LANES), F32))

    lane_max = tiles(jnp.maximum, jnp.full((LANES, LANES), NEG_INF, F32))
    v, need = _kth_largest(count_fn, float(sc_ref.shape[1]), lane_max)
    lane = lax.broadcasted_iota(I32, (LANES, LANES), 1)
    par_ref[...] = jnp.where(lane == 0, v, jnp.where(lane == 1, need, 0.0))


def _sample_select(scores2d):
    n_rows, width = scores2d.shape
    sel_tile = max(t for t in range(LANES, 6 * LANES, LANES) if width % t == 0)
    pad = -n_rows % LANES
    if pad:
        scores2d = jnp.pad(scores2d, [(0, pad), (0, 0)])
    par = pl.pallas_call(
        functools.partial(_sselect_body, sel_tile=sel_tile),
        grid=((n_rows + pad) // LANES,),
        in_specs=[pl.BlockSpec((LANES, width), lambda i: (i, 0))],
        out_specs=pl.BlockSpec((LANES, LANES), lambda i: (i, 0)),
        out_shape=jax.ShapeDtypeStruct((n_rows + pad, LANES), F32),
        compiler_params=_cparams(1, VMEM_LIMIT),
        name="sample_select",
    )(scores2d)
    return par[:n_rows]


def _sample_scores(page_table, qi_s, w_s, kinew, cache_kidx):
    n_b, n_pages = page_table.shape
    n_chunks = n_pages // PAGES_PER_STEP
    past = n_pages * PAGE
    width = past + PAGE

    def page_spec(r):
        return pl.BlockSpec((None, IDX_DIM, PAGE), lambda b, c, pt: (pt[b, c * PAGES_PER_STEP + r], 0, 0))

    def per_b(shape):
        return pl.BlockSpec((None,) + shape, lambda b, c, pt: (b, 0, 0))

    grid_spec = pltpu.PrefetchScalarGridSpec(
        num_scalar_prefetch=1,
        grid=(n_b, n_chunks),
        in_specs=[per_b((IDX_HEADS * S_ROWS, IDX_DIM)), per_b((IDX_HEADS * S_ROWS, 1)), per_b((PAGE, IDX_DIM))]
                 + [page_spec(r) for r in range(PAGES_PER_STEP)],
        out_specs=per_b((S_ROWS, width)),
        scratch_shapes=[pltpu.VMEM((IDX_DIM, S_CHUNK), BF16)],
    )
    scores = pl.pallas_call(
        functools.partial(_sscore_body, n_chunks=n_chunks, past=past),
        grid_spec=grid_spec,
        out_shape=jax.ShapeDtypeStruct((n_b, S_ROWS, width), F32),
        compiler_params=_cparams(2, VMEM_LIMIT),
        name="sample_scores",
    )(page_table, qi_s, w_s, kinew, *([cache_kidx] * PAGES_PER_STEP))
    par = _sample_select(scores.reshape(n_b * S_ROWS, width)).reshape(n_b, S_ROWS, LANES)
    return scores, par


S_TIE = 256


def _sattn_body(pt_ref, q_ref, sc_ref, scn_ref, par_ref, knew_ref, vnew_ref, tri_ref, *refs, n_chunks):
    k_refs = refs[:PAGES_PER_STEP]
    v_refs = refs[PAGES_PER_STEP:2 * PAGES_PER_STEP]
    o_ref, qbd_ref, kc_ref, vc_ref, m_ref, l_ref, acc_ref, off_ref = refs[2 * PAGES_PER_STEP:]
    c = pl.program_id(1)
    n_q = 4
    n_rows = n_q * N_HEADS
    head_of_col = lax.broadcasted_iota(I32, (N_HEADS, ATT_WIDTH), 1) // HEAD_DIM
    own_head = head_of_col == lax.broadcasted_iota(I32, (N_HEADS, ATT_WIDTH), 0)

    @pl.when(c == 0)
    def _():
        for s in range(n_q):
            qs = jnp.broadcast_to(q_ref[s:s + 1, :], (N_HEADS, ATT_WIDTH))
            qbd_ref[N_HEADS * s:N_HEADS * (s + 1), :] = jnp.where(own_head, qs, 0.0)
        m_ref[...] = jnp.full_like(m_ref, M_INIT)
        l_ref[...] = jnp.zeros_like(l_ref)
        acc_ref[...] = jnp.zeros_like(acc_ref)
        off_ref[...] = jnp.zeros_like(off_ref)

    v_thr = par_ref[:, 0:1]
    need = par_ref[:, 1:2]
    qbd = qbd_ref[...].astype(BF16)

    nt_dims = (((1,), (1,)), ((), ()))

    def attend(s_raw, pv_fn, x):
        width = x.shape[1]
        off = off_ref[...]
        parts = []
        for a in range(0, width, S_TIE):
            wa = min(S_TIE, width - a)
            sel, off = _topk_mask(x[:, a:a + wa], v_thr, need, off, tri_ref[0:wa, 0:wa])
            parts.append(jnp.where(sel, 0.0, NEG_INF))
        off_ref[...] = off
        bias8 = parts[0] if len(parts) == 1 else jnp.concatenate(parts, axis=1)
        bias = jnp.concatenate([jnp.broadcast_to(bias8[s:s + 1, :], (N_HEADS, width)) for s in range(n_q)], axis=0)
        s_mat = s_raw + bias
        m_old = m_ref[...]
        m_new = jnp.maximum(m_old, jnp.max(s_mat, axis=1, keepdims=True))
        p = jnp.exp(s_mat - m_new)
        alpha = jnp.exp(m_old - m_new)
        l_ref[...] = alpha * l_ref[...] + jnp.sum(p, axis=1, keepdims=True)
        acc_ref[...] = alpha * acc_ref[...] + pv_fn(p.astype(BF16))
        m_ref[...] = m_new

    for r in range(PAGES_PER_STEP):
        kc_ref[:, r * PAGE:(r + 1) * PAGE] = k_refs[r][...].astype(BF16)
        vc_ref[:, r * PAGE:(r + 1) * PAGE] = v_refs[r][...].astype(BF16)
    attend(jnp.dot(qbd, kc_ref[...], preferred_element_type=F32),
           lambda p: lax.dot_general(p, vc_ref[...], nt_dims, preferred_element_type=F32), sc_ref[...])

    @pl.when(c == n_chunks - 1)
    def _():
        pad = jnp.zeros((PAGE - S_ROWS, ATT_WIDTH), F32)
        k_new = jnp.concatenate([knew_ref[...], pad], axis=0).astype(BF16)
        v_new = jnp.concatenate([vnew_ref[...], pad], axis=0).astype(BF16)
        attend(lax.dot_general(qbd, k_new, nt_dims, preferred_element_type=F32),
               lambda p: jnp.dot(p, v_new, preferred_element_type=F32), scn_ref[...])
        l = l_ref[...]
        o_all = acc_ref[...] * (1.0 / jnp.where(l > 0.0, l, 1.0))
        row = lax.broadcasted_iota(I32, (S_ROWS, ATT_WIDTH), 0)
        out = jnp.zeros((S_ROWS, ATT_WIDTH), F32)
        for s in range(n_q):
            o_s = jnp.sum(jnp.where(own_head, o_all[N_HEADS * s:N_HEADS * (s + 1), :], 0.0), axis=0, keepdims=True)
            out = jnp.where(row == s, jnp.broadcast_to(o_s, (S_ROWS, ATT_WIDTH)), out)
        o_ref[...] = out


def _sample_attention(page_table, q_s, scores, par, knew, vnew, tri, cache_k, cache_v):
    n_b, n_pages = page_table.shape
    n_chunks = n_pages // PAGES_PER_STEP
    n_rows = 4 * N_HEADS

    def page_spec(r):
        return pl.BlockSpec((None, ATT_WIDTH, PAGE), lambda b, c, pt: (pt[b, c * PAGES_PER_STEP + r], 0, 0))

    def per_b(shape):
        return pl.BlockSpec((None,) + shape, lambda b, c, pt: (b, 0, 0))

    grid_spec = pltpu.PrefetchScalarGridSpec(
        num_scalar_prefetch=1,
        grid=(n_b, n_chunks),
        in_specs=[per_b((S_ROWS, ATT_WIDTH)),
                  pl.BlockSpec((None, S_ROWS, S_CHUNK), lambda b, c, pt: (b, 0, c)),
                  pl.BlockSpec((None, S_ROWS, PAGE), lambda b, c, pt: (b, 0, n_pages)),
                  per_b((S_ROWS, PAGE)), per_b((S_ROWS, ATT_WIDTH)), per_b((S_ROWS, ATT_WIDTH)),
                  pl.BlockSpec((S_TIE, S_TIE), lambda b, c, pt: (0, 0))]
                 + [page_spec(r) for r in range(PAGES_PER_STEP)] * 2,
        out_specs=per_b((S_ROWS, ATT_WIDTH)),
        scratch_shapes=[pltpu.VMEM((n_rows, ATT_WIDTH), F32),
                        pltpu.VMEM((ATT_WIDTH, S_CHUNK), BF16), pltpu.VMEM((ATT_WIDTH, S_CHUNK), BF16),
                        pltpu.VMEM((n_rows, 1), F32), pltpu.VMEM((n_rows, 1), F32),
                        pltpu.VMEM((n_rows, ATT_WIDTH), F32), pltpu.VMEM((S_ROWS, 1), F32)],
    )
    return pl.pallas_call(
        functools.partial(_sattn_body, n_chunks=n_chunks),
        grid_spec=grid_spec,
        out_shape=jax.ShapeDtypeStruct((n_b, S_ROWS, ATT_WIDTH), F32),
        compiler_params=_cparams(2, VMEM_LIMIT),
        name="sample_attention",
    )(page_table, q_s, scores, scores, par, knew, vnew, tri,
      *([cache_k] * PAGES_PER_STEP), *([cache_v] * PAGES_PER_STEP))


R_LANES = 128
R_E0 = N_EGROUPS


def _route(logits):
    lane = lax.broadcasted_iota(I32, logits.shape, 1)
    lane_f = lane.astype(F32)
    big = float(R_LANES)
    is_g = lane < N_EGROUPS
    gl = jnp.where(is_g, logits, NEG_INF)
    g_max = jnp.max(gl, axis=1, keepdims=True)
    g_sel = jnp.min(jnp.where(gl == g_max, lane_f, big), axis=1, keepdims=True)
    p_g = 1.0 / jnp.sum(jnp.exp(gl - g_max), axis=1, keepdims=True)
    member = (lane >= R_E0) & (lane < R_E0 + N_EXPERTS) & (((lane - R_E0) >> 2).astype(F32) == g_sel)
    el = jnp.where(member, logits, NEG_INF)
    e_max = jnp.max(el, axis=1, keepdims=True)
    pe = jnp.exp(el - e_max)
    pe = pe / jnp.sum(pe, axis=1, keepdims=True)
    p1 = jnp.max(pe, axis=1, keepdims=True)
    i1 = jnp.min(jnp.where(member & (pe == p1), lane_f, big), axis=1, keepdims=True)
    rest = member & (lane_f != i1)
    p2 = jnp.max(jnp.where(rest, pe, -1.0), axis=1, keepdims=True)
    i2 = jnp.min(jnp.where(rest & (pe == p2), lane_f, big), axis=1, keepdims=True)
    tot = p1 + p2
    return jnp.where(lane_f == i1, p1 / tot * p_g, 0.0) + jnp.where(lane_f == i2, p2 / tot * p_g, 0.0)


def _mix_body(x_ref, h0_ref, ys_ref, oa_ref, ga_ref, gb_ref, wso_ref, wao_ref, wo_ref, g1_ref, b1_ref,
              wrh_ref, wrl_ref, br_ref, h1_ref, h1b_ref, comb_ref, *, n_head_tiles):
    if n_head_tiles:
        h = jnp.where(pl.program_id(0) < n_head_tiles, h0_ref[...], x_ref[...])
    else:
        h = x_ref[...]
    a = jnp.dot(ys_ref[...], wso_ref[...], preferred_element_type=F32)
    b = jnp.dot(oa_ref[...], wao_ref[...], preferred_element_type=F32)
    m = _sigmoid(ga_ref[...]) * a + _sigmoid(gb_ref[...]) * b
    mix = jnp.dot(m.astype(BF16), wo_ref[...], preferred_element_type=F32)
    h1 = _layernorm(DN_ALPHA * h + mix, g1_ref[...], b1_ref[...])
    h1_ref[...] = h1
    h_hi = h1.astype(BF16)
    h1b_ref[...] = h_hi
    h_lo = (h1 - h_hi.astype(F32)).astype(BF16)
    logits = (jnp.dot(h_hi, wrh_ref[...], preferred_element_type=F32)
              + (jnp.dot(h_lo, wrh_ref[...], preferred_element_type=F32)
                 + jnp.dot(h_hi, wrl_ref[...], preferred_element_type=F32))) + br_ref[...]
    comb_ref[...] = _route(logits)


def _mix_ln_route(x, h0, ys, oa, ga, gb, wso, wao, wo, g1, b1, wrh, wrl, br, *, tm, head_rows):
    n_head_tiles = head_rows // tm
    n_rows = head_rows + x.shape[0]
    if n_head_tiles:
        x_spec = pl.BlockSpec((tm, D_MODEL), lambda i: (jnp.maximum(i - n_head_tiles, 0), 0))
        h_spec = pl.BlockSpec((tm, D_MODEL), lambda i: (jnp.minimum(i, n_head_tiles - 1), 0))
    else:
        h0 = x
        x_spec = pl.BlockSpec((tm, D_MODEL), lambda i: (i, 0))
        h_spec = pl.BlockSpec((tm, D_MODEL), lambda i: (0, 0))

    def row(width):
        return pl.BlockSpec((tm, width), lambda i: (i, 0))

    consts = (wso, wao, wo, g1, b1, wrh, wrl, br)
    return pl.pallas_call(
        functools.partial(_mix_body, n_head_tiles=n_head_tiles),
        grid=(n_rows // tm,),
        in_specs=[x_spec, h_spec, row(SSM_WIDTH), row(ATT_WIDTH), row(D_MODEL), row(D_MODEL)]
                 + [_const_spec(c.shape) for c in consts],
        out_specs=[row(D_MODEL), row(D_MODEL), row(R_LANES)],
        out_shape=[jax.ShapeDtypeStruct((n_rows, D_MODEL), F32), jax.ShapeDtypeStruct((n_rows, D_MODEL), BF16),
                   jax.ShapeDtypeStruct((n_rows, R_LANES), F32)],
        compiler_params=_cparams(1, VMEM_LIMIT),
        name="mix_ln_route",
    )(x, h0, ys, oa, ga, gb, *consts)


def _moe_body(h1_ref, h1b_ref, comb_ref, wgu_ref, wd_ref, g2_ref, b2_ref, y_ref, acc_ref):
    e = pl.program_id(1)

    @pl.when(e == 0)
    def _():
        acc_ref[...] = jnp.zeros_like(acc_ref)

    gu = jnp.dot(h1b_ref[...], wgu_ref[...], preferred_element_type=F32)
    gate = gu[:, 0:D_EXPERT]
    up = gu[:, D_EXPERT:2 * D_EXPERT]
    act = gate * _sigmoid(gate) * up
    comb = comb_ref[...]
    lane = lax.broadcasted_iota(I32, comb.shape, 1)
    c_e = jnp.sum(jnp.where(lane == e + R_E0, comb, 0.0), axis=1, keepdims=True)
    acc_ref[...] += jnp.dot((act * c_e).astype(BF16), wd_ref[...], preferred_element_type=F32)

    @pl.when(e == N_EXPERTS - 1)
    def _():
        y_ref[...] = _layernorm(DN_ALPHA * h1_ref[...] + acc_ref[...], g2_ref[...], b2_ref[...])


def _moe_ln(h1, h1b, comb, wgu, wd, g2, b2, *, tm, head_rows):
    n_head_tiles = head_rows // tm
    n_rows = h1.shape[0]
    out_rows = n_rows - head_rows

    def row(width):
        return pl.BlockSpec((tm, width), lambda i, e: (i, 0))

    return pl.pallas_call(
        _moe_body,
        grid=(n_rows // tm, N_EXPERTS),
        in_specs=[row(D_MODEL), row(D_MODEL), row(R_LANES),
                  pl.BlockSpec((None, D_MODEL, 2 * D_EXPERT), lambda i, e: (e, 0, 0)),
                  pl.BlockSpec((None, D_EXPERT, D_MODEL), lambda i, e: (e, 0, 0)),
                  pl.BlockSpec((1, D_MODEL), lambda i, e: (0, 0)), pl.BlockSpec((1, D_MODEL), lambda i, e: (0, 0))],
        out_specs=pl.BlockSpec((tm, D_MODEL), lambda i, e: (jnp.maximum(i - n_head_tiles, 0), 0)),
        out_shape=jax.ShapeDtypeStruct((out_rows, D_MODEL), F32),
        scratch_shapes=[pltpu.VMEM((tm, D_MODEL), F32)],
        compiler_params=_cparams(2, VMEM_LIMIT),
        name="moe_ln",
    )(h1, h1b, comb, wgu, wd, g2, b2)


def _ssm_tables(a_re, a_im, log_dt, b_re, b_im, c_re, c_im):
    dt = jnp.exp(log_dt)[:, None]
    mag = jnp.exp(a_re * dt)
    ab_re = mag * jnp.cos(a_im * dt)
    ab_im = mag * jnp.sin(a_im * dt)
    den = a_re * a_re + a_im * a_im
    nr = ab_re - 1.0
    f_re = (nr * a_re + ab_im * a_im) / den
    f_im = (ab_im * a_re - nr * a_im) / den
    bb_re = f_re[..., None] * b_re - f_im[..., None] * b_im
    bb_im = f_re[..., None] * b_im + f_im[..., None] * b_re
    eye = jnp.eye(SSM_GROUPS, dtype=F32)

    def in_mat(bb):
        return jnp.einsum("gpm,gh->gmhp", bb, eye).reshape(SSM_WIDTH, N_STATE)

    def out_mat(cc):
        return jnp.einsum("gmp,gh->gphm", cc, eye).reshape(N_STATE, SSM_WIDTH)

    bbig = jnp.concatenate([in_mat(bb_re), in_mat(bb_im)], axis=1)
    cbig = jnp.concatenate([out_mat(c_re), -out_mat(c_im)], axis=0)
    ar = ab_re.reshape(1, N_STATE)
    ai = ab_im.reshape(1, N_STATE)

    def cmul(x, y):
        return x[0] * y[0] - x[1] * y[1], x[0] * y[1] + x[1] * y[0]

    a1 = (ar, ai)
    pows = [a1]
    for _ in range(7):
        pows.append(cmul(pows[-1], a1))
    row = jnp.arange(8)[:, None]
    tabs = []
    for sh, pw in ((1, pows[0]), (2, pows[1]), (4, pows[3])):
        keep = (row >= sh).astype(F32)
        tabs += [keep * pw[0], keep * pw[1]]
    tabs += [jnp.concatenate([p[0] for p in pows], axis=0), jnp.concatenate([p[1] for p in pows], axis=0)]
    apw = jnp.stack(tabs)
    a_pair = jnp.concatenate([ar, ai], axis=0)
    return bbig, cbig, apw, a_pair


def _split_bf16(x):
    hi = x.astype(BF16)
    return hi, (x - hi.astype(F32)).astype(BF16)


def kernel(x_prompt, x_sample, cache_k, cache_v, cache_kidx, state_ssm_re, state_ssm_im, page_table,
           meta_tokens, w_in, ssm_a_re, ssm_a_im, ssm_log_dt, ssm_b_re, ssm_b_im, ssm_c_re, ssm_c_im,
           ssm_d, w_glu, w_ssm_out, w_att_out, w_o, ln1_g, ln1_b, w_route_group, b_route_group,
           w_route_expert, b_route_expert, w_exp_gate, w_exp_up, w_exp_down, ln2_g, ln2_b):
    depth = w_in.shape[0]
    assert depth == 1 and x_prompt.shape[0] == 1
    n_b, n_s, _ = x_sample.shape
    seq = x_prompt.shape[1]
    t_len = seq + N_META
    lyr = 0

    cuts = [0]
    for c in IN_SPLITS:
        cuts.append(cuts[-1] + c)
    w = w_in[lyr]
    wp = jnp.concatenate([w[:, :cuts[7]], jnp.zeros((D_MODEL, C_GA - C_KIW - IDX_DIM - IDX_HEADS), F32),
                          w[:, cuts[7]:]], axis=1).astype(BF16)
    bbig, cbig, apw, a_pair = _ssm_tables(ssm_a_re[lyr], ssm_a_im[lyr], ssm_log_dt[lyr], ssm_b_re[lyr],
                                          ssm_b_im[lyr], ssm_c_re[lyr], ssm_c_im[lyr])
    bbig_hi, bbig_lo = _split_bf16(bbig)
    cbig_b = cbig.astype(BF16)
    dskip = ssm_d[lyr].reshape(1, SSM_WIDTH)
    wglu_b = w_glu[lyr].astype(BF16)
    wso = w_ssm_out[lyr].astype(BF16)
    wao = w_att_out[lyr].astype(BF16)
    wo = w_o[lyr].astype(BF16)
    g1 = ln1_g[lyr].reshape(1, D_MODEL)
    b1 = ln1_b[lyr].reshape(1, D_MODEL)
    g2 = ln2_g[lyr].reshape(1, D_MODEL)
    b2 = ln2_b[lyr].reshape(1, D_MODEL)
    r_pad = R_LANES - N_EGROUPS - N_EXPERTS
    wr = jnp.concatenate([w_route_group[lyr], w_route_expert[lyr], jnp.zeros((D_MODEL, r_pad), F32)], axis=1)
    wrh, wrl = _split_bf16(wr)
    br = jnp.concatenate([b_route_group[lyr], b_route_expert[lyr], jnp.zeros((r_pad,), F32)]).reshape(1, R_LANES)
    wgu = jnp.concatenate([w_exp_gate[lyr], w_exp_up[lyr]], axis=2).astype(BF16)
    wd = w_exp_down[lyr].astype(BF16)
    tri = (jnp.arange(TIE_W)[:, None] <= jnp.arange(TIE_W)[None, :])

    xp = x_prompt[0]
    head = jnp.concatenate([jnp.zeros((PADF, D_MODEL), F32), meta_tokens.astype(F32)], axis=0)
    u, q, k, v, kb, vb, qi, kiw, ga, gb = _in_proj(xp, head, wp, tm=256, head_rows=HEAD_ROWS)
    y_ssm, st = _ssm_prompt(u, bbig_hi, apw, cbig_b, dskip, wglu_b)
    kit = kiw[:, :IDX_DIM].astype(BF16).T
    o_att = _prompt_attention(qi, kiw, q, kit, kb.T, vb, tri.astype(BF16))
    h1, h1b, comb = _mix_ln_route(xp, head, y_ssm, o_att, ga, gb, wso, wao, wo, g1, b1, wrh, wrl, br,
                                  tm=ROW_TILE, head_rows=HEAD_ROWS)
    y_prompt = _moe_ln(h1, h1b, comb, wgu, wd, g2, b2, tm=HEAD_ROWS, head_rows=HEAD_ROWS)

    n_tok = n_b * n_s
    xs = x_sample.reshape(n_tok, D_MODEL)
    us, qs, ks, vs, _, _, qis, kiws, gas, gbs = _in_proj(xs, None, wp, tm=n_tok, head_rows=0)
    u_s = us.reshape(n_b, n_s, SSM_WIDTH).transpose(1, 0, 2)
    ys_s, hre_s, him_s = _ssm_sample(u_s, state_ssm_re[lyr].reshape(n_b, N_STATE),
                                     state_ssm_im[lyr].reshape(n_b, N_STATE),
                                     bbig_hi, bbig_lo, a_pair, cbig_b, dskip, wglu_b)
    ys_s = ys_s.transpose(1, 0, 2).reshape(n_tok, SSM_WIDTH)

    def pad_rows(a):
        return jnp.pad(a, [(0, 0), (0, S_ROWS - n_s)] + [(0, 0)] * (a.ndim - 2))

    qi4 = pad_rows(qis.reshape(n_b, n_s, IDX_HEADS, IDX_DIM)).transpose(0, 2, 1, 3)
    qi_s = qi4.reshape(n_b, IDX_HEADS * S_ROWS, IDX_DIM)
    w4 = kiws[:, IDX_DIM:IDX_DIM + IDX_HEADS] * (IDX_HEADS ** -0.5 * IDX_DIM ** -0.5)
    w_s = pad_rows(w4.reshape(n_b, n_s, IDX_HEADS)).transpose(0, 2, 1).reshape(n_b, IDX_HEADS * S_ROWS, 1)
    ki_new = kiws[:, :IDX_DIM].reshape(n_b, n_s, IDX_DIM)
    kinew = jnp.pad(ki_new, [(0, 0), (0, PAGE - n_s), (0, 0)]).astype(BF16)
    n_phys = cache_k.shape[1]
    kidx_t = cache_kidx[lyr].transpose(0, 2, 1)
    ck_t = cache_k[lyr].transpose(0, 2, 3, 1).reshape(n_phys, ATT_WIDTH, PAGE)
    cv_t = cache_v[lyr].transpose(0, 2, 3, 1).reshape(n_phys, ATT_WIDTH, PAGE)
    scores, par = _sample_scores(page_table, qi_s, w_s, kinew, kidx_t)
    o_s = _sample_attention(page_table, pad_rows(qs.astype(F32).reshape(n_b, n_s, ATT_WIDTH)), scores, par,
                            pad_rows(ks.reshape(n_b, n_s, ATT_WIDTH)), pad_rows(vs.reshape(n_b, n_s, ATT_WIDTH)),
                            tri[:S_TIE, :S_TIE].astype(F32), ck_t, cv_t)
    o_s = o_s[:, :n_s].reshape(n_tok, ATT_WIDTH).astype(BF16)
    h1s, h1bs, combs = _mix_ln_route(xs, None, ys_s, o_s, gas, gbs, wso, wao, wo, g1, b1, wrh, wrl, br,
                                     tm=n_tok, head_rows=0)
    y_sample = _moe_ln(h1s, h1bs, combs, wgu, wd, g2, b2, tm=n_tok, head_rows=0)

    def heads(a, lead):
        return a.reshape((depth,) + lead + (N_HEADS, HEAD_DIM))

    return (y_prompt.reshape(1, seq, D_MODEL),
            y_sample.reshape(n_b, n_s, D_MODEL),
            heads(k[PADF:], (1, t_len)), heads(v[PADF:], (1, t_len)),
            kiw[PADF:, :IDX_DIM].reshape(depth, 1, t_len, IDX_DIM),
            st[0].reshape(depth, 1, SSM_GROUPS, SSM_STATE), st[1].reshape(depth, 1, SSM_GROUPS, SSM_STATE),
            heads(ks, (n_b, n_s)), heads(vs, (n_b, n_s)),
            kiws[:, :IDX_DIM].reshape(depth, n_b, n_s, IDX_DIM),
            hre_s.reshape(depth, n_b, SSM_GROUPS, SSM_STATE), him_s.reshape(depth, n_b, SSM_GROUPS, SSM_STATE))
```

```python
import functools
import math

import jax
import jax.numpy as jnp
from jax import lax
from jax.experimental import pallas as pl
from jax.experimental.pallas import tpu as pltpu

F32 = jnp.float32
BF16 = jnp.bfloat16
I32 = jnp.int32

D_MODEL = 1024
N_META = 16
SSM_WIDTH = 512
SSM_GROUP_CH = 16
SSM_GROUPS = 32
SSM_STATE = 64
N_STATE = SSM_GROUPS * SSM_STATE
N_HEADS = 8
HEAD_DIM = 64
ATT_WIDTH = N_HEADS * HEAD_DIM
IDX_HEADS = 4
IDX_DIM = 64
TOPK = 256
PAGE = 128
N_EGROUPS = 4
EXPERTS_PER_GROUP = 4
N_EXPERTS = 16
D_EXPERT = 256
DN_ALPHA = 2.0 ** 0.25
LN_EPS = 1e-5
NEG_INF = float("-inf")

C_U, C_Q, C_K, C_V, C_QI, C_KIW, C_GA, C_GB, C_END = 0, 512, 1024, 1536, 2048, 2304, 2432, 3456, 4480
IN_SPLITS = (512, 512, 512, 512, 256, 64, 4, 1024, 1024)

ROW_TILE = 512
HEAD_ROWS = 1024
PADF = HEAD_ROWS - N_META
TQ = 128
TK = 1024
LANES = 128
HEAD_KEY0 = HEAD_ROWS - LANES
TIE_W = 256
M_INIT = -1e30
VMEM_LIMIT = 56 * 1024 * 1024

KEY_NEG_INF = -2139095041
KEY_POS_INF = 2139095040


def _cparams(n_axes, vmem=None):
    return pltpu.CompilerParams(dimension_semantics=("arbitrary",) * n_axes, vmem_limit_bytes=vmem)


def _const_spec(shape):
    nd = len(shape)
    return pl.BlockSpec(shape, lambda *_: (0,) * nd)


def _resident_spec(shape):
    nd = len(shape)
    return pl.BlockSpec(shape, lambda *_: (0,) * nd, pipeline_mode=pl.Buffered(1))


def _sigmoid(x):
    return 1.0 / (1.0 + jnp.exp(-x))


def _gelu_tanh(x):
    return 0.5 * x * (1.0 + jnp.tanh(math.sqrt(2.0 / math.pi) * (x + 0.044715 * (x * x * x))))


def _layernorm(z, g, b):
    mu = jnp.mean(z, axis=-1, keepdims=True)
    zc = z - mu
    var = jnp.mean(zc * zc, axis=-1, keepdims=True)
    return zc * lax.rsqrt(var + LN_EPS) * g + b


def _inproj_body(x_ref, h0_ref, w_ref, u_ref, q_ref, k_ref, v_ref, kb_ref, vb_ref, qi_ref, kiw_ref,
                 ga_ref, gb_ref, *, n_head_tiles):
    if n_head_tiles:
        h = jnp.where(pl.program_id(0) < n_head_tiles, h0_ref[...], x_ref[...])
    else:
        h = x_ref[...]
    hb = h.astype(BF16)

    def seg(a, b):
        return jnp.dot(hb, w_ref[:, a:b], preferred_element_type=F32)

    u_ref[...] = seg(C_U, C_Q)
    q_ref[...] = (seg(C_Q, C_K) * HEAD_DIM ** -0.5).astype(BF16)
    k = seg(C_K, C_V)
    k_ref[...] = k
    kb_ref[...] = k.astype(BF16)
    v = seg(C_V, C_QI)
    v_ref[...] = v
    vb_ref[...] = v.astype(BF16)
    qi_ref[...] = seg(C_QI, C_KIW).astype(BF16)
    kiw_ref[...] = seg(C_KIW, C_GA)
    ga_ref[...] = seg(C_GA, C_GB)
    gb_ref[...] = seg(C_GB, C_END)


def _in_proj(x, h0, wp, *, tm, head_rows):
    n_head_tiles = head_rows // tm
    n_rows = head_rows + x.shape[0]
    grid = (n_rows // tm,)
    if n_head_tiles:
        x_spec = pl.BlockSpec((tm, D_MODEL), lambda i: (jnp.maximum(i - n_head_tiles, 0), 0))
        h_spec = pl.BlockSpec((tm, D_MODEL), lambda i: (jnp.minimum(i, n_head_tiles - 1), 0))
    else:
        h0 = x
        x_spec = pl.BlockSpec((tm, D_MODEL), lambda i: (i, 0))
        h_spec = pl.BlockSpec((tm, D_MODEL), lambda i: (0, 0))

    def row(width):
        return pl.BlockSpec((tm, width), lambda i: (i, 0))

    widths = (512, 512, 512, 512, 512, 512, 256, 128, 1024, 1024)
    dtypes = (F32, BF16, F32, F32, BF16, BF16, BF16, F32, F32, F32)
    return pl.pallas_call(
        functools.partial(_inproj_body, n_head_tiles=n_head_tiles),
        grid=grid,
        in_specs=[x_spec, h_spec, _resident_spec((D_MODEL, C_END))],
        out_specs=[row(w) for w in widths],
        out_shape=[jax.ShapeDtypeStruct((n_rows, w), d) for w, d in zip(widths, dtypes)],
        compiler_params=_cparams(1, VMEM_LIMIT),
        name="in_proj",
    )(x, h0, wp)


SSM_CHUNK = 256


SSM_C_TILE = 128
SSM_C_DEPTH = (SSM_C_TILE // SSM_GROUP_CH) * SSM_STATE
SSM_B_TILE = 256
SSM_B_DEPTH = (SSM_B_TILE // SSM_STATE) * SSM_GROUP_CH


def _ssm_out(h_cat, u, cbig_ref, d_ref, wglu_ref):
    hb = h_cat.astype(BF16)
    tiles = []
    for t in range(SSM_WIDTH // SSM_C_TILE):
        cols = slice(SSM_C_TILE * t, SSM_C_TILE * (t + 1))
        re = slice(SSM_C_DEPTH * t, SSM_C_DEPTH * (t + 1))
        im = slice(N_STATE + SSM_C_DEPTH * t, N_STATE + SSM_C_DEPTH * (t + 1))
        tiles.append(jnp.dot(hb[:, re], cbig_ref[re, cols], preferred_element_type=F32)
                     + jnp.dot(hb[:, im], cbig_ref[im, cols], preferred_element_type=F32))
    y_lin = jnp.concatenate(tiles, axis=1)
    y = _gelu_tanh(y_lin + d_ref[...] * u)
    z = jnp.dot(y.astype(BF16), wglu_ref[...], preferred_element_type=F32)
    return y * _sigmoid(z)


def _ssm_prompt_body(u_ref, bbig_ref, apw_ref, cbig_ref, d_ref, wglu_ref, y_ref, st_ref,
                     bu_ref, hs_ref, hc_ref):
    @pl.when(pl.program_id(0) == 0)
    def _():
        hc_ref[...] = jnp.zeros_like(hc_ref)

    u = u_ref[...]
    ub = u.astype(BF16)
    for j in range(2 * N_STATE // SSM_B_TILE):
        jj = j % (N_STATE // SSM_B_TILE)
        cols = slice(SSM_B_TILE * j, SSM_B_TILE * (j + 1))
        rows = slice(SSM_B_DEPTH * jj, SSM_B_DEPTH * (jj + 1))
        bu_ref[:, cols] = jnp.dot(ub[:, rows], bbig_ref[rows, cols], preferred_element_type=F32)

    def step(r, carry):
        h_re, h_im = carry
        r8 = pl.multiple_of(r * 8, 8)
        x_re = bu_ref[pl.ds(r8, 8), 0:N_STATE]
        x_im = bu_ref[pl.ds(r8, 8), N_STATE:2 * N_STATE]
        for lvl, sh in enumerate((1, 2, 4)):
            a_re = apw_ref[2 * lvl]
            a_im = apw_ref[2 * lvl + 1]
            s_re = pltpu.roll(x_re, sh, 0)
            s_im = pltpu.roll(x_im, sh, 0)
            x_re, x_im = (x_re + (a_re * s_re - a_im * s_im), x_im + (a_re * s_im + a_im * s_re))
        c_re = apw_ref[6]
        c_im = apw_ref[7]
        x_re, x_im = (x_re + (c_re * h_re - c_im * h_im), x_im + (c_re * h_im + c_im * h_re))
        hs_ref[pl.ds(r8, 8), 0:N_STATE] = x_re
        hs_ref[pl.ds(r8, 8), N_STATE:2 * N_STATE] = x_im
        return x_re[7:8, :], x_im[7:8, :]

    h_re, h_im = lax.fori_loop(0, SSM_CHUNK // 8, step, (hc_ref[0:1, :], hc_ref[1:2, :]))
    hc_ref[0:1, :] = h_re
    hc_ref[1:2, :] = h_im
    st_ref[...] = hc_ref[...]
    y_ref[...] = _ssm_out(hs_ref[...], u, cbig_ref, d_ref, wglu_ref).astype(BF16)


def _ssm_prompt(u, bbig, apw, cbig, dskip, wglu):
    n_rows = u.shape[0]
    return pl.pallas_call(
        _ssm_prompt_body,
        grid=(n_rows // SSM_CHUNK,),
        in_specs=[pl.BlockSpec((SSM_CHUNK, SSM_WIDTH), lambda i: (i, 0)),
                  _const_spec((SSM_WIDTH, 2 * N_STATE)), _const_spec((8, 8, N_STATE)),
                  _const_spec((2 * N_STATE, SSM_WIDTH)), _const_spec((1, SSM_WIDTH)),
                  _const_spec((SSM_WIDTH, SSM_WIDTH))],
        out_specs=[pl.BlockSpec((SSM_CHUNK, SSM_WIDTH), lambda i: (i, 0)), _const_spec((2, N_STATE))],
        out_shape=[jax.ShapeDtypeStruct((n_rows, SSM_WIDTH), BF16),
                   jax.ShapeDtypeStruct((2, N_STATE), F32)],
        scratch_shapes=[pltpu.VMEM((SSM_CHUNK, 2 * N_STATE), F32), pltpu.VMEM((SSM_CHUNK, 2 * N_STATE), F32),
                        pltpu.VMEM((2, N_STATE), F32)],
        compiler_params=_cparams(1, VMEM_LIMIT),
        name="ssm_prompt",
    )(u, bbig, apw, cbig, dskip, wglu)


def _ssm_sample_body(u_ref, h0re_ref, h0im_ref, bhi_ref, blo_ref, a_ref, cbig_ref, d_ref, wglu_ref,
                     y_ref, hre_ref, him_ref):
    a_re = a_ref[0:1, :]
    a_im = a_ref[1:2, :]
    h_re = h0re_ref[...]
    h_im = h0im_ref[...]
    for s in range(u_ref.shape[0]):
        u = u_ref[s]
        u_hi = u.astype(BF16)
        u_lo = (u - u_hi.astype(F32)).astype(BF16)
        bu = (jnp.dot(u_hi, bhi_ref[...], preferred_element_type=F32)
              + (jnp.dot(u_lo, bhi_ref[...], preferred_element_type=F32)
                 + jnp.dot(u_hi, blo_ref[...], preferred_element_type=F32)))
        h_re, h_im = (a_re * h_re - a_im * h_im + bu[:, 0:N_STATE],
                      a_re * h_im + a_im * h_re + bu[:, N_STATE:2 * N_STATE])
        h_cat = jnp.concatenate([h_re, h_im], axis=1)
        y_ref[s] = _ssm_out(h_cat, u, cbig_ref, d_ref, wglu_ref).astype(BF16)
    hre_ref[...] = h_re
    him_ref[...] = h_im


def _ssm_sample(u_s, h0_re, h0_im, bhi, blo, a_pair, cbig, dskip, wglu):
    n_s, n_b, _ = u_s.shape
    return pl.pallas_call(
        _ssm_sample_body,
        grid=(1,),
        in_specs=[_const_spec(u_s.shape), _const_spec(h0_re.shape), _const_spec(h0_im.shape),
                  _const_spec(bhi.shape), _const_spec(blo.shape), _const_spec(a_pair.shape),
                  _const_spec(cbig.shape), _const_spec(dskip.shape), _const_spec(wglu.shape)],
        out_specs=[_const_spec((n_s, n_b, SSM_WIDTH)), _const_spec((n_b, N_STATE)), _const_spec((n_b, N_STATE))],
        out_shape=[jax.ShapeDtypeStruct((n_s, n_b, SSM_WIDTH), BF16),
                   jax.ShapeDtypeStruct((n_b, N_STATE), F32), jax.ShapeDtypeStruct((n_b, N_STATE), F32)],
        compiler_params=_cparams(1, VMEM_LIMIT),
        name="ssm_sample",
    )(u_s, h0_re, h0_im, bhi, blo, a_pair, cbig, dskip, wglu)


def _key_to_float(key):
    bits = key ^ ((key >> 31) & 0x7FFFFFFF)
    return lax.bitcast_convert_type(bits, F32)


def _float_to_key(x):
    bits = lax.bitcast_convert_type(x, I32)
    return bits ^ ((bits >> 31) & 0x7FFFFFFF)


BISECT_EVERY = 8
SELECT_MAX_PASSES = 32 * BISECT_EVERY + 8


def _kth_largest(count_fn, n_all, lane_max):
    kk = float(TOPK)
    rows = LANES

    def to_cols(vec):
        return jnp.broadcast_to(vec, (LANES, rows)).T

    def total(th_vec, strict):
        return jnp.sum(count_fn(to_cols(th_vec), strict).T, axis=0, keepdims=True)

    row_max = jnp.max(lane_max.T, axis=0, keepdims=True)
    zero = jnp.zeros((1, rows), F32)
    c_ge0 = total(zero, False)
    c_gt0 = total(zero, True)
    pos = c_gt0 > kk
    at0 = (c_ge0 >= kk) & (c_gt0 <= kk)
    lo0 = jnp.where(pos | at0, 0, KEY_NEG_INF).astype(I32)
    hi0 = jnp.where(pos, _float_to_key(row_max) + 1, jnp.where(at0, 0, -1)).astype(I32)
    c_lo0 = jnp.where(pos, c_gt0, jnp.where(at0, c_ge0, n_all))
    c_hi0 = jnp.where(pos, 0.0, jnp.where(at0, c_gt0, c_ge0))
    one = jnp.ones((1, rows), F32)

    def mid_key(lo, hi):
        return (lo >> 1) + (hi >> 1) + (lo & hi & 1)

    def finished(lo, hi, c_lo):
        return (c_lo == kk) | (mid_key(lo, hi) == lo)

    def cond(st):
        it, lo, hi, c_lo = st[0], st[1], st[2], st[3]
        open_rows = jnp.where(finished(lo, hi, c_lo), 0.0, 1.0)
        return (it < SELECT_MAX_PASSES) & (jnp.max(open_rows) > 0.0)

    def body(st):
        it, lo, hi, c_lo, c_hi, w_lo, w_hi, side = st
        done = finished(lo, hi, c_lo)
        lo_v = _key_to_float(lo)
        hi_v = _key_to_float(hi)
        g_lo = jnp.log(c_lo * (1.0 / (kk - 0.5))) * w_lo
        g_hi = jnp.log(jnp.maximum(c_hi, 0.5) * (1.0 / (kk - 0.5))) * w_hi
        th_i = lo_v + (g_lo / (g_lo - g_hi)) * (hi_v - lo_v)
        bounded = (lo_v > NEG_INF) & (hi_v < float("inf"))
        k_i = jnp.minimum(jnp.maximum(_float_to_key(th_i), lo + 1), hi - 1)
        k_mid = mid_key(lo, hi)
        probe = jnp.where((it & (BISECT_EVERY - 1)) == BISECT_EVERY - 1, k_mid, jnp.where(bounded, k_i, k_mid))
        c = total(_key_to_float(probe), False)
        ge = c >= kk
        mv_lo = (~done) & ge
        mv_hi = (~done) & (~ge)
        w_hi = jnp.where(mv_lo, jnp.where(side > 0.0, w_hi * 0.5, one), jnp.where(mv_hi, one, w_hi))
        w_lo = jnp.where(mv_hi, jnp.where(side < 0.0, w_lo * 0.5, one), jnp.where(mv_lo, one, w_lo))
        side = jnp.where(mv_lo, 1.0, jnp.where(mv_hi, -1.0, side))
        return (it + 1, jnp.where(mv_lo, probe, lo), jnp.where(mv_hi, probe, hi),
                jnp.where(mv_lo, c, c_lo), jnp.where(mv_hi, c, c_hi), w_lo, w_hi, side)

    st = lax.while_loop(cond, lambda s: body(body(s)), (jnp.int32(0), lo0, hi0, c_lo0, c_hi0, one, one, zero))
    lo, c_lo, c_hi = st[1], st[3], st[4]
    v = _key_to_float(lo)
    need = jnp.where(v == NEG_INF, 0.0, jnp.where(c_lo == kk, kk, kk - c_hi))
    return to_cols(v)[:, 0:1], to_cols(need)[:, 0:1]


def _topk_mask(x, v, need, off, tri):
    eq = x == v
    cum = jnp.dot(jnp.where(eq, 1.0, 0.0).astype(tri.dtype), tri, preferred_element_type=F32)
    sel = (x > v) | (eq & ((cum + off) <= need))
    w = x.shape[1]
    return sel, off + cum[:, w - 1:w]


def _pattn_body(qi_ref, kiw_ref, q_ref, kit_ref, kt_ref, v_ref, tri_ref, o_ref,
                sc_ref, m_ref, l_ref, acc_ref, mx_ref, *, first_block):
    i = pl.program_id(0)
    row0 = i * TQ

    @pl.when(i < first_block)
    def _():
        o_ref[...] = jnp.zeros_like(o_ref)

    @pl.when(i >= first_block)
    def _():
        jd = row0 // TK
        w = kiw_ref[:, IDX_DIM:IDX_DIM + IDX_HEADS] * (IDX_HEADS ** -0.5 * IDX_DIM ** -0.5)
        qi_h = [qi_ref[:, IDX_DIM * h:IDX_DIM * (h + 1)] for h in range(IDX_HEADS)]
        q_h = [q_ref[:, HEAD_DIM * h:HEAD_DIM * (h + 1)] for h in range(N_HEADS)]

        def score_tile(c0, width, masked, mx):
            kit = kit_ref[:, pl.ds(c0, width)]
            tot = None
            for h in range(IDX_HEADS):
                z = jnp.dot(qi_h[h], kit, preferred_element_type=F32)
                r = jnp.maximum(z, 0.0) * w[:, h:h + 1]
                tot = r if tot is None else tot + r
            if masked:
                s_idx = c0 + lax.broadcasted_iota(I32, (TQ, width), 1)
                t_idx = row0 + lax.broadcasted_iota(I32, (TQ, width), 0)
                tot = jnp.where((s_idx <= t_idx) & (s_idx >= PADF), tot, NEG_INF)
            sc_ref[:, pl.ds(c0, width)] = tot
            for a in range(width // LANES):
                mx = jnp.maximum(mx, tot[:, LANES * a:LANES * (a + 1)])
            return mx

        def main_start(j):
            return pl.multiple_of(j * TK, TK)

        mx = score_tile(HEAD_KEY0, LANES, True, jnp.full((TQ, LANES), NEG_INF, F32))
        mx = lax.fori_loop(1, jd, lambda j, mx: score_tile(main_start(j), TK, False, mx), mx)
        mx_ref[...] = mx

        @pl.when(jd >= 1)
        def _():
            mx_ref[...] = score_tile(main_start(jd), TK, True, mx)


        def count_fn(th, strict):
            def add(cnt, x):
                return cnt + jnp.where((x > th) if strict else (x >= th), 1.0, 0.0)

            def count_main(j, cnt):
                c0 = main_start(j)
                for a in range(TK // LANES):
                    cnt = add(cnt, sc_ref[:, pl.ds(c0 + a * LANES, LANES)])
                return cnt

            cnt = add(jnp.zeros((TQ, LANES), F32), sc_ref[:, HEAD_KEY0:HEAD_ROWS])
            return lax.fori_loop(1, jd + 1, count_main, cnt)

        v, need = _kth_largest(count_fn, (LANES + TK * jd).astype(F32), mx_ref[...])

        m_ref[...] = jnp.full_like(m_ref, M_INIT)
        l_ref[...] = jnp.zeros_like(l_ref)
        acc_ref[...] = jnp.zeros_like(acc_ref)
        first_half = lax.broadcasted_iota(I32, (TQ, LANES), 1) < HEAD_DIM

        def attend_tile(c0, width, off):
            x = sc_ref[:, pl.ds(c0, width)]
            parts = []
            for a in range(0, width, TIE_W):
                wa = min(TIE_W, width - a)
                sel, off = _topk_mask(x[:, a:a + wa], v, need, off, tri_ref[0:wa, 0:wa])
                parts.append(jnp.where(sel, 0.0, NEG_INF))
            bias = parts[0] if len(parts) == 1 else jnp.concatenate(parts, axis=1)
            n_sl = width // LANES

            def probs(h):
                kt = kt_ref[HEAD_DIM * h:HEAD_DIM * (h + 1), pl.ds(c0, width)]
                s = jnp.dot(q_h[h], kt, preferred_element_type=F32) + bias
                sl = [s[:, LANES * a:LANES * (a + 1)] for a in range(n_sl)]
                pm = sl[0]
                for a in range(1, n_sl):
                    pm = jnp.maximum(pm, sl[a])
                m_old = m_ref[h]
                m_new = jnp.maximum(m_old, jnp.max(pm, axis=1, keepdims=True))
                alpha = jnp.exp(m_old - m_new)
                ps = [jnp.exp(sl[a] - m_new) for a in range(n_sl)]
                psum = ps[0]
                for a in range(1, n_sl):
                    psum = psum + ps[a]
                l_ref[h] = alpha * l_ref[h] + psum
                m_ref[h] = m_new
                return (ps[0] if n_sl == 1 else jnp.concatenate(ps, axis=1)).astype(BF16), alpha

            for j in range(N_HEADS // 2):
                p_e, alpha_e = probs(2 * j)
                p_o, alpha_o = probs(2 * j + 1)
                cols = slice(LANES * j, LANES * (j + 1))
                pv = jnp.dot(jnp.concatenate([p_e, p_o], axis=0), v_ref[pl.ds(c0, width), cols],
                             preferred_element_type=F32)
                a_pair = jnp.where(first_half, alpha_e, alpha_o)
                pv_pair = jnp.where(first_half, pv[0:TQ], pv[TQ:2 * TQ])
                acc_ref[:, cols] = a_pair * acc_ref[:, cols] + pv_pair
            return off

        off = attend_tile(HEAD_KEY0, LANES, jnp.zeros((TQ, 1), F32))
        lax.fori_loop(1, jd + 1, lambda j, off: attend_tile(main_start(j), TK, off), off)

        for j in range(N_HEADS // 2):
            l_e = jnp.sum(l_ref[2 * j], axis=1, keepdims=True)
            l_o = jnp.sum(l_ref[2 * j + 1], axis=1, keepdims=True)
            l_pair = jnp.where(first_half, l_e, l_o)
            inv = 1.0 / jnp.where(l_pair > 0.0, l_pair, 1.0)
            o_ref[:, LANES * j:LANES * (j + 1)] = (acc_ref[:, LANES * j:LANES * (j + 1)] * inv).astype(o_ref.dtype)


def _prompt_attention(qi, kiw, q, kit, kt, vb, tri):
    n_rows = q.shape[0]

    def row(width):
        return pl.BlockSpec((TQ, width), lambda i: (i, 0))

    return pl.pallas_call(
        functools.partial(_pattn_body, first_block=HEAD_KEY0 // TQ),
        grid=(n_rows // TQ,),
        in_specs=[row(IDX_HEADS * IDX_DIM), row(128), row(ATT_WIDTH),
                  _resident_spec(kit.shape), _resident_spec(kt.shape), _resident_spec(vb.shape),
                  _resident_spec(tri.shape)],
        out_specs=row(ATT_WIDTH),
        out_shape=jax.ShapeDtypeStruct((n_rows, ATT_WIDTH), BF16),
        scratch_shapes=[pltpu.VMEM((TQ, n_rows), F32), pltpu.VMEM((N_HEADS, TQ, LANES), F32),
                        pltpu.VMEM((N_HEADS, TQ, LANES), F32), pltpu.VMEM((TQ, ATT_WIDTH), F32),
                        pltpu.VMEM((TQ, LANES), F32)],
        compiler_params=_cparams(1, VMEM_LIMIT),
        name="prompt_attention",
    )(qi, kiw, q, kit, kt, vb, tri)


PAGES_PER_STEP = 32
S_CHUNK = PAGES_PER_STEP * PAGE
S_ROWS = 8


def _idx_scores_rows(qi, w, keys_bf16):
    z = lax.dot_general(qi, keys_bf16, (((1,), (1,)), ((), ())), preferred_element_type=F32)
    r = jnp.maximum(z, 0.0) * w
    tot = r[0:S_ROWS]
    for h in range(1, IDX_HEADS):
        tot = tot + r[S_ROWS * h:S_ROWS * (h + 1)]
    return tot


def _sscore_body(pt_ref, qi_ref, w_ref, kinew_ref, *refs, n_chunks, past):
    page_refs = refs[:PAGES_PER_STEP]
    sc_ref, kic_ref = refs[PAGES_PER_STEP:]
    c = pl.program_id(1)
    qi = qi_ref[...]
    w = w_ref[...]
    for r in range(PAGES_PER_STEP):
        kic_ref[:, r * PAGE:(r + 1) * PAGE] = page_refs[r][...].astype(BF16)
    rr = jnp.maximum(jnp.dot(qi, kic_ref[...], preferred_element_type=F32), 0.0) * w
    tot = rr[0:S_ROWS]
    for h in range(1, IDX_HEADS):
        tot = tot + rr[S_ROWS * h:S_ROWS * (h + 1)]
    sc_ref[:, pl.ds(pl.multiple_of(c * S_CHUNK, S_CHUNK), S_CHUNK)] = tot

    @pl.when(c == n_chunks - 1)
    def _():
        tot = _idx_scores_rows(qi, w, kinew_ref[...])
        lane = lax.broadcasted_iota(I32, (S_ROWS, PAGE), 1)
        row = lax.broadcasted_iota(I32, (S_ROWS, PAGE), 0)
        sc_ref[:, past:past + PAGE] = jnp.where(lane <= row, tot, NEG_INF)


def _sselect_body(sc_ref, par_ref, *, sel_tile):
    n_tiles = sc_ref.shape[1] // sel_tile

    def tiles(fn, init):
        def body(j, acc):
            c0 = pl.multiple_of(j * sel_tile, LANES)
            for a in range(sel_tile // LANES):
                acc = fn(acc, sc_ref[:, pl.ds(c0 + a * LANES, LANES)])
            return acc

        return lax.fori_loop(0, n_tiles, body, init)

    def count_fn(th, strict):
        return tiles(lambda cnt, x: cnt + jnp.where((x > th) if strict else (x >= th), 1.0, 0.0),
                     jnp.zeros((LANES, LANES), F32))

    lane_max = tiles(jnp.maximum, jnp.full((LANES, LANES), NEG_INF, F32))
    v, need = _kth_largest(count_fn, float(sc_ref.shape[1]), lane_max)
    lane = lax.broadcasted_iota(I32, (LANES, LANES), 1)
    par_ref[...] = jnp.where(lane == 0, v, jnp.where(lane == 1, need, 0.0))


def _sample_select(scores2d):
    n_rows, width = scores2d.shape
    sel_tile = max(t for t in range(LANES, 6 * LANES, LANES) if width % t == 0)
    pad = -n_rows % LANES
    if pad:
        scores2d = jnp.pad(scores2d, [(0, pad), (0, 0)])
    par = pl.pallas_call(
        functools.partial(_sselect_body, sel_tile=sel_tile),
        grid=((n_rows + pad) // LANES,),
        in_specs=[pl.BlockSpec((LANES, width), lambda i: (i, 0))],
        out_specs=pl.BlockSpec((LANES, LANES), lambda i: (i, 0)),
        out_shape=jax.ShapeDtypeStruct((n_rows + pad, LANES), F32),
        compiler_params=_cparams(1, VMEM_LIMIT),
        name="sample_select",
    )(scores2d)
    return par[:n_rows]


def _sample_scores(page_table, qi_s, w_s, kinew, cache_kidx):
    n_b, n_pages = page_table.shape
    n_chunks = n_pages // PAGES_PER_STEP
    past = n_pages * PAGE
    width = past + PAGE

    def page_spec(r):
        return pl.BlockSpec((None, IDX_DIM, PAGE), lambda b, c, pt: (pt[b, c * PAGES_PER_STEP + r], 0, 0))

    def per_b(shape):
        return pl.BlockSpec((None,) + shape, lambda b, c, pt: (b, 0, 0))

    grid_spec = pltpu.PrefetchScalarGridSpec(
        num_scalar_prefetch=1,
        grid=(n_b, n_chunks),
        in_specs=[per_b((IDX_HEADS * S_ROWS, IDX_DIM)), per_b((IDX_HEADS * S_ROWS, 1)), per_b((PAGE, IDX_DIM))]
                 + [page_spec(r) for r in range(PAGES_PER_STEP)],
        out_specs=per_b((S_ROWS, width)),
        scratch_shapes=[pltpu.VMEM((IDX_DIM, S_CHUNK), BF16)],
    )
    scores = pl.pallas_call(
        functools.partial(_sscore_body, n_chunks=n_chunks, past=past),
        grid_spec=grid_spec,
        out_shape=jax.ShapeDtypeStruct((n_b, S_ROWS, width), F32),
        compiler_params=_cparams(2, VMEM_LIMIT),
        name="sample_scores",
    )(page_table, qi_s, w_s, kinew, *([cache_kidx] * PAGES_PER_STEP))
    par = _sample_select(scores.reshape(n_b * S_ROWS, width)).reshape(n_b, S_ROWS, LANES)
    return scores, par


S_TIE = 256


def _sattn_body(pt_ref, q_ref, sc_ref, scn_ref, par_ref, knew_ref, vnew_ref, tri_ref, *refs, n_chunks):
    k_refs = refs[:PAGES_PER_STEP]
    v_refs = refs[PAGES_PER_STEP:2 * PAGES_PER_STEP]
    o_ref, qbd_ref, kc_ref, vc_ref, m_ref, l_ref, acc_ref, off_ref = refs[2 * PAGES_PER_STEP:]
    c = pl.program_id(1)
    n_q = 4
    n_rows = n_q * N_HEADS
    head_of_col = lax.broadcasted_iota(I32, (N_HEADS, ATT_WIDTH), 1) // HEAD_DIM
    own_head = head_of_col == lax.broadcasted_iota(I32, (N_HEADS, ATT_WIDTH), 0)

    @pl.when(c == 0)
    def _():
        for s in range(n_q):
            qs = jnp.broadcast_to(q_ref[s:s + 1, :], (N_HEADS, ATT_WIDTH))
            qbd_ref[N_HEADS * s:N_HEADS * (s + 1), :] = jnp.where(own_head, qs, 0.0)
        m_ref[...] = jnp.full_like(m_ref, M_INIT)
        l_ref[...] = jnp.zeros_like(l_ref)
        acc_ref[...] = jnp.zeros_like(acc_ref)
        off_ref[...] = jnp.zeros_like(off_ref)

    v_thr = par_ref[:, 0:1]
    need = par_ref[:, 1:2]
    qbd = qbd_ref[...].astype(BF16)

    nt_dims = (((1,), (1,)), ((), ()))

    def attend(s_raw, pv_fn, x):
        width = x.shape[1]
        off = off_ref[...]
        parts = []
        for a in range(0, width, S_TIE):
            wa = min(S_TIE, width - a)
            sel, off = _topk_mask(x[:, a:a + wa], v_thr, need, off, tri_ref[0:wa, 0:wa])
            parts.append(jnp.where(sel, 0.0, NEG_INF))
        off_ref[...] = off
        bias8 = parts[0] if len(parts) == 1 else jnp.concatenate(parts, axis=1)
        bias = jnp.concatenate([jnp.broadcast_to(bias8[s:s + 1, :], (N_HEADS, width)) for s in range(n_q)], axis=0)
        s_mat = s_raw + bias
        m_old = m_ref[...]
        m_new = jnp.maximum(m_old, jnp.max(s_mat, axis=1, keepdims=True))
        p = jnp.exp(s_mat - m_new)
        alpha = jnp.exp(m_old - m_new)
        l_ref[...] = alpha * l_ref[...] + jnp.sum(p, axis=1, keepdims=True)
        acc_ref[...] = alpha * acc_ref[...] + pv_fn(p.astype(BF16))
        m_ref[...] = m_new

    for r in range(PAGES_PER_STEP):
        kc_ref[:, r * PAGE:(r + 1) * PAGE] = k_refs[r][...].astype(BF16)
        vc_ref[:, r * PAGE:(r + 1) * PAGE] = v_refs[r][...].astype(BF16)
    attend(jnp.dot(qbd, kc_ref[...], preferred_element_type=F32),
           lambda p: lax.dot_general(p, vc_ref[...], nt_dims, preferred_element_type=F32), sc_ref[...])

    @pl.when(c == n_chunks - 1)
    def _():
        pad = jnp.zeros((PAGE - S_ROWS, ATT_WIDTH), F32)
        k_new = jnp.concatenate([knew_ref[...], pad], axis=0).astype(BF16)
        v_new = jnp.concatenate([vnew_ref[...], pad], axis=0).astype(BF16)
        attend(lax.dot_general(qbd, k_new, nt_dims, preferred_element_type=F32),
               lambda p: jnp.dot(p, v_new, preferred_element_type=F32), scn_ref[...])
        l = l_ref[...]
        o_all = acc_ref[...] * (1.0 / jnp.where(l > 0.0, l, 1.0))
        row = lax.broadcasted_iota(I32, (S_ROWS, ATT_WIDTH), 0)
        out = jnp.zeros((S_ROWS, ATT_WIDTH), F32)
        for s in range(n_q):
            o_s = jnp.sum(jnp.where(own_head, o_all[N_HEADS * s:N_HEADS * (s + 1), :], 0.0), axis=0, keepdims=True)
            out = jnp.where(row == s, jnp.broadcast_to(o_s, (S_ROWS, ATT_WIDTH)), out)
        o_ref[...] = out


def _sample_attention(page_table, q_s, scores, par, knew, vnew, tri, cache_k, cache_v):
    n_b, n_pages = page_table.shape
    n_chunks = n_pages // PAGES_PER_STEP
    n_rows = 4 * N_HEADS

    def page_spec(r):
        return pl.BlockSpec((None, ATT_WIDTH, PAGE), lambda b, c, pt: (pt[b, c * PAGES_PER_STEP + r], 0, 0))

    def per_b(shape):
        return pl.BlockSpec((None,) + shape, lambda b, c, pt: (b, 0, 0))

    grid_spec = pltpu.PrefetchScalarGridSpec(
        num_scalar_prefetch=1,
        grid=(n_b, n_chunks),
        in_specs=[per_b((S_ROWS, ATT_WIDTH)),
                  pl.BlockSpec((None, S_ROWS, S_CHUNK), lambda b, c, pt: (b, 0, c)),
                  pl.BlockSpec((None, S_ROWS, PAGE), lambda b, c, pt: (b, 0, n_pages)),
                  per_b((S_ROWS, PAGE)), per_b((S_ROWS, ATT_WIDTH)), per_b((S_ROWS, ATT_WIDTH)),
                  pl.BlockSpec((S_TIE, S_TIE), lambda b, c, pt: (0, 0))]
                 + [page_spec(r) for r in range(PAGES_PER_STEP)] * 2,
        out_specs=per_b((S_ROWS, ATT_WIDTH)),
        scratch_shapes=[pltpu.VMEM((n_rows, ATT_WIDTH), F32),
                        pltpu.VMEM((ATT_WIDTH, S_CHUNK), BF16), pltpu.VMEM((ATT_WIDTH, S_CHUNK), BF16),
                        pltpu.VMEM((n_rows, 1), F32), pltpu.VMEM((n_rows, 1), F32),
                        pltpu.VMEM((n_rows, ATT_WIDTH), F32), pltpu.VMEM((S_ROWS, 1), F32)],
    )
    return pl.pallas_call(
        functools.partial(_sattn_body, n_chunks=n_chunks),
        grid_spec=grid_spec,
        out_shape=jax.ShapeDtypeStruct((n_b, S_ROWS, ATT_WIDTH), F32),
        compiler_params=_cparams(2, VMEM_LIMIT),
        name="sample_attention",
    )(page_table, q_s, scores, scores, par, knew, vnew, tri,
      *([cache_k] * PAGES_PER_STEP), *([cache_v] * PAGES_PER_STEP))


R_LANES = 128
R_E0 = N_EGROUPS


def _route(logits):
    lane = lax.broadcasted_iota(I32, logits.shape, 1)
    lane_f = lane.astype(F32)
    big = float(R_LANES)
    is_g = lane < N_EGROUPS
    gl = jnp.where(is_g, logits, NEG_INF)
    g_max = jnp.max(gl, axis=1, keepdims=True)
    g_sel = jnp.min(jnp.where(gl == g_max, lane_f, big), axis=1, keepdims=True)
    p_g = 1.0 / jnp.sum(jnp.exp(gl - g_max), axis=1, keepdims=True)
    member = (lane >= R_E0) & (lane < R_E0 + N_EXPERTS) & (((lane - R_E0) >> 2).astype(F32) == g_sel)
    el = jnp.where(member, logits, NEG_INF)
    e_max = jnp.max(el, axis=1, keepdims=True)
    pe = jnp.exp(el - e_max)
    pe = pe / jnp.sum(pe, axis=1, keepdims=True)
    p1 = jnp.max(pe, axis=1, keepdims=True)
    i1 = jnp.min(jnp.where(member & (pe == p1), lane_f, big), axis=1, keepdims=True)
    rest = member & (lane_f != i1)
    p2 = jnp.max(jnp.where(rest, pe, -1.0), axis=1, keepdims=True)
    i2 = jnp.min(jnp.where(rest & (pe == p2), lane_f, big), axis=1, keepdims=True)
    tot = p1 + p2
    return jnp.where(lane_f == i1, p1 / tot * p_g, 0.0) + jnp.where(lane_f == i2, p2 / tot * p_g, 0.0)


def _mix_body(x_ref, h0_ref, ys_ref, oa_ref, ga_ref, gb_ref, wso_ref, wao_ref, wo_ref, g1_ref, b1_ref,
              wrh_ref, wrl_ref, br_ref, h1_ref, h1b_ref, comb_ref, *, n_head_tiles):
    if n_head_tiles:
        h = jnp.where(pl.program_id(0) < n_head_tiles, h0_ref[...], x_ref[...])
    else:
        h = x_ref[...]
    a = jnp.dot(ys_ref[...], wso_ref[...], preferred_element_type=F32)
    b = jnp.dot(oa_ref[...], wao_ref[...], preferred_element_type=F32)
    m = _sigmoid(ga_ref[...]) * a + _sigmoid(gb_ref[...]) * b
    mix = jnp.dot(m.astype(BF16), wo_ref[...], preferred_element_type=F32)
    h1 = _layernorm(DN_ALPHA * h + mix, g1_ref[...], b1_ref[...])
    h1_ref[...] = h1
    h_hi = h1.astype(BF16)
    h1b_ref[...] = h_hi
    h_lo = (h1 - h_hi.astype(F32)).astype(BF16)
    logits = (jnp.dot(h_hi, wrh_ref[...], preferred_element_type=F32)
              + (jnp.dot(h_lo, wrh_ref[...], preferred_element_type=F32)
                 + jnp.dot(h_hi, wrl_ref[...], preferred_element_type=F32))) + br_ref[...]
    comb_ref[...] = _route(logits)


def _mix_ln_route(x, h0, ys, oa, ga, gb, wso, wao, wo, g1, b1, wrh, wrl, br, *, tm, head_rows):
    n_head_tiles = head_rows // tm
    n_rows = head_rows + x.shape[0]
    if n_head_tiles:
        x_spec = pl.BlockSpec((tm, D_MODEL), lambda i: (jnp.maximum(i - n_head_tiles, 0), 0))
        h_spec = pl.BlockSpec((tm, D_MODEL), lambda i: (jnp.minimum(i, n_head_tiles - 1), 0))
    else:
        h0 = x
        x_spec = pl.BlockSpec((tm, D_MODEL), lambda i: (i, 0))
        h_spec = pl.BlockSpec((tm, D_MODEL), lambda i: (0, 0))

    def row(width):
        return pl.BlockSpec((tm, width), lambda i: (i, 0))

    consts = (wso, wao, wo, g1, b1, wrh, wrl, br)
    return pl.pallas_call(
        functools.partial(_mix_body, n_head_tiles=n_head_tiles),
        grid=(n_rows // tm,),
        in_specs=[x_spec, h_spec, row(SSM_WIDTH), row(ATT_WIDTH), row(D_MODEL), row(D_MODEL)]
                 + [_const_spec(c.shape) for c in consts],
        out_specs=[row(D_MODEL), row(D_MODEL), row(R_LANES)],
        out_shape=[jax.ShapeDtypeStruct((n_rows, D_MODEL), F32), jax.ShapeDtypeStruct((n_rows, D_MODEL), BF16),
                   jax.ShapeDtypeStruct((n_rows, R_LANES), F32)],
        compiler_params=_cparams(1, VMEM_LIMIT),
        name="mix_ln_route",
    )(x, h0, ys, oa, ga, gb, *consts)


def _moe_body(h1_ref, h1b_ref, comb_ref, wgu_ref, wd_ref, g2_ref, b2_ref, y_ref, acc_ref):
    e = pl.program_id(1)

    @pl.when(e == 0)
    def _():
        acc_ref[...] = jnp.zeros_like(acc_ref)

    gu = jnp.dot(h1b_ref[...], wgu_ref[...], preferred_element_type=F32)
    gate = gu[:, 0:D_EXPERT]
    up = gu[:, D_EXPERT:2 * D_EXPERT]
    act = gate * _sigmoid(gate) * up
    comb = comb_ref[...]
    lane = lax.broadcasted_iota(I32, comb.shape, 1)
    c_e = jnp.sum(jnp.where(lane == e + R_E0, comb, 0.0), axis=1, keepdims=True)
    acc_ref[...] += jnp.dot((act * c_e).astype(BF16), wd_ref[...], preferred_element_type=F32)

    @pl.when(e == N_EXPERTS - 1)
    def _():
        y_ref[...] = _layernorm(DN_ALPHA * h1_ref[...] + acc_ref[...], g2_ref[...], b2_ref[...])


def _moe_ln(h1, h1b, comb, wgu, wd, g2, b2, *, tm, head_rows):
    n_head_tiles = head_rows // tm
    n_rows = h1.shape[0]
    out_rows = n_rows - head_rows

    def row(width):
        return pl.BlockSpec((tm, width), lambda i, e: (i, 0))

    return pl.pallas_call(
        _moe_body,
        grid=(n_rows // tm, N_EXPERTS),
        in_specs=[row(D_MODEL), row(D_MODEL), row(R_LANES),
                  pl.BlockSpec((None, D_MODEL, 2 * D_EXPERT), lambda i, e: (e, 0, 0)),
                  pl.BlockSpec((None, D_EXPERT, D_MODEL), lambda i, e: (e, 0, 0)),
                  pl.BlockSpec((1, D_MODEL), lambda i, e: (0, 0)), pl.BlockSpec((1, D_MODEL), lambda i, e: (0, 0))],
        out_specs=pl.BlockSpec((tm, D_MODEL), lambda i, e: (jnp.maximum(i - n_head_tiles, 0), 0)),
        out_shape=jax.ShapeDtypeStruct((out_rows, D_MODEL), F32),
        scratch_shapes=[pltpu.VMEM((tm, D_MODEL), F32)],
        compiler_params=_cparams(2, VMEM_LIMIT),
        name="moe_ln",
    )(h1, h1b, comb, wgu, wd, g2, b2)


def _ssm_tables(a_re, a_im, log_dt, b_re, b_im, c_re, c_im):
    dt = jnp.exp(log_dt)[:, None]
    mag = jnp.exp(a_re * dt)
    ab_re = mag * jnp.cos(a_im * dt)
    ab_im = mag * jnp.sin(a_im * dt)
    den = a_re * a_re + a_im * a_im
    nr = ab_re - 1.0
    f_re = (nr * a_re + ab_im * a_im) / den
    f_im = (ab_im * a_re - nr * a_im) / den
    bb_re = f_re[..., None] * b_re - f_im[..., None] * b_im
    bb_im = f_re[..., None] * b_im + f_im[..., None] * b_re
    eye = jnp.eye(SSM_GROUPS, dtype=F32)

    def in_mat(bb):
        return jnp.einsum("gpm,gh->gmhp", bb, eye).reshape(SSM_WIDTH, N_STATE)

    def out_mat(cc):
        return jnp.einsum("gmp,gh->gphm", cc, eye).reshape(N_STATE, SSM_WIDTH)

    bbig = jnp.concatenate([in_mat(bb_re), in_mat(bb_im)], axis=1)
    cbig = jnp.concatenate([out_mat(c_re), -out_mat(c_im)], axis=0)
    ar = ab_re.reshape(1, N_STATE)
    ai = ab_im.reshape(1, N_STATE)

    def cmul(x, y):
        return x[0] * y[0] - x[1] * y[1], x[0] * y[1] + x[1] * y[0]

    a1 = (ar, ai)
    pows = [a1]
    for _ in range(7):
        pows.append(cmul(pows[-1], a1))
    row = jnp.arange(8)[:, None]
    tabs = []
    for sh, pw in ((1, pows[0]), (2, pows[1]), (4, pows[3])):
        keep = (row >= sh).astype(F32)
        tabs += [keep * pw[0], keep * pw[1]]
    tabs += [jnp.concatenate([p[0] for p in pows], axis=0), jnp.concatenate([p[1] for p in pows], axis=0)]
    apw = jnp.stack(tabs)
    a_pair = jnp.concatenate([ar, ai], axis=0)
    return bbig, cbig, apw, a_pair


def _split_bf16(x):
    hi = x.astype(BF16)
    return hi, (x - hi.astype(F32)).astype(BF16)


def kernel(x_prompt, x_sample, cache_k, cache_v, cache_kidx, state_ssm_re, state_ssm_im, page_table,
           meta_tokens, w_in, ssm_a_re, ssm_a_im, ssm_log_dt, ssm_b_re, ssm_b_im, ssm_c_re, ssm_c_im,
           ssm_d, w_glu, w_ssm_out, w_att_out, w_o, ln1_g, ln1_b, w_route_group, b_route_group,
           w_route_expert, b_route_expert, w_exp_gate, w_exp_up, w_exp_down, ln2_g, ln2_b):
    depth = w_in.shape[0]
    assert depth == 1 and x_prompt.shape[0] == 1
    n_b, n_s, _ = x_sample.shape
    seq = x_prompt.shape[1]
    t_len = seq + N_META
    lyr = 0

    cuts = [0]
    for c in IN_SPLITS:
        cuts.append(cuts[-1] + c)
    w = w_in[lyr]
    wp = jnp.concatenate([w[:, :cuts[7]], jnp.zeros((D_MODEL, C_GA - C_KIW - IDX_DIM - IDX_HEADS), F32),
                          w[:, cuts[7]:]], axis=1).astype(BF16)
    bbig, cbig, apw, a_pair = _ssm_tables(ssm_a_re[lyr], ssm_a_im[lyr], ssm_log_dt[lyr], ssm_b_re[lyr],
                                          ssm_b_im[lyr], ssm_c_re[lyr], ssm_c_im[lyr])
    bbig_hi, bbig_lo = _split_bf16(bbig)
    cbig_b = cbig.astype(BF16)
    dskip = ssm_d[lyr].reshape(1, SSM_WIDTH)
    wglu_b = w_glu[lyr].astype(BF16)
    wso = w_ssm_out[lyr].astype(BF16)
    wao = w_att_out[lyr].astype(BF16)
    wo = w_o[lyr].astype(BF16)
    g1 = ln1_g[lyr].reshape(1, D_MODEL)
    b1 = ln1_b[lyr].reshape(1, D_MODEL)
    g2 = ln2_g[lyr].reshape(1, D_MODEL)
    b2 = ln2_b[lyr].reshape(1, D_MODEL)
    r_pad = R_LANES - N_EGROUPS - N_EXPERTS
    wr = jnp.concatenate([w_route_group[lyr], w_route_expert[lyr], jnp.zeros((D_MODEL, r_pad), F32)], axis=1)
    wrh, wrl = _split_bf16(wr)
    br = jnp.concatenate([b_route_group[lyr], b_route_expert[lyr], jnp.zeros((r_pad,), F32)]).reshape(1, R_LANES)
    wgu = jnp.concatenate([w_exp_gate[lyr], w_exp_up[lyr]], axis=2).astype(BF16)
    wd = w_exp_down[lyr].astype(BF16)
    tri = (jnp.arange(TIE_W)[:, None] <= jnp.arange(TIE_W)[None, :])

    xp = x_prompt[0]
    head = jnp.concatenate([jnp.zeros((PADF, D_MODEL), F32), meta_tokens.astype(F32)], axis=0)
    u, q, k, v, kb, vb, qi, kiw, ga, gb = _in_proj(xp, head, wp, tm=256, head_rows=HEAD_ROWS)
    y_ssm, st = _ssm_prompt(u, bbig_hi, apw, cbig_b, dskip, wglu_b)
    kit = kiw[:, :IDX_DIM].astype(BF16).T
    o_att = _prompt_attention(qi, kiw, q, kit, kb.T, vb, tri.astype(BF16))
    h1, h1b, comb = _mix_ln_route(xp, head, y_ssm, o_att, ga, gb, wso, wao, wo, g1, b1, wrh, wrl, br,
                                  tm=ROW_TILE, head_rows=HEAD_ROWS)
    y_prompt = _moe_ln(h1, h1b, comb, wgu, wd, g2, b2, tm=HEAD_ROWS, head_rows=HEAD_ROWS)

    n_tok = n_b * n_s
    xs = x_sample.reshape(n_tok, D_MODEL)
    us, qs, ks, vs, _, _, qis, kiws, gas, gbs = _in_proj(xs, None, wp, tm=n_tok, head_rows=0)
    u_s = us.reshape(n_b, n_s, SSM_WIDTH).transpose(1, 0, 2)
    ys_s, hre_s, him_s = _ssm_sample(u_s, state_ssm_re[lyr].reshape(n_b, N_STATE),
                                     state_ssm_im[lyr].reshape(n_b, N_STATE),
                                     bbig_hi, bbig_lo, a_pair, cbig_b, dskip, wglu_b)
    ys_s = ys_s.transpose(1, 0, 2).reshape(n_tok, SSM_WIDTH)

    def pad_rows(a):
        return jnp.pad(a, [(0, 0), (0, S_ROWS - n_s)] + [(0, 0)] * (a.ndim - 2))

    qi4 = pad_rows(qis.reshape(n_b, n_s, IDX_HEADS, IDX_DIM)).transpose(0, 2, 1, 3)
    qi_s = qi4.reshape(n_b, IDX_HEADS * S_ROWS, IDX_DIM)
    w4 = kiws[:, IDX_DIM:IDX_DIM + IDX_HEADS] * (IDX_HEADS ** -0.5 * IDX_DIM ** -0.5)
    w_s = pad_rows(w4.reshape(n_b, n_s, IDX_HEADS)).transpose(0, 2, 1).reshape(n_b, IDX_HEADS * S_ROWS, 1)
    ki_new = kiws[:, :IDX_DIM].reshape(n_b, n_s, IDX_DIM)
    kinew = jnp.pad(ki_new, [(0, 0), (0, PAGE - n_s), (0, 0)]).astype(BF16)
    n_phys = cache_k.shape[1]
    kidx_t = cache_kidx[lyr].transpose(0, 2, 1)
    ck_t = cache_k[lyr].transpose(0, 2, 3, 1).reshape(n_phys, ATT_WIDTH, PAGE)
    cv_t = cache_v[lyr].transpose(0, 2, 3, 1).reshape(n_phys, ATT_WIDTH, PAGE)
    scores, par = _sample_scores(page_table, qi_s, w_s, kinew, kidx_t)
    o_s = _sample_attention(page_table, pad_rows(qs.astype(F32).reshape(n_b, n_s, ATT_WIDTH)), scores, par,
                            pad_rows(ks.reshape(n_b, n_s, ATT_WIDTH)), pad_rows(vs.reshape(n_b, n_s, ATT_WIDTH)),
                            tri[:S_TIE, :S_TIE].astype(F32), ck_t, cv_t)
    o_s = o_s[:, :n_s].reshape(n_tok, ATT_WIDTH).astype(BF16)
    h1s, h1bs, combs = _mix_ln_route(xs, None, ys_s, o_s, gas, gbs, wso, wao, wo, g1, b1, wrh, wrl, br,
                                     tm=n_tok, head_rows=0)
    y_sample = _moe_ln(h1s, h1bs, combs, wgu, wd, g2, b2, tm=n_tok, head_rows=0)

    def heads(a, lead):
        return a.reshape((depth,) + lead + (N_HEADS, HEAD_DIM))

    return (y_prompt.reshape(1, seq, D_MODEL),
            y_sample.reshape(n_b, n_s, D_MODEL),
            heads(k[PADF:], (1, t_len)), heads(v[PADF:], (1, t_len)),
            kiw[PADF:, :IDX_DIM].reshape(depth, 1, t_len, IDX_DIM),
            st[0].reshape(depth, 1, SSM_GROUPS, SSM_STATE), st[1].reshape(depth, 1, SSM_GROUPS, SSM_STATE),
            heads(ks, (n_b, n_s)), heads(vs, (n_b, n_s)),
            kiws[:, :IDX_DIM].reshape(depth, n_b, n_s, IDX_DIM),
            hre_s.reshape(depth, n_b, SSM_GROUPS, SSM_STATE), him_s.reshape(depth, n_b, SSM_GROUPS, SSM_STATE))
```

```python
import functools
import math

import jax
import jax.numpy as jnp
from jax import lax
from jax.experimental import pallas as pl
from jax.experimental.pallas import tpu as pltpu

F32 = jnp.float32
BF16 = jnp.bfloat16
I32 = jnp.int32

D_MODEL = 1024
N_META = 16
SSM_WIDTH = 512
SSM_GROUP_CH = 16
SSM_GROUPS = 32
SSM_STATE = 64
N_STATE = SSM_GROUPS * SSM_STATE
N_HEADS = 8
HEAD_DIM = 64
ATT_WIDTH = N_HEADS * HEAD_DIM
IDX_HEADS = 4
IDX_DIM = 64
TOPK = 256
PAGE = 128
N_EGROUPS = 4
EXPERTS_PER_GROUP = 4
N_EXPERTS = 16
D_EXPERT = 256
DN_ALPHA = 2.0 ** 0.25
LN_EPS = 1e-5
NEG_INF = float("-inf")

C_U, C_Q, C_K, C_V, C_QI, C_KIW, C_GA, C_GB, C_END = 0, 512, 1024, 1536, 2048, 2304, 2432, 3456, 4480
IN_SPLITS = (512, 512, 512, 512, 256, 64, 4, 1024, 1024)

ROW_TILE = 512
HEAD_ROWS = 1024
PADF = HEAD_ROWS - N_META
TQ = 256
TK = 1024
LANES = 128
HEAD_KEY0 = HEAD_ROWS - LANES
TIE_W = 256
M_INIT = -1e30
VMEM_LIMIT = 56 * 1024 * 1024

KEY_NEG_INF = -2139095041
KEY_POS_INF = 2139095040


def _cparams(n_axes, vmem=None):
    return pltpu.CompilerParams(dimension_semantics=("arbitrary",) * n_axes, vmem_limit_bytes=vmem)


def _const_spec(shape):
    nd = len(shape)
    return pl.BlockSpec(shape, lambda *_: (0,) * nd)


def _resident_spec(shape):
    nd = len(shape)
    return pl.BlockSpec(shape, lambda *_: (0,) * nd, pipeline_mode=pl.Buffered(1))


def _sigmoid(x):
    return 1.0 / (1.0 + jnp.exp(-x))


def _gelu_tanh(x):
    return 0.5 * x * (1.0 + jnp.tanh(math.sqrt(2.0 / math.pi) * (x + 0.044715 * (x * x * x))))


def _layernorm(z, g, b):
    mu = jnp.mean(z, axis=-1, keepdims=True)
    zc = z - mu
    var = jnp.mean(zc * zc, axis=-1, keepdims=True)
    return zc * lax.rsqrt(var + LN_EPS) * g + b


def _inproj_body(x_ref, h0_ref, w_ref, u_ref, q_ref, k_ref, v_ref, kb_ref, vb_ref, qi_ref, kiw_ref,
                 ga_ref, gb_ref, *, n_head_tiles):
    if n_head_tiles:
        h = jnp.where(pl.program_id(0) < n_head_tiles, h0_ref[...], x_ref[...])
    else:
        h = x_ref[...]
    hb = h.astype(BF16)

    def seg(a, b):
        return jnp.dot(hb, w_ref[:, a:b], preferred_element_type=F32)

    u_ref[...] = seg(C_U, C_Q)
    q_ref[...] = (seg(C_Q, C_K) * HEAD_DIM ** -0.5).astype(BF16)
    k = seg(C_K, C_V)
    k_ref[...] = k
    kb_ref[...] = k.astype(BF16)
    v = seg(C_V, C_QI)
    v_ref[...] = v
    vb_ref[...] = v.astype(BF16)
    qi_ref[...] = seg(C_QI, C_KIW).astype(BF16)
    kiw_ref[...] = seg(C_KIW, C_GA)
    ga_ref[...] = seg(C_GA, C_GB)
    gb_ref[...] = seg(C_GB, C_END)


def _in_proj(x, h0, wp, *, tm, head_rows):
    n_head_tiles = head_rows // tm
    n_rows = head_rows + x.shape[0]
    grid = (n_rows // tm,)
    if n_head_tiles:
        x_spec = pl.BlockSpec((tm, D_MODEL), lambda i: (jnp.maximum(i - n_head_tiles, 0), 0))
        h_spec = pl.BlockSpec((tm, D_MODEL), lambda i: (jnp.minimum(i, n_head_tiles - 1), 0))
    else:
        h0 = x
        x_spec = pl.BlockSpec((tm, D_MODEL), lambda i: (i, 0))
        h_spec = pl.BlockSpec((tm, D_MODEL), lambda i: (0, 0))

    def row(width):
        return pl.BlockSpec((tm, width), lambda i: (i, 0))

    widths = (512, 512, 512, 512, 512, 512, 256, 128, 1024, 1024)
    dtypes = (F32, BF16, F32, F32, BF16, BF16, BF16, F32, F32, F32)
    return pl.pallas_call(
        functools.partial(_inproj_body, n_head_tiles=n_head_tiles),
        grid=grid,
        in_specs=[x_spec, h_spec, _resident_spec((D_MODEL, C_END))],
        out_specs=[row(w) for w in widths],
        out_shape=[jax.ShapeDtypeStruct((n_rows, w), d) for w, d in zip(widths, dtypes)],
        compiler_params=_cparams(1, VMEM_LIMIT),
        name="in_proj",
    )(x, h0, wp)


SSM_CHUNK = 256


SSM_C_TILE = 128
SSM_C_DEPTH = (SSM_C_TILE // SSM_GROUP_CH) * SSM_STATE
SSM_B_TILE = 256
SSM_B_DEPTH = (SSM_B_TILE // SSM_STATE) * SSM_GROUP_CH


def _ssm_out(h_cat, u, cbig_ref, d_ref, wglu_ref):
    hb = h_cat.astype(BF16)
    tiles = []
    for t in range(SSM_WIDTH // SSM_C_TILE):
        cols = slice(SSM_C_TILE * t, SSM_C_TILE * (t + 1))
        re = slice(SSM_C_DEPTH * t, SSM_C_DEPTH * (t + 1))
        im = slice(N_STATE + SSM_C_DEPTH * t, N_STATE + SSM_C_DEPTH * (t + 1))
        tiles.append(jnp.dot(hb[:, re], cbig_ref[re, cols], preferred_element_type=F32)
                     + jnp.dot(hb[:, im], cbig_ref[im, cols], preferred_element_type=F32))
    y_lin = jnp.concatenate(tiles, axis=1)
    y = _gelu_tanh(y_lin + d_ref[...] * u)
    z = jnp.dot(y.astype(BF16), wglu_ref[...], preferred_element_type=F32)
    return y * _sigmoid(z)


def _ssm_prompt_body(u_ref, bbig_ref, apw_ref, cbig_ref, d_ref, wglu_ref, y_ref, st_ref,
                     bu_ref, hs_ref, hc_ref):
    @pl.when(pl.program_id(0) == 0)
    def _():
        hc_ref[...] = jnp.zeros_like(hc_ref)

    u = u_ref[...]
    ub = u.astype(BF16)
    for j in range(2 * N_STATE // SSM_B_TILE):
        jj = j % (N_STATE // SSM_B_TILE)
        cols = slice(SSM_B_TILE * j, SSM_B_TILE * (j + 1))
        rows = slice(SSM_B_DEPTH * jj, SSM_B_DEPTH * (jj + 1))
        bu_ref[:, cols] = jnp.dot(ub[:, rows], bbig_ref[rows, cols], preferred_element_type=F32)

    def step(r, carry):
        h_re, h_im = carry
        r8 = pl.multiple_of(r * 8, 8)
        x_re = bu_ref[pl.ds(r8, 8), 0:N_STATE]
        x_im = bu_ref[pl.ds(r8, 8), N_STATE:2 * N_STATE]
        for lvl, sh in enumerate((1, 2, 4)):
            a_re = apw_ref[2 * lvl]
            a_im = apw_ref[2 * lvl + 1]
            s_re = pltpu.roll(x_re, sh, 0)
            s_im = pltpu.roll(x_im, sh, 0)
            x_re, x_im = (x_re + (a_re * s_re - a_im * s_im), x_im + (a_re * s_im + a_im * s_re))
        c_re = apw_ref[6]
        c_im = apw_ref[7]
        x_re, x_im = (x_re + (c_re * h_re - c_im * h_im), x_im + (c_re * h_im + c_im * h_re))
        hs_ref[pl.ds(r8, 8), 0:N_STATE] = x_re
        hs_ref[pl.ds(r8, 8), N_STATE:2 * N_STATE] = x_im
        return x_re[7:8, :], x_im[7:8, :]

    h_re, h_im = lax.fori_loop(0, SSM_CHUNK // 8, step, (hc_ref[0:1, :], hc_ref[1:2, :]))
    hc_ref[0:1, :] = h_re
    hc_ref[1:2, :] = h_im
    st_ref[...] = hc_ref[...]
    y_ref[...] = _ssm_out(hs_ref[...], u, cbig_ref, d_ref, wglu_ref).astype(BF16)


def _ssm_prompt(u, bbig, apw, cbig, dskip, wglu):
    n_rows = u.shape[0]
    return pl.pallas_call(
        _ssm_prompt_body,
        grid=(n_rows // SSM_CHUNK,),
        in_specs=[pl.BlockSpec((SSM_CHUNK, SSM_WIDTH), lambda i: (i, 0)),
                  _const_spec((SSM_WIDTH, 2 * N_STATE)), _const_spec((8, 8, N_STATE)),
                  _const_spec((2 * N_STATE, SSM_WIDTH)), _const_spec((1, SSM_WIDTH)),
                  _const_spec((SSM_WIDTH, SSM_WIDTH))],
        out_specs=[pl.BlockSpec((SSM_CHUNK, SSM_WIDTH), lambda i: (i, 0)), _const_spec((2, N_STATE))],
        out_shape=[jax.ShapeDtypeStruct((n_rows, SSM_WIDTH), BF16),
                   jax.ShapeDtypeStruct((2, N_STATE), F32)],
        scratch_shapes=[pltpu.VMEM((SSM_CHUNK, 2 * N_STATE), F32), pltpu.VMEM((SSM_CHUNK, 2 * N_STATE), F32),
                        pltpu.VMEM((2, N_STATE), F32)],
        compiler_params=_cparams(1, VMEM_LIMIT),
        name="ssm_prompt",
    )(u, bbig, apw, cbig, dskip, wglu)


def _ssm_sample_body(u_ref, h0re_ref, h0im_ref, bhi_ref, blo_ref, a_ref, cbig_ref, d_ref, wglu_ref,
                     y_ref, hre_ref, him_ref):
    a_re = a_ref[0:1, :]
    a_im = a_ref[1:2, :]
    h_re = h0re_ref[...]
    h_im = h0im_ref[...]
    for s in range(u_ref.shape[0]):
        u = u_ref[s]
        u_hi = u.astype(BF16)
        u_lo = (u - u_hi.astype(F32)).astype(BF16)
        bu = (jnp.dot(u_hi, bhi_ref[...], preferred_element_type=F32)
              + (jnp.dot(u_lo, bhi_ref[...], preferred_element_type=F32)
                 + jnp.dot(u_hi, blo_ref[...], preferred_element_type=F32)))
        h_re, h_im = (a_re * h_re - a_im * h_im + bu[:, 0:N_STATE],
                      a_re * h_im + a_im * h_re + bu[:, N_STATE:2 * N_STATE])
        h_cat = jnp.concatenate([h_re, h_im], axis=1)
        y_ref[s] = _ssm_out(h_cat, u, cbig_ref, d_ref, wglu_ref).astype(BF16)
    hre_ref[...] = h_re
    him_ref[...] = h_im


def _ssm_sample(u_s, h0_re, h0_im, bhi, blo, a_pair, cbig, dskip, wglu):
    n_s, n_b, _ = u_s.shape
    return pl.pallas_call(
        _ssm_sample_body,
        grid=(1,),
        in_specs=[_const_spec(u_s.shape), _const_spec(h0_re.shape), _const_spec(h0_im.shape),
                  _const_spec(bhi.shape), _const_spec(blo.shape), _const_spec(a_pair.shape),
                  _const_spec(cbig.shape), _const_spec(dskip.shape), _const_spec(wglu.shape)],
        out_specs=[_const_spec((n_s, n_b, SSM_WIDTH)), _const_spec((n_b, N_STATE)), _const_spec((n_b, N_STATE))],
        out_shape=[jax.ShapeDtypeStruct((n_s, n_b, SSM_WIDTH), BF16),
                   jax.ShapeDtypeStruct((n_b, N_STATE), F32), jax.ShapeDtypeStruct((n_b, N_STATE), F32)],
        compiler_params=_cparams(1, VMEM_LIMIT),
        name="ssm_sample",
    )(u_s, h0_re, h0_im, bhi, blo, a_pair, cbig, dskip, wglu)


def _key_to_float(key):
    bits = key ^ ((key >> 31) & 0x7FFFFFFF)
    return lax.bitcast_convert_type(bits, F32)


def _float_to_key(x):
    bits = lax.bitcast_convert_type(x, I32)
    return bits ^ ((bits >> 31) & 0x7FFFFFFF)


BISECT_EVERY = 8
SELECT_MAX_PASSES = 32 * BISECT_EVERY + 8


def _kth_largest(count_fn, n_all, lane_max):
    kk = float(TOPK)
    n_grp = lane_max.shape[0] // LANES

    def to_cols(vec):
        cols = [jnp.broadcast_to(vec[g:g + 1], (LANES, LANES)).T for g in range(n_grp)]
        return cols[0] if n_grp == 1 else jnp.concatenate(cols, axis=0)

    def per_row(part, reduce):
        red = [reduce(part[LANES * g:LANES * (g + 1)].T, axis=0, keepdims=True) for g in range(n_grp)]
        return red[0] if n_grp == 1 else jnp.concatenate(red, axis=0)

    def total(th_vec, strict):
        return per_row(count_fn(to_cols(th_vec), strict), jnp.sum)

    row_max = per_row(lane_max, jnp.max)
    zero = jnp.zeros((n_grp, LANES), F32)
    c_ge0 = total(zero, False)
    c_gt0 = total(zero, True)
    pos = c_gt0 > kk
    at0 = (c_ge0 >= kk) & (c_gt0 <= kk)
    lo0 = jnp.where(pos | at0, 0, KEY_NEG_INF).astype(I32)
    hi0 = jnp.where(pos, _float_to_key(row_max) + 1, jnp.where(at0, 0, -1)).astype(I32)
    c_lo0 = jnp.where(pos, c_gt0, jnp.where(at0, c_ge0, n_all))
    c_hi0 = jnp.where(pos, 0.0, jnp.where(at0, c_gt0, c_ge0))
    one = jnp.ones((n_grp, LANES), F32)

    def mid_key(lo, hi):
        return (lo >> 1) + (hi >> 1) + (lo & hi & 1)

    def finished(lo, hi, c_lo):
        return (c_lo == kk) | (mid_key(lo, hi) == lo)

    def cond(st):
        it, lo, hi, c_lo = st[0], st[1], st[2], st[3]
        open_rows = jnp.where(finished(lo, hi, c_lo), 0.0, 1.0)
        return (it < SELECT_MAX_PASSES) & (jnp.max(open_rows) > 0.0)

    def body(st):
        it, lo, hi, c_lo, c_hi, w_lo, w_hi, side = st
        done = finished(lo, hi, c_lo)
        lo_v = _key_to_float(lo)
        hi_v = _key_to_float(hi)
        g_lo = jnp.log(c_lo * (1.0 / (kk - 0.5))) * w_lo
        g_hi = jnp.log(jnp.maximum(c_hi, 0.5) * (1.0 / (kk - 0.5))) * w_hi
        th_i = lo_v + (g_lo / (g_lo - g_hi)) * (hi_v - lo_v)
        bounded = (lo_v > NEG_INF) & (hi_v < float("inf"))
        k_i = jnp.minimum(jnp.maximum(_float_to_key(th_i), lo + 1), hi - 1)
        k_mid = mid_key(lo, hi)
        probe = jnp.where((it & (BISECT_EVERY - 1)) == BISECT_EVERY - 1, k_mid, jnp.where(bounded, k_i, k_mid))
        c = total(_key_to_float(probe), False)
        ge = c >= kk
        mv_lo = (~done) & ge
        mv_hi = (~done) & (~ge)
        w_hi = jnp.where(mv_lo, jnp.where(side > 0.0, w_hi * 0.5, one), jnp.where(mv_hi, one, w_hi))
        w_lo = jnp.where(mv_hi, jnp.where(side < 0.0, w_lo * 0.5, one), jnp.where(mv_lo, one, w_lo))
        side = jnp.where(mv_lo, 1.0, jnp.where(mv_hi, -1.0, side))
        return (it + 1, jnp.where(mv_lo, probe, lo), jnp.where(mv_hi, probe, hi),
                jnp.where(mv_lo, c, c_lo), jnp.where(mv_hi, c, c_hi), w_lo, w_hi, side)

    st = lax.while_loop(cond, lambda s: body(body(s)), (jnp.int32(0), lo0, hi0, c_lo0, c_hi0, one, one, zero))
    lo, c_lo, c_hi = st[1], st[3], st[4]
    v = _key_to_float(lo)
    need = jnp.where(v == NEG_INF, 0.0, jnp.where(c_lo == kk, kk, kk - c_hi))
    return to_cols(v)[:, 0:1], to_cols(need)[:, 0:1]


def _topk_mask(x, v, need, off, tri):
    eq = x == v
    cum = jnp.dot(jnp.where(eq, 1.0, 0.0).astype(tri.dtype), tri, preferred_element_type=F32)
    sel = (x > v) | (eq & ((cum + off) <= need))
    w = x.shape[1]
    return sel, off + cum[:, w - 1:w]


def _pattn_body(qi_ref, kiw_ref, q_ref, kit_ref, kt_ref, v_hbm, tri_ref, o_ref,
                sc_ref, m_ref, l_ref, acc_ref, mx_ref, vhead_ref, vbuf_ref, vsem, *, first_block):
    i = pl.program_id(0)
    row0 = i * TQ

    @pl.when(i < first_block)
    def _():
        o_ref[...] = jnp.zeros_like(o_ref)

    @pl.when(i >= first_block)
    def _():
        jd = row0 // TK

        def head_copy():
            return pltpu.make_async_copy(v_hbm.at[pl.ds(HEAD_KEY0, LANES), :], vhead_ref, vsem.at[2])

        def tile_copy(j):
            return pltpu.make_async_copy(v_hbm.at[pl.ds(pl.multiple_of(j * TK, TK), TK), :],
                                         vbuf_ref.at[j & 1], vsem.at[j & 1])

        head_copy().start()

        @pl.when(jd >= 1)
        def _():
            tile_copy(1).start()
        w = kiw_ref[:, IDX_DIM:IDX_DIM + IDX_HEADS] * (IDX_HEADS ** -0.5 * IDX_DIM ** -0.5)
        qi_h = [qi_ref[:, IDX_DIM * h:IDX_DIM * (h + 1)] for h in range(IDX_HEADS)]
        q_h = [q_ref[:, HEAD_DIM * h:HEAD_DIM * (h + 1)] for h in range(N_HEADS)]

        def score_tile(c0, width, masked, mx):
            kit = kit_ref[:, pl.ds(c0, width)]
            tot = None
            for h in range(IDX_HEADS):
                z = jnp.dot(qi_h[h], kit, preferred_element_type=F32)
                r = jnp.maximum(z, 0.0) * w[:, h:h + 1]
                tot = r if tot is None else tot + r
            if masked:
                s_idx = c0 + lax.broadcasted_iota(I32, (TQ, width), 1)
                t_idx = row0 + lax.broadcasted_iota(I32, (TQ, width), 0)
                tot = jnp.where((s_idx <= t_idx) & (s_idx >= PADF), tot, NEG_INF)
            sc_ref[:, pl.ds(c0, width)] = tot
            for a in range(width // LANES):
                mx = jnp.maximum(mx, tot[:, LANES * a:LANES * (a + 1)])
            return mx

        def main_start(j):
            return pl.multiple_of(j * TK, TK)

        mx = score_tile(HEAD_KEY0, LANES, True, jnp.full((TQ, LANES), NEG_INF, F32))
        mx = lax.fori_loop(1, jd, lambda j, mx: score_tile(main_start(j), TK, False, mx), mx)
        mx_ref[...] = mx

        @pl.when(jd >= 1)
        def _():
            mx_ref[...] = score_tile(main_start(jd), TK, True, mx)


        def count_fn(th, strict):
            groups = []
            for g in range(TQ // LANES):
                rows = pl.ds(LANES * g, LANES)
                th_g = th[LANES * g:LANES * (g + 1)]

                def add(cnt, x, th_g=th_g):
                    return cnt + jnp.where((x > th_g) if strict else (x >= th_g), 1.0, 0.0)

                def count_main(j, cnt, rows=rows, add=add):
                    c0 = main_start(j)
                    for a in range(TK // LANES):
                        cnt = add(cnt, sc_ref[rows, pl.ds(c0 + a * LANES, LANES)])
                    return cnt

                cnt = add(jnp.zeros((LANES, LANES), F32), sc_ref[rows, HEAD_KEY0:HEAD_ROWS])
                groups.append(lax.fori_loop(1, jd + 1, count_main, cnt))
            return jnp.concatenate(groups, axis=0)

        v, need = _kth_largest(count_fn, (LANES + TK * jd).astype(F32), mx_ref[...])

        m_ref[...] = jnp.full_like(m_ref, M_INIT)
        l_ref[...] = jnp.zeros_like(l_ref)
        acc_ref[...] = jnp.zeros_like(acc_ref)
        first_half = lax.broadcasted_iota(I32, (TQ, LANES), 1) < HEAD_DIM

        def attend_tile(c0, width, off, v_cols):
            x = sc_ref[:, pl.ds(c0, width)]
            parts = []
            for a in range(0, width, TIE_W):
                wa = min(TIE_W, width - a)
                sel, off = _topk_mask(x[:, a:a + wa], v, need, off, tri_ref[0:wa, 0:wa])
                parts.append(jnp.where(sel, 0.0, NEG_INF))
            bias = parts[0] if len(parts) == 1 else jnp.concatenate(parts, axis=1)
            n_sl = width // LANES

            def probs(h):
                kt = kt_ref[HEAD_DIM * h:HEAD_DIM * (h + 1), pl.ds(c0, width)]
                s = jnp.dot(q_h[h], kt, preferred_element_type=F32) + bias
                sl = [s[:, LANES * a:LANES * (a + 1)] for a in range(n_sl)]
                pm = sl[0]
                for a in range(1, n_sl):
                    pm = jnp.maximum(pm, sl[a])
                m_old = m_ref[h]
                m_new = jnp.maximum(m_old, jnp.max(pm, axis=1, keepdims=True))
                alpha = jnp.exp(m_old - m_new)
                ps = [jnp.exp(sl[a] - m_new) for a in range(n_sl)]
                psum = ps[0]
                for a in range(1, n_sl):
                    psum = psum + ps[a]
                l_ref[h] = alpha * l_ref[h] + psum
                m_ref[h] = m_new
                return (ps[0] if n_sl == 1 else jnp.concatenate(ps, axis=1)).astype(BF16), alpha

            for j in range(N_HEADS // 2):
                p_e, alpha_e = probs(2 * j)
                p_o, alpha_o = probs(2 * j + 1)
                cols = slice(LANES * j, LANES * (j + 1))
                pv = jnp.dot(jnp.concatenate([p_e, p_o], axis=0), v_cols(cols),
                             preferred_element_type=F32)
                a_pair = jnp.where(first_half, alpha_e, alpha_o)
                pv_pair = jnp.where(first_half, pv[0:TQ], pv[TQ:2 * TQ])
                acc_ref[:, cols] = a_pair * acc_ref[:, cols] + pv_pair
            return off

        head_copy().wait()
        off = attend_tile(HEAD_KEY0, LANES, jnp.zeros((TQ, 1), F32), lambda cols: vhead_ref[:, cols])

        def attend_main(j, off):
            tile_copy(j).wait()

            @pl.when(j + 1 <= jd)
            def _():
                tile_copy(j + 1).start()

            return attend_tile(main_start(j), TK, off, lambda cols: vbuf_ref[j & 1, :, cols])

        lax.fori_loop(1, jd + 1, attend_main, off)

        for j in range(N_HEADS // 2):
            l_e = jnp.sum(l_ref[2 * j], axis=1, keepdims=True)
            l_o = jnp.sum(l_ref[2 * j + 1], axis=1, keepdims=True)
            l_pair = jnp.where(first_half, l_e, l_o)
            inv = 1.0 / jnp.where(l_pair > 0.0, l_pair, 1.0)
            o_ref[:, LANES * j:LANES * (j + 1)] = (acc_ref[:, LANES * j:LANES * (j + 1)] * inv).astype(o_ref.dtype)


def _prompt_attention(qi, kiw, q, kit, kt, vb, tri):
    n_rows = q.shape[0]

    def row(width):
        return pl.BlockSpec((TQ, width), lambda i: (i, 0))

    return pl.pallas_call(
        functools.partial(_pattn_body, first_block=HEAD_KEY0 // TQ),
        grid=(n_rows // TQ,),
        in_specs=[row(IDX_HEADS * IDX_DIM), row(128), row(ATT_WIDTH),
                  _resident_spec(kit.shape), _resident_spec(kt.shape), pl.BlockSpec(memory_space=pl.ANY),
                  _resident_spec(tri.shape)],
        out_specs=row(ATT_WIDTH),
        out_shape=jax.ShapeDtypeStruct((n_rows, ATT_WIDTH), BF16),
        scratch_shapes=[pltpu.VMEM((TQ, n_rows), F32), pltpu.VMEM((N_HEADS, TQ, LANES), F32),
                        pltpu.VMEM((N_HEADS, TQ, LANES), F32), pltpu.VMEM((TQ, ATT_WIDTH), F32),
                        pltpu.VMEM((TQ, LANES), F32), pltpu.VMEM((LANES, ATT_WIDTH), BF16),
                        pltpu.VMEM((2, TK, ATT_WIDTH), BF16), pltpu.SemaphoreType.DMA((3,))],
        compiler_params=_cparams(1, VMEM_LIMIT),
        name="prompt_attention",
    )(qi, kiw, q, kit, kt, vb, tri)


PAGES_PER_STEP = 32
S_CHUNK = PAGES_PER_STEP * PAGE
S_ROWS = 8


def _idx_scores_rows(qi, w, keys_bf16):
    z = lax.dot_general(qi, keys_bf16, (((1,), (1,)), ((), ())), preferred_element_type=F32)
    r = jnp.maximum(z, 0.0) * w
    tot = r[0:S_ROWS]
    for h in range(1, IDX_HEADS):
        tot = tot + r[S_ROWS * h:S_ROWS * (h + 1)]
    return tot


def _sscore_body(pt_ref, qi_ref, w_ref, kinew_ref, *refs, n_chunks, past):
    page_refs = refs[:PAGES_PER_STEP]
    sc_ref, kic_ref = refs[PAGES_PER_STEP:]
    c = pl.program_id(1)
    qi = qi_ref[...]
    w = w_ref[...]
    for r in range(PAGES_PER_STEP):
        kic_ref[:, r * PAGE:(r + 1) * PAGE] = page_refs[r][...].astype(BF16)
    rr = jnp.maximum(jnp.dot(qi, kic_ref[...], preferred_element_type=F32), 0.0) * w
    tot = rr[0:S_ROWS]
    for h in range(1, IDX_HEADS):
        tot = tot + rr[S_ROWS * h:S_ROWS * (h + 1)]
    sc_ref[:, pl.ds(pl.multiple_of(c * S_CHUNK, S_CHUNK), S_CHUNK)] = tot

    @pl.when(c == n_chunks - 1)
    def _():
        tot = _idx_scores_rows(qi, w, kinew_ref[...])
        lane = lax.broadcasted_iota(I32, (S_ROWS, PAGE), 1)
        row = lax.broadcasted_iota(I32, (S_ROWS, PAGE), 0)
        sc_ref[:, past:past + PAGE] = jnp.where(lane <= row, tot, NEG_INF)


def _sselect_body(sc_ref, par_ref, *, sel_tile):
    n_tiles = sc_ref.shape[1] // sel_tile

    def tiles(fn, init):
        def body(j, acc):
            c0 = pl.multiple_of(j * sel_tile, LANES)
            for a in range(sel_tile // LANES):
                acc = fn(acc, sc_ref[:, pl.ds(c0 + a * LANES, LANES)])
            return acc

        return lax.fori_loop(0, n_tiles, body, init)

    def count_fn(th, strict):
        return tiles(lambda cnt, x: cnt + jnp.where((x > th) if strict else (x >= th), 1.0, 0.0),
                     jnp.zeros((LANES, LANES), F32))

    lane_max = tiles(jnp.maximum, jnp.full((LANES, LANES), NEG_INF, F32))
    v, need = _kth_largest(count_fn, float(sc_ref.shape[1]), lane_max)
    lane = lax.broadcasted_iota(I32, (LANES, LANES), 1)
    par_ref[...] = jnp.where(lane == 0, v, jnp.where(lane == 1, need, 0.0))


def _sample_select(scores2d):
    n_rows, width = scores2d.shape
    sel_tile = max(t for t in range(LANES, 6 * LANES, LANES) if width % t == 0)
    pad = -n_rows % LANES
    if pad:
        scores2d = jnp.pad(scores2d, [(0, pad), (0, 0)])
    par = pl.pallas_call(
        functools.partial(_sselect_body, sel_tile=sel_tile),
        grid=((n_rows + pad) // LANES,),
        in_specs=[pl.BlockSpec((LANES, width), lambda i: (i, 0))],
        out_specs=pl.BlockSpec((LANES, LANES), lambda i: (i, 0)),
        out_shape=jax.ShapeDtypeStruct((n_rows + pad, LANES), F32),
        compiler_params=_cparams(1, VMEM_LIMIT),
        name="sample_select",
    )(scores2d)
    return par[:n_rows]


def _sample_scores(page_table, qi_s, w_s, kinew, cache_kidx):
    n_b, n_pages = page_table.shape
    n_chunks = n_pages // PAGES_PER_STEP
    past = n_pages * PAGE
    width = past + PAGE

    def page_spec(r):
        return pl.BlockSpec((None, IDX_DIM, PAGE), lambda b, c, pt: (pt[b, c * PAGES_PER_STEP + r], 0, 0))

    def per_b(shape):
        return pl.BlockSpec((None,) + shape, lambda b, c, pt: (b, 0, 0))

    grid_spec = pltpu.PrefetchScalarGridSpec(
        num_scalar_prefetch=1,
        grid=(n_b, n_chunks),
        in_specs=[per_b((IDX_HEADS * S_ROWS, IDX_DIM)), per_b((IDX_HEADS * S_ROWS, 1)), per_b((PAGE, IDX_DIM))]
                 + [page_spec(r) for r in range(PAGES_PER_STEP)],
        out_specs=per_b((S_ROWS, width)),
        scratch_shapes=[pltpu.VMEM((IDX_DIM, S_CHUNK), BF16)],
    )
    scores = pl.pallas_call(
        functools.partial(_sscore_body, n_chunks=n_chunks, past=past),
        grid_spec=grid_spec,
        out_shape=jax.ShapeDtypeStruct((n_b, S_ROWS, width), F32),
        compiler_params=_cparams(2, VMEM_LIMIT),
        name="sample_scores",
    )(page_table, qi_s, w_s, kinew, *([cache_kidx] * PAGES_PER_STEP))
    par = _sample_select(scores.reshape(n_b * S_ROWS, width)).reshape(n_b, S_ROWS, LANES)
    return scores, par


S_TIE = 256


def _sattn_body(pt_ref, q_ref, sc_ref, scn_ref, par_ref, knew_ref, vnew_ref, tri_ref, *refs, n_chunks):
    k_refs = refs[:PAGES_PER_STEP]
    v_refs = refs[PAGES_PER_STEP:2 * PAGES_PER_STEP]
    o_ref, qbd_ref, kc_ref, vc_ref, m_ref, l_ref, acc_ref, off_ref = refs[2 * PAGES_PER_STEP:]
    c = pl.program_id(1)
    n_q = 4
    n_rows = n_q * N_HEADS
    head_of_col = lax.broadcasted_iota(I32, (N_HEADS, ATT_WIDTH), 1) // HEAD_DIM
    own_head = head_of_col == lax.broadcasted_iota(I32, (N_HEADS, ATT_WIDTH), 0)

    @pl.when(c == 0)
    def _():
        for s in range(n_q):
            qs = jnp.broadcast_to(q_ref[s:s + 1, :], (N_HEADS, ATT_WIDTH))
            qbd_ref[N_HEADS * s:N_HEADS * (s + 1), :] = jnp.where(own_head, qs, 0.0)
        m_ref[...] = jnp.full_like(m_ref, M_INIT)
        l_ref[...] = jnp.zeros_like(l_ref)
        acc_ref[...] = jnp.zeros_like(acc_ref)
        off_ref[...] = jnp.zeros_like(off_ref)

    v_thr = par_ref[:, 0:1]
    need = par_ref[:, 1:2]
    qbd = qbd_ref[...].astype(BF16)

    nt_dims = (((1,), (1,)), ((), ()))

    def attend(s_raw, pv_fn, x):
        width = x.shape[1]
        off = off_ref[...]
        parts = []
        for a in range(0, width, S_TIE):
            wa = min(S_TIE, width - a)
            sel, off = _topk_mask(x[:, a:a + wa], v_thr, need, off, tri_ref[0:wa, 0:wa])
            parts.append(jnp.where(sel, 0.0, NEG_INF))
        off_ref[...] = off
        bias8 = parts[0] if len(parts) == 1 else jnp.concatenate(parts, axis=1)
        bias = jnp.concatenate([jnp.broadcast_to(bias8[s:s + 1, :], (N_HEADS, width)) for s in range(n_q)], axis=0)
        s_mat = s_raw + bias
        m_old = m_ref[...]
        m_new = jnp.maximum(m_old, jnp.max(s_mat, axis=1, keepdims=True))
        p = jnp.exp(s_mat - m_new)
        alpha = jnp.exp(m_old - m_new)
        l_ref[...] = alpha * l_ref[...] + jnp.sum(p, axis=1, keepdims=True)
        acc_ref[...] = alpha * acc_ref[...] + pv_fn(p.astype(BF16))
        m_ref[...] = m_new

    for r in range(PAGES_PER_STEP):
        kc_ref[:, r * PAGE:(r + 1) * PAGE] = k_refs[r][...].astype(BF16)
        vc_ref[:, r * PAGE:(r + 1) * PAGE] = v_refs[r][...].astype(BF16)
    attend(jnp.dot(qbd, kc_ref[...], preferred_element_type=F32),
           lambda p: lax.dot_general(p, vc_ref[...], nt_dims, preferred_element_type=F32), sc_ref[...])

    @pl.when(c == n_chunks - 1)
    def _():
        pad = jnp.zeros((PAGE - S_ROWS, ATT_WIDTH), F32)
        k_new = jnp.concatenate([knew_ref[...], pad], axis=0).astype(BF16)
        v_new = jnp.concatenate([vnew_ref[...], pad], axis=0).astype(BF16)
        attend(lax.dot_general(qbd, k_new, nt_dims, preferred_element_type=F32),
               lambda p: jnp.dot(p, v_new, preferred_element_type=F32), scn_ref[...])
        l = l_ref[...]
        o_all = acc_ref[...] * (1.0 / jnp.where(l > 0.0, l, 1.0))
        row = lax.broadcasted_iota(I32, (S_ROWS, ATT_WIDTH), 0)
        out = jnp.zeros((S_ROWS, ATT_WIDTH), F32)
        for s in range(n_q):
            o_s = jnp.sum(jnp.where(own_head, o_all[N_HEADS * s:N_HEADS * (s + 1), :], 0.0), axis=0, keepdims=True)
            out = jnp.where(row == s, jnp.broadcast_to(o_s, (S_ROWS, ATT_WIDTH)), out)
        o_ref[...] = out


def _sample_attention(page_table, q_s, scores, par, knew, vnew, tri, cache_k, cache_v):
    n_b, n_pages = page_table.shape
    n_chunks = n_pages // PAGES_PER_STEP
    n_rows = 4 * N_HEADS

    def page_spec(r):
        return pl.BlockSpec((None, ATT_WIDTH, PAGE), lambda b, c, pt: (pt[b, c * PAGES_PER_STEP + r], 0, 0))

    def per_b(shape):
        return pl.BlockSpec((None,) + shape, lambda b, c, pt: (b, 0, 0))

    grid_spec = pltpu.PrefetchScalarGridSpec(
        num_scalar_prefetch=1,
        grid=(n_b, n_chunks),
        in_specs=[per_b((S_ROWS, ATT_WIDTH)),
                  pl.BlockSpec((None, S_ROWS, S_CHUNK), lambda b, c, pt: (b, 0, c)),
                  pl.BlockSpec((None, S_ROWS, PAGE), lambda b, c, pt: (b, 0, n_pages)),
                  per_b((S_ROWS, PAGE)), per_b((S_ROWS, ATT_WIDTH)), per_b((S_ROWS, ATT_WIDTH)),
                  pl.BlockSpec((S_TIE, S_TIE), lambda b, c, pt: (0, 0))]
                 + [page_spec(r) for r in range(PAGES_PER_STEP)] * 2,
        out_specs=per_b((S_ROWS, ATT_WIDTH)),
        scratch_shapes=[pltpu.VMEM((n_rows, ATT_WIDTH), F32),
                        pltpu.VMEM((ATT_WIDTH, S_CHUNK), BF16), pltpu.VMEM((ATT_WIDTH, S_CHUNK), BF16),
                        pltpu.VMEM((n_rows, 1), F32), pltpu.VMEM((n_rows, 1), F32),
                        pltpu.VMEM((n_rows, ATT_WIDTH), F32), pltpu.VMEM((S_ROWS, 1), F32)],
    )
    return pl.pallas_call(
        functools.partial(_sattn_body, n_chunks=n_chunks),
        grid_spec=grid_spec,
        out_shape=jax.ShapeDtypeStruct((n_b, S_ROWS, ATT_WIDTH), F32),
        compiler_params=_cparams(2, VMEM_LIMIT),
        name="sample_attention",
    )(page_table, q_s, scores, scores, par, knew, vnew, tri,
      *([cache_k] * PAGES_PER_STEP), *([cache_v] * PAGES_PER_STEP))


R_LANES = 128
R_E0 = N_EGROUPS


def _route(logits):
    lane = lax.broadcasted_iota(I32, logits.shape, 1)
    lane_f = lane.astype(F32)
    big = float(R_LANES)
    is_g = lane < N_EGROUPS
    gl = jnp.where(is_g, logits, NEG_INF)
    g_max = jnp.max(gl, axis=1, keepdims=True)
    g_sel = jnp.min(jnp.where(gl == g_max, lane_f, big), axis=1, keepdims=True)
    p_g = 1.0 / jnp.sum(jnp.exp(gl - g_max), axis=1, keepdims=True)
    member = (lane >= R_E0) & (lane < R_E0 + N_EXPERTS) & (((lane - R_E0) >> 2).astype(F32) == g_sel)
    el = jnp.where(member, logits, NEG_INF)
    e_max = jnp.max(el, axis=1, keepdims=True)
    pe = jnp.exp(el - e_max)
    pe = pe / jnp.sum(pe, axis=1, keepdims=True)
    p1 = jnp.max(pe, axis=1, keepdims=True)
    i1 = jnp.min(jnp.where(member & (pe == p1), lane_f, big), axis=1, keepdims=True)
    rest = member & (lane_f != i1)
    p2 = jnp.max(jnp.where(rest, pe, -1.0), axis=1, keepdims=True)
    i2 = jnp.min(jnp.where(rest & (pe == p2), lane_f, big), axis=1, keepdims=True)
    tot = p1 + p2
    return jnp.where(lane_f == i1, p1 / tot * p_g, 0.0) + jnp.where(lane_f == i2, p2 / tot * p_g, 0.0)


def _mix_body(x_ref, h0_ref, ys_ref, oa_ref, ga_ref, gb_ref, wso_ref, wao_ref, wo_ref, g1_ref, b1_ref,
              wrh_ref, wrl_ref, br_ref, h1_ref, h1b_ref, comb_ref, *, n_head_tiles):
    if n_head_tiles:
        h = jnp.where(pl.program_id(0) < n_head_tiles, h0_ref[...], x_ref[...])
    else:
        h = x_ref[...]
    a = jnp.dot(ys_ref[...], wso_ref[...], preferred_element_type=F32)
    b = jnp.dot(oa_ref[...], wao_ref[...], preferred_element_type=F32)
    m = _sigmoid(ga_ref[...]) * a + _sigmoid(gb_ref[...]) * b
    mix = jnp.dot(m.astype(BF16), wo_ref[...], preferred_element_type=F32)
    h1 = _layernorm(DN_ALPHA * h + mix, g1_ref[...], b1_ref[...])
    h1_ref[...] = h1
    h_hi = h1.astype(BF16)
    h1b_ref[...] = h_hi
    h_lo = (h1 - h_hi.astype(F32)).astype(BF16)
    logits = (jnp.dot(h_hi, wrh_ref[...], preferred_element_type=F32)
              + (jnp.dot(h_lo, wrh_ref[...], preferred_element_type=F32)
                 + jnp.dot(h_hi, wrl_ref[...], preferred_element_type=F32))) + br_ref[...]
    comb_ref[...] = _route(logits)


def _mix_ln_route(x, h0, ys, oa, ga, gb, wso, wao, wo, g1, b1, wrh, wrl, br, *, tm, head_rows):
    n_head_tiles = head_rows // tm
    n_rows = head_rows + x.shape[0]
    if n_head_tiles:
        x_spec = pl.BlockSpec((tm, D_MODEL), lambda i: (jnp.maximum(i - n_head_tiles, 0), 0))
        h_spec = pl.BlockSpec((tm, D_MODEL), lambda i: (jnp.minimum(i, n_head_tiles - 1), 0))
    else:
        h0 = x
        x_spec = pl.BlockSpec((tm, D_MODEL), lambda i: (i, 0))
        h_spec = pl.BlockSpec((tm, D_MODEL), lambda i: (0, 0))

    def row(width):
        return pl.BlockSpec((tm, width), lambda i: (i, 0))

    consts = (wso, wao, wo, g1, b1, wrh, wrl, br)
    return pl.pallas_call(
        functools.partial(_mix_body, n_head_tiles=n_head_tiles),
        grid=(n_rows // tm,),
        in_specs=[x_spec, h_spec, row(SSM_WIDTH), row(ATT_WIDTH), row(D_MODEL), row(D_MODEL)]
                 + [_const_spec(c.shape) for c in consts],
        out_specs=[row(D_MODEL), row(D_MODEL), row(R_LANES)],
        out_shape=[jax.ShapeDtypeStruct((n_rows, D_MODEL), F32), jax.ShapeDtypeStruct((n_rows, D_MODEL), BF16),
                   jax.ShapeDtypeStruct((n_rows, R_LANES), F32)],
        compiler_params=_cparams(1, VMEM_LIMIT),
        name="mix_ln_route",
    )(x, h0, ys, oa, ga, gb, *consts)


def _moe_body(h1_ref, h1b_ref, comb_ref, wgu_ref, wd_ref, g2_ref, b2_ref, y_ref, acc_ref):
    e = pl.program_id(1)

    @pl.when(e == 0)
    def _():
        acc_ref[...] = jnp.zeros_like(acc_ref)

    gu = jnp.dot(h1b_ref[...], wgu_ref[...], preferred_element_type=F32)
    gate = gu[:, 0:D_EXPERT]
    up = gu[:, D_EXPERT:2 * D_EXPERT]
    act = gate * _sigmoid(gate) * up
    comb = comb_ref[...]
    lane = lax.broadcasted_iota(I32, comb.shape, 1)
    c_e = jnp.sum(jnp.where(lane == e + R_E0, comb, 0.0), axis=1, keepdims=True)
    acc_ref[...] += jnp.dot((act * c_e).astype(BF16), wd_ref[...], preferred_element_type=F32)

    @pl.when(e == N_EXPERTS - 1)
    def _():
        y_ref[...] = _layernorm(DN_ALPHA * h1_ref[...] + acc_ref[...], g2_ref[...], b2_ref[...])


def _moe_ln(h1, h1b, comb, wgu, wd, g2, b2, *, tm, head_rows):
    n_head_tiles = head_rows // tm
    n_rows = h1.shape[0]
    out_rows = n_rows - head_rows

    def row(width):
        return pl.BlockSpec((tm, width), lambda i, e: (i, 0))

    return pl.pallas_call(
        _moe_body,
        grid=(n_rows // tm, N_EXPERTS),
        in_specs=[row(D_MODEL), row(D_MODEL), row(R_LANES),
                  pl.BlockSpec((None, D_MODEL, 2 * D_EXPERT), lambda i, e: (e, 0, 0)),
                  pl.BlockSpec((None, D_EXPERT, D_MODEL), lambda i, e: (e, 0, 0)),
                  pl.BlockSpec((1, D_MODEL), lambda i, e: (0, 0)), pl.BlockSpec((1, D_MODEL), lambda i, e: (0, 0))],
        out_specs=pl.BlockSpec((tm, D_MODEL), lambda i, e: (jnp.maximum(i - n_head_tiles, 0), 0)),
        out_shape=jax.ShapeDtypeStruct((out_rows, D_MODEL), F32),
        scratch_shapes=[pltpu.VMEM((tm, D_MODEL), F32)],
        compiler_params=_cparams(2, VMEM_LIMIT),
        name="moe_ln",
    )(h1, h1b, comb, wgu, wd, g2, b2)


def _ssm_tables(a_re, a_im, log_dt, b_re, b_im, c_re, c_im):
    dt = jnp.exp(log_dt)[:, None]
    mag = jnp.exp(a_re * dt)
    ab_re = mag * jnp.cos(a_im * dt)
    ab_im = mag * jnp.sin(a_im * dt)
    den = a_re * a_re + a_im * a_im
    nr = ab_re - 1.0
    f_re = (nr * a_re + ab_im * a_im) / den
    f_im = (ab_im * a_re - nr * a_im) / den
    bb_re = f_re[..., None] * b_re - f_im[..., None] * b_im
    bb_im = f_re[..., None] * b_im + f_im[..., None] * b_re
    eye = jnp.eye(SSM_GROUPS, dtype=F32)

    def in_mat(bb):
        return jnp.einsum("gpm,gh->gmhp", bb, eye).reshape(SSM_WIDTH, N_STATE)

    def out_mat(cc):
        return jnp.einsum("gmp,gh->gphm", cc, eye).reshape(N_STATE, SSM_WIDTH)

    bbig = jnp.concatenate([in_mat(bb_re), in_mat(bb_im)], axis=1)
    cbig = jnp.concatenate([out_mat(c_re), -out_mat(c_im)], axis=0)
    ar = ab_re.reshape(1, N_STATE)
    ai = ab_im.reshape(1, N_STATE)

    def cmul(x, y):
        return x[0] * y[0] - x[1] * y[1], x[0] * y[1] + x[1] * y[0]

    a1 = (ar, ai)
    pows = [a1]
    for _ in range(7):
        pows.append(cmul(pows[-1], a1))
    row = jnp.arange(8)[:, None]
    tabs = []
    for sh, pw in ((1, pows[0]), (2, pows[1]), (4, pows[3])):
        keep = (row >= sh).astype(F32)
        tabs += [keep * pw[0], keep * pw[1]]
    tabs += [jnp.concatenate([p[0] for p in pows], axis=0), jnp.concatenate([p[1] for p in pows], axis=0)]
    apw = jnp.stack(tabs)
    a_pair = jnp.concatenate([ar, ai], axis=0)
    return bbig, cbig, apw, a_pair


def _split_bf16(x):
    hi = x.astype(BF16)
    return hi, (x - hi.astype(F32)).astype(BF16)


def kernel(x_prompt, x_sample, cache_k, cache_v, cache_kidx, state_ssm_re, state_ssm_im, page_table,
           meta_tokens, w_in, ssm_a_re, ssm_a_im, ssm_log_dt, ssm_b_re, ssm_b_im, ssm_c_re, ssm_c_im,
           ssm_d, w_glu, w_ssm_out, w_att_out, w_o, ln1_g, ln1_b, w_route_group, b_route_group,
           w_route_expert, b_route_expert, w_exp_gate, w_exp_up, w_exp_down, ln2_g, ln2_b):
    depth = w_in.shape[0]
    assert depth == 1 and x_prompt.shape[0] == 1
    n_b, n_s, _ = x_sample.shape
    seq = x_prompt.shape[1]
    t_len = seq + N_META
    lyr = 0

    cuts = [0]
    for c in IN_SPLITS:
        cuts.append(cuts[-1] + c)
    w = w_in[lyr]
    wp = jnp.concatenate([w[:, :cuts[7]], jnp.zeros((D_MODEL, C_GA - C_KIW - IDX_DIM - IDX_HEADS), F32),
                          w[:, cuts[7]:]], axis=1).astype(BF16)
    bbig, cbig, apw, a_pair = _ssm_tables(ssm_a_re[lyr], ssm_a_im[lyr], ssm_log_dt[lyr], ssm_b_re[lyr],
                                          ssm_b_im[lyr], ssm_c_re[lyr], ssm_c_im[lyr])
    bbig_hi, bbig_lo = _split_bf16(bbig)
    cbig_b = cbig.astype(BF16)
    dskip = ssm_d[lyr].reshape(1, SSM_WIDTH)
    wglu_b = w_glu[lyr].astype(BF16)
    wso = w_ssm_out[lyr].astype(BF16)
    wao = w_att_out[lyr].astype(BF16)
    wo = w_o[lyr].astype(BF16)
    g1 = ln1_g[lyr].reshape(1, D_MODEL)
    b1 = ln1_b[lyr].reshape(1, D_MODEL)
    g2 = ln2_g[lyr].reshape(1, D_MODEL)
    b2 = ln2_b[lyr].reshape(1, D_MODEL)
    r_pad = R_LANES - N_EGROUPS - N_EXPERTS
    wr = jnp.concatenate([w_route_group[lyr], w_route_expert[lyr], jnp.zeros((D_MODEL, r_pad), F32)], axis=1)
    wrh, wrl = _split_bf16(wr)
    br = jnp.concatenate([b_route_group[lyr], b_route_expert[lyr], jnp.zeros((r_pad,), F32)]).reshape(1, R_LANES)
    wgu = jnp.concatenate([w_exp_gate[lyr], w_exp_up[lyr]], axis=2).astype(BF16)
    wd = w_exp_down[lyr].astype(BF16)
    tri = (jnp.arange(TIE_W)[:, None] <= jnp.arange(TIE_W)[None, :])

    xp = x_prompt[0]
    head = jnp.concatenate([jnp.zeros((PADF, D_MODEL), F32), meta_tokens.astype(F32)], axis=0)
    u, q, k, v, kb, vb, qi, kiw, ga, gb = _in_proj(xp, head, wp, tm=256, head_rows=HEAD_ROWS)
    y_ssm, st = _ssm_prompt(u, bbig_hi, apw, cbig_b, dskip, wglu_b)
    kit = kiw[:, :IDX_DIM].astype(BF16).T
    o_att = _prompt_attention(qi, kiw, q, kit, kb.T, vb, tri.astype(BF16))
    h1, h1b, comb = _mix_ln_route(xp, head, y_ssm, o_att, ga, gb, wso, wao, wo, g1, b1, wrh, wrl, br,
                                  tm=ROW_TILE, head_rows=HEAD_ROWS)
    y_prompt = _moe_ln(h1, h1b, comb, wgu, wd, g2, b2, tm=HEAD_ROWS, head_rows=HEAD_ROWS)

    n_tok = n_b * n_s
    xs = x_sample.reshape(n_tok, D_MODEL)
    us, qs, ks, vs, _, _, qis, kiws, gas, gbs = _in_proj(xs, None, wp, tm=n_tok, head_rows=0)
    u_s = us.reshape(n_b, n_s, SSM_WIDTH).transpose(1, 0, 2)
    ys_s, hre_s, him_s = _ssm_sample(u_s, state_ssm_re[lyr].reshape(n_b, N_STATE),
                                     state_ssm_im[lyr].reshape(n_b, N_STATE),
                                     bbig_hi, bbig_lo, a_pair, cbig_b, dskip, wglu_b)
    ys_s = ys_s.transpose(1, 0, 2).reshape(n_tok, SSM_WIDTH)

    def pad_rows(a):
        return jnp.pad(a, [(0, 0), (0, S_ROWS - n_s)] + [(0, 0)] * (a.ndim - 2))

    qi4 = pad_rows(qis.reshape(n_b, n_s, IDX_HEADS, IDX_DIM)).transpose(0, 2, 1, 3)
    qi_s = qi4.reshape(n_b, IDX_HEADS * S_ROWS, IDX_DIM)
    w4 = kiws[:, IDX_DIM:IDX_DIM + IDX_HEADS] * (IDX_HEADS ** -0.5 * IDX_DIM ** -0.5)
    w_s = pad_rows(w4.reshape(n_b, n_s, IDX_HEADS)).transpose(0, 2, 1).reshape(n_b, IDX_HEADS * S_ROWS, 1)
    ki_new = kiws[:, :IDX_DIM].reshape(n_b, n_s, IDX_DIM)
    kinew = jnp.pad(ki_new, [(0, 0), (0, PAGE - n_s), (0, 0)]).astype(BF16)
    n_phys = cache_k.shape[1]
    kidx_t = cache_kidx[lyr].transpose(0, 2, 1)
    ck_t = cache_k[lyr].transpose(0, 2, 3, 1).reshape(n_phys, ATT_WIDTH, PAGE)
    cv_t = cache_v[lyr].transpose(0, 2, 3, 1).reshape(n_phys, ATT_WIDTH, PAGE)
    scores, par = _sample_scores(page_table, qi_s, w_s, kinew, kidx_t)
    o_s = _sample_attention(page_table, pad_rows(qs.astype(F32).reshape(n_b, n_s, ATT_WIDTH)), scores, par,
                            pad_rows(ks.reshape(n_b, n_s, ATT_WIDTH)), pad_rows(vs.reshape(n_b, n_s, ATT_WIDTH)),
                            tri[:S_TIE, :S_TIE].astype(F32), ck_t, cv_t)
    o_s = o_s[:, :n_s].reshape(n_tok, ATT_WIDTH).astype(BF16)
    h1s, h1bs, combs = _mix_ln_route(xs, None, ys_s, o_s, gas, gbs, wso, wao, wo, g1, b1, wrh, wrl, br,
                                     tm=n_tok, head_rows=0)
    y_sample = _moe_ln(h1s, h1bs, combs, wgu, wd, g2, b2, tm=n_tok, head_rows=0)

    def heads(a, lead):
        return a.reshape((depth,) + lead + (N_HEADS, HEAD_DIM))

    return (y_prompt.reshape(1, seq, D_MODEL),
            y_sample.reshape(n_b, n_s, D_MODEL),
            heads(k[PADF:], (1, t_len)), heads(v[PADF:], (1, t_len)),
            kiw[PADF:, :IDX_DIM].reshape(depth, 1, t_len, IDX_DIM),
            st[0].reshape(depth, 1, SSM_GROUPS, SSM_STATE), st[1].reshape(depth, 1, SSM_GROUPS, SSM_STATE),
            heads(ks, (n_b, n_s)), heads(vs, (n_b, n_s)),
            kiws[:, :IDX_DIM].reshape(depth, n_b, n_s, IDX_DIM),
            hre_s.reshape(depth, n_b, SSM_GROUPS, SSM_STATE), him_s.reshape(depth, n_b, SSM_GROUPS, SSM_STATE))
```

```python
import functools
import math

import jax
import jax.numpy as jnp
from jax import lax
from jax.experimental import pallas as pl
from jax.experimental.pallas import tpu as pltpu

F32 = jnp.float32
BF16 = jnp.bfloat16
I32 = jnp.int32

D_MODEL = 1024
N_META = 16
SSM_WIDTH = 512
SSM_GROUP_CH = 16
SSM_GROUPS = 32
SSM_STATE = 64
N_STATE = SSM_GROUPS * SSM_STATE
N_HEADS = 8
HEAD_DIM = 64
ATT_WIDTH = N_HEADS * HEAD_DIM
IDX_HEADS = 4
IDX_DIM = 64
TOPK = 256
PAGE = 128
N_EGROUPS = 4
EXPERTS_PER_GROUP = 4
N_EXPERTS = 16
D_EXPERT = 256
DN_ALPHA = 2.0 ** 0.25
LN_EPS = 1e-5
NEG_INF = float("-inf")

C_U, C_Q, C_K, C_V, C_QI, C_KIW, C_GA, C_GB, C_END = 0, 512, 1024, 1536, 2048, 2304, 2432, 3456, 4480
IN_SPLITS = (512, 512, 512, 512, 256, 64, 4, 1024, 1024)

ROW_TILE = 512
HEAD_ROWS = 1024
PADF = HEAD_ROWS - N_META
TQ = 256
TK = 1024
LANES = 128
HEAD_KEY0 = HEAD_ROWS - LANES
TIE_W = 256
M_INIT = -1e30
VMEM_LIMIT = 56 * 1024 * 1024

KEY_NEG_INF = -2139095041
KEY_POS_INF = 2139095040


def _cparams(n_axes, vmem=None):
    return pltpu.CompilerParams(dimension_semantics=("arbitrary",) * n_axes, vmem_limit_bytes=vmem)


def _const_spec(shape):
    nd = len(shape)
    return pl.BlockSpec(shape, lambda *_: (0,) * nd)


def _resident_spec(shape):
    nd = len(shape)
    return pl.BlockSpec(shape, lambda *_: (0,) * nd, pipeline_mode=pl.Buffered(1))


def _sigmoid(x):
    return 1.0 / (1.0 + jnp.exp(-x))


def _gelu_tanh(x):
    return 0.5 * x * (1.0 + jnp.tanh(math.sqrt(2.0 / math.pi) * (x + 0.044715 * (x * x * x))))


def _layernorm(z, g, b):
    mu = jnp.mean(z, axis=-1, keepdims=True)
    zc = z - mu
    var = jnp.mean(zc * zc, axis=-1, keepdims=True)
    return zc * lax.rsqrt(var + LN_EPS) * g + b


def _inproj_body(x_ref, h0_ref, w_ref, u_ref, q_ref, k_ref, v_ref, kb_ref, vb_ref, qi_ref, kiw_ref,
                 ga_ref, gb_ref, *, n_head_tiles):
    if n_head_tiles:
        h = jnp.where(pl.program_id(0) < n_head_tiles, h0_ref[...], x_ref[...])
    else:
        h = x_ref[...]
    hb = h.astype(BF16)

    def seg(a, b):
        return jnp.dot(hb, w_ref[:, a:b], preferred_element_type=F32)

    u_ref[...] = seg(C_U, C_Q)
    q_ref[...] = (seg(C_Q, C_K) * HEAD_DIM ** -0.5).astype(BF16)
    k = seg(C_K, C_V)
    k_ref[...] = k
    kb_ref[...] = k.astype(BF16)
    v = seg(C_V, C_QI)
    v_ref[...] = v
    vb_ref[...] = v.astype(BF16)
    qi_ref[...] = seg(C_QI, C_KIW).astype(BF16)
    kiw_ref[...] = seg(C_KIW, C_GA)
    ga_ref[...] = seg(C_GA, C_GB)
    gb_ref[...] = seg(C_GB, C_END)


def _in_proj(x, h0, wp, *, tm, head_rows):
    n_head_tiles = head_rows // tm
    n_rows = head_rows + x.shape[0]
    grid = (n_rows // tm,)
    if n_head_tiles:
        x_spec = pl.BlockSpec((tm, D_MODEL), lambda i: (jnp.maximum(i - n_head_tiles, 0), 0))
        h_spec = pl.BlockSpec((tm, D_MODEL), lambda i: (jnp.minimum(i, n_head_tiles - 1), 0))
    else:
        h0 = x
        x_spec = pl.BlockSpec((tm, D_MODEL), lambda i: (i, 0))
        h_spec = pl.BlockSpec((tm, D_MODEL), lambda i: (0, 0))

    def row(width):
        return pl.BlockSpec((tm, width), lambda i: (i, 0))

    widths = (512, 512, 512, 512, 512, 512, 256, 128, 1024, 1024)
    dtypes = (F32, BF16, F32, F32, BF16, BF16, BF16, F32, F32, F32)
    return pl.pallas_call(
        functools.partial(_inproj_body, n_head_tiles=n_head_tiles),
        grid=grid,
        in_specs=[x_spec, h_spec, _resident_spec((D_MODEL, C_END))],
        out_specs=[row(w) for w in widths],
        out_shape=[jax.ShapeDtypeStruct((n_rows, w), d) for w, d in zip(widths, dtypes)],
        compiler_params=_cparams(1, VMEM_LIMIT),
        name="in_proj",
    )(x, h0, wp)


SSM_CHUNK = 256


SSM_C_TILE = 128
SSM_C_DEPTH = (SSM_C_TILE // SSM_GROUP_CH) * SSM_STATE
SSM_B_TILE = 256
SSM_B_DEPTH = (SSM_B_TILE // SSM_STATE) * SSM_GROUP_CH


def _ssm_out(h_cat, u, cbig_ref, d_ref, wglu_ref):
    hb = h_cat.astype(BF16)
    tiles = []
    for t in range(SSM_WIDTH // SSM_C_TILE):
        cols = slice(SSM_C_TILE * t, SSM_C_TILE * (t + 1))
        re = slice(SSM_C_DEPTH * t, SSM_C_DEPTH * (t + 1))
        im = slice(N_STATE + SSM_C_DEPTH * t, N_STATE + SSM_C_DEPTH * (t + 1))
        tiles.append(jnp.dot(hb[:, re], cbig_ref[re, cols], preferred_element_type=F32)
                     + jnp.dot(hb[:, im], cbig_ref[im, cols], preferred_element_type=F32))
    y_lin = jnp.concatenate(tiles, axis=1)
    y = _gelu_tanh(y_lin + d_ref[...] * u)
    z = jnp.dot(y.astype(BF16), wglu_ref[...], preferred_element_type=F32)
    return y * _sigmoid(z)


def _ssm_prompt_body(u_ref, bbig_ref, apw_ref, cbig_ref, d_ref, wglu_ref, y_ref, st_ref,
                     bu_ref, hs_ref, hc_ref):
    @pl.when(pl.program_id(0) == 0)
    def _():
        hc_ref[...] = jnp.zeros_like(hc_ref)

    u = u_ref[...]
    ub = u.astype(BF16)
    for j in range(2 * N_STATE // SSM_B_TILE):
        jj = j % (N_STATE // SSM_B_TILE)
        cols = slice(SSM_B_TILE * j, SSM_B_TILE * (j + 1))
        rows = slice(SSM_B_DEPTH * jj, SSM_B_DEPTH * (jj + 1))
        bu_ref[:, cols] = jnp.dot(ub[:, rows], bbig_ref[rows, cols], preferred_element_type=F32)

    def step(r, carry):
        h_re, h_im = carry
        r8 = pl.multiple_of(r * 8, 8)
        x_re = bu_ref[pl.ds(r8, 8), 0:N_STATE]
        x_im = bu_ref[pl.ds(r8, 8), N_STATE:2 * N_STATE]
        for lvl, sh in enumerate((1, 2, 4)):
            a_re = apw_ref[2 * lvl]
            a_im = apw_ref[2 * lvl + 1]
            s_re = pltpu.roll(x_re, sh, 0)
            s_im = pltpu.roll(x_im, sh, 0)
            x_re, x_im = (x_re + (a_re * s_re - a_im * s_im), x_im + (a_re * s_im + a_im * s_re))
        c_re = apw_ref[6]
        c_im = apw_ref[7]
        x_re, x_im = (x_re + (c_re * h_re - c_im * h_im), x_im + (c_re * h_im + c_im * h_re))
        hs_ref[pl.ds(r8, 8), 0:N_STATE] = x_re
        hs_ref[pl.ds(r8, 8), N_STATE:2 * N_STATE] = x_im
        return x_re[7:8, :], x_im[7:8, :]

    h_re, h_im = lax.fori_loop(0, SSM_CHUNK // 8, step, (hc_ref[0:1, :], hc_ref[1:2, :]))
    hc_ref[0:1, :] = h_re
    hc_ref[1:2, :] = h_im
    st_ref[...] = hc_ref[...]
    y_ref[...] = _ssm_out(hs_ref[...], u, cbig_ref, d_ref, wglu_ref).astype(BF16)


def _ssm_prompt(u, bbig, apw, cbig, dskip, wglu):
    n_rows = u.shape[0]
    return pl.pallas_call(
        _ssm_prompt_body,
        grid=(n_rows // SSM_CHUNK,),
        in_specs=[pl.BlockSpec((SSM_CHUNK, SSM_WIDTH), lambda i: (i, 0)),
                  _const_spec((SSM_WIDTH, 2 * N_STATE)), _const_spec((8, 8, N_STATE)),
                  _const_spec((2 * N_STATE, SSM_WIDTH)), _const_spec((1, SSM_WIDTH)),
                  _const_spec((SSM_WIDTH, SSM_WIDTH))],
        out_specs=[pl.BlockSpec((SSM_CHUNK, SSM_WIDTH), lambda i: (i, 0)), _const_spec((2, N_STATE))],
        out_shape=[jax.ShapeDtypeStruct((n_rows, SSM_WIDTH), BF16),
                   jax.ShapeDtypeStruct((2, N_STATE), F32)],
        scratch_shapes=[pltpu.VMEM((SSM_CHUNK, 2 * N_STATE), F32), pltpu.VMEM((SSM_CHUNK, 2 * N_STATE), F32),
                        pltpu.VMEM((2, N_STATE), F32)],
        compiler_params=_cparams(1, VMEM_LIMIT),
        name="ssm_prompt",
    )(u, bbig, apw, cbig, dskip, wglu)


def _ssm_sample_body(u_ref, h0re_ref, h0im_ref, bhi_ref, blo_ref, a_ref, cbig_ref, d_ref, wglu_ref,
                     y_ref, hre_ref, him_ref):
    a_re = a_ref[0:1, :]
    a_im = a_ref[1:2, :]
    h_re = h0re_ref[...]
    h_im = h0im_ref[...]
    for s in range(u_ref.shape[0]):
        u = u_ref[s]
        u_hi = u.astype(BF16)
        u_lo = (u - u_hi.astype(F32)).astype(BF16)
        bu = (jnp.dot(u_hi, bhi_ref[...], preferred_element_type=F32)
              + (jnp.dot(u_lo, bhi_ref[...], preferred_element_type=F32)
                 + jnp.dot(u_hi, blo_ref[...], preferred_element_type=F32)))
        h_re, h_im = (a_re * h_re - a_im * h_im + bu[:, 0:N_STATE],
                      a_re * h_im + a_im * h_re + bu[:, N_STATE:2 * N_STATE])
        h_cat = jnp.concatenate([h_re, h_im], axis=1)
        y_ref[s] = _ssm_out(h_cat, u, cbig_ref, d_ref, wglu_ref).astype(BF16)
    hre_ref[...] = h_re
    him_ref[...] = h_im


def _ssm_sample(u_s, h0_re, h0_im, bhi, blo, a_pair, cbig, dskip, wglu):
    n_s, n_b, _ = u_s.shape
    return pl.pallas_call(
        _ssm_sample_body,
        grid=(1,),
        in_specs=[_const_spec(u_s.shape), _const_spec(h0_re.shape), _const_spec(h0_im.shape),
                  _const_spec(bhi.shape), _const_spec(blo.shape), _const_spec(a_pair.shape),
                  _const_spec(cbig.shape), _const_spec(dskip.shape), _const_spec(wglu.shape)],
        out_specs=[_const_spec((n_s, n_b, SSM_WIDTH)), _const_spec((n_b, N_STATE)), _const_spec((n_b, N_STATE))],
        out_shape=[jax.ShapeDtypeStruct((n_s, n_b, SSM_WIDTH), BF16),
                   jax.ShapeDtypeStruct((n_b, N_STATE), F32), jax.ShapeDtypeStruct((n_b, N_STATE), F32)],
        compiler_params=_cparams(1, VMEM_LIMIT),
        name="ssm_sample",
    )(u_s, h0_re, h0_im, bhi, blo, a_pair, cbig, dskip, wglu)


def _key_to_float(key):
    bits = key ^ ((key >> 31) & 0x7FFFFFFF)
    return lax.bitcast_convert_type(bits, F32)


def _float_to_key(x):
    bits = lax.bitcast_convert_type(x, I32)
    return bits ^ ((bits >> 31) & 0x7FFFFFFF)


BISECT_EVERY = 8
SELECT_MAX_PASSES = 32 * BISECT_EVERY + 8


def _kth_largest(count_fn, n_all, lane_max):
    kk = float(TOPK)
    n_grp = lane_max.shape[0] // LANES

    def to_cols(vec):
        cols = [jnp.broadcast_to(vec[g:g + 1], (LANES, LANES)).T for g in range(n_grp)]
        return cols[0] if n_grp == 1 else jnp.concatenate(cols, axis=0)

    def per_row(part, reduce):
        red = [reduce(part[LANES * g:LANES * (g + 1)].T, axis=0, keepdims=True) for g in range(n_grp)]
        return red[0] if n_grp == 1 else jnp.concatenate(red, axis=0)

    def total(th_vec, strict):
        return per_row(count_fn(to_cols(th_vec), strict), jnp.sum)

    row_max = per_row(lane_max, jnp.max)
    zero = jnp.zeros((n_grp, LANES), F32)
    c_ge0 = total(zero, False)
    c_gt0 = total(zero, True)
    pos = c_gt0 > kk
    at0 = (c_ge0 >= kk) & (c_gt0 <= kk)
    lo0 = jnp.where(pos | at0, 0, KEY_NEG_INF).astype(I32)
    hi0 = jnp.where(pos, _float_to_key(row_max) + 1, jnp.where(at0, 0, -1)).astype(I32)
    c_lo0 = jnp.where(pos, c_gt0, jnp.where(at0, c_ge0, n_all))
    c_hi0 = jnp.where(pos, 0.0, jnp.where(at0, c_gt0, c_ge0))
    one = jnp.ones((n_grp, LANES), F32)

    def mid_key(lo, hi):
        return (lo >> 1) + (hi >> 1) + (lo & hi & 1)

    def finished(lo, hi, c_lo):
        return (c_lo == kk) | (mid_key(lo, hi) == lo)

    def cond(st):
        it, lo, hi, c_lo = st[0], st[1], st[2], st[3]
        open_rows = jnp.where(finished(lo, hi, c_lo), 0.0, 1.0)
        return (it < SELECT_MAX_PASSES) & (jnp.max(open_rows) > 0.0)

    def body(st):
        it, lo, hi, c_lo, c_hi, w_lo, w_hi, side = st
        done = finished(lo, hi, c_lo)
        lo_v = _key_to_float(lo)
        hi_v = _key_to_float(hi)
        g_lo = jnp.log(c_lo * (1.0 / (kk - 0.5))) * w_lo
        g_hi = jnp.log(jnp.maximum(c_hi, 0.5) * (1.0 / (kk - 0.5))) * w_hi
        th_i = lo_v + (g_lo / (g_lo - g_hi)) * (hi_v - lo_v)
        bounded = (lo_v > NEG_INF) & (hi_v < float("inf"))
        k_i = jnp.minimum(jnp.maximum(_float_to_key(th_i), lo + 1), hi - 1)
        k_mid = mid_key(lo, hi)
        probe = jnp.where((it & (BISECT_EVERY - 1)) == BISECT_EVERY - 1, k_mid, jnp.where(bounded, k_i, k_mid))
        c = total(_key_to_float(probe), False)
        ge = c >= kk
        mv_lo = (~done) & ge
        mv_hi = (~done) & (~ge)
        w_hi = jnp.where(mv_lo, jnp.where(side > 0.0, w_hi * 0.5, one), jnp.where(mv_hi, one, w_hi))
        w_lo = jnp.where(mv_hi, jnp.where(side < 0.0, w_lo * 0.5, one), jnp.where(mv_lo, one, w_lo))
        side = jnp.where(mv_lo, 1.0, jnp.where(mv_hi, -1.0, side))
        return (it + 1, jnp.where(mv_lo, probe, lo), jnp.where(mv_hi, probe, hi),
                jnp.where(mv_lo, c, c_lo), jnp.where(mv_hi, c, c_hi), w_lo, w_hi, side)

    st = lax.while_loop(cond, lambda s: body(body(s)), (jnp.int32(0), lo0, hi0, c_lo0, c_hi0, one, one, zero))
    lo, c_lo, c_hi = st[1], st[3], st[4]
    v = _key_to_float(lo)
    need = jnp.where(v == NEG_INF, 0.0, jnp.where(c_lo == kk, kk, kk - c_hi))
    return to_cols(v)[:, 0:1], to_cols(need)[:, 0:1]


def _topk_mask(x, v, need, off, tri):
    eq = x == v
    cum = jnp.dot(jnp.where(eq, 1.0, 0.0).astype(tri.dtype), tri, preferred_element_type=F32)
    sel = (x > v) | (eq & ((cum + off) <= need))
    w = x.shape[1]
    return sel, off + cum[:, w - 1:w]


def _pattn_body(qi_ref, kiw_ref, q_ref, kit_ref, kt_ref, v_hbm, tri_ref, o_ref,
                sc_ref, m_ref, l_ref, acc_ref, vhead_ref, vbuf_ref, vsem, *, first_block):
    i = pl.program_id(0)
    row0 = i * TQ

    @pl.when(i < first_block)
    def _():
        o_ref[...] = jnp.zeros_like(o_ref)

    @pl.when(i >= first_block)
    def _():
        jd = row0 // TK

        def head_copy():
            return pltpu.make_async_copy(v_hbm.at[pl.ds(HEAD_KEY0, LANES), :], vhead_ref, vsem.at[2])

        def tile_copy(j):
            return pltpu.make_async_copy(v_hbm.at[pl.ds(pl.multiple_of(j * TK, TK), TK), :],
                                         vbuf_ref.at[j & 1], vsem.at[j & 1])

        head_copy().start()

        @pl.when(jd >= 1)
        def _():
            tile_copy(1).start()
        w = kiw_ref[:, IDX_DIM:IDX_DIM + IDX_HEADS] * (IDX_HEADS ** -0.5 * IDX_DIM ** -0.5)
        qi_h = [qi_ref[:, IDX_DIM * h:IDX_DIM * (h + 1)] for h in range(IDX_HEADS)]
        q_h = [q_ref[:, HEAD_DIM * h:HEAD_DIM * (h + 1)] for h in range(N_HEADS)]

        def score_tile(c0, width, masked, mx):
            kit = kit_ref[:, pl.ds(c0, width)]
            tot = None
            for h in range(IDX_HEADS):
                z = jnp.dot(qi_h[h], kit, preferred_element_type=F32)
                r = jnp.maximum(z, 0.0) * w[:, h:h + 1]
                tot = r if tot is None else tot + r
            if masked:
                s_idx = c0 + lax.broadcasted_iota(I32, (TQ, width), 1)
                t_idx = row0 + lax.broadcasted_iota(I32, (TQ, width), 0)
                tot = jnp.where((s_idx <= t_idx) & (s_idx >= PADF), tot, NEG_INF)
            sc_ref[:, pl.ds(c0, width)] = tot
            for a in range(width // LANES):
                mx = jnp.maximum(mx, tot[:, LANES * a:LANES * (a + 1)])
            return mx

        def main_start(j):
            return pl.multiple_of(j * TK, TK)

        n_var = TK // TQ
        d_var = (row0 // TQ) % n_var
        d_widths = [TQ * (d + 1) for d in range(n_var)]
        d_sel = jnp.where(jd >= 1, d_var, n_var)

        def on_diag(fn, carry):
            return lax.switch(d_sel, [functools.partial(fn, wd) for wd in d_widths] + [lambda c: c], carry)

        mx = score_tile(HEAD_KEY0, LANES, True, jnp.full((TQ, LANES), NEG_INF, F32))
        mx = lax.fori_loop(1, jd, lambda j, mx: score_tile(main_start(j), TK, False, mx), mx)
        mx = on_diag(lambda wd, mx: score_tile(main_start(jd), wd, True, mx), mx)

        def count_fn(th, strict):
            groups = []
            for g in range(TQ // LANES):
                rows = pl.ds(LANES * g, LANES)
                th_g = th[LANES * g:LANES * (g + 1)]

                def add(cnt, x, th_g=th_g):
                    return cnt + jnp.where((x > th_g) if strict else (x >= th_g), 1.0, 0.0)

                def count_tile(width, j, cnt, rows=rows, add=add):
                    c0 = main_start(j)
                    for a in range(width // LANES):
                        cnt = add(cnt, sc_ref[rows, pl.ds(c0 + a * LANES, LANES)])
                    return cnt

                cnt = add(jnp.zeros((LANES, LANES), F32), sc_ref[rows, HEAD_KEY0:HEAD_ROWS])
                cnt = lax.fori_loop(1, jd, functools.partial(count_tile, TK), cnt)
                groups.append(on_diag(lambda wd, cnt, count_tile=count_tile: count_tile(wd, jd, cnt), cnt))
            return jnp.concatenate(groups, axis=0)

        n_all = jnp.where(jd >= 1, LANES + TK * (jd - 1) + TQ * (d_var + 1), LANES).astype(F32)
        v, need = _kth_largest(count_fn, n_all, mx)

        m_ref[...] = jnp.full_like(m_ref, M_INIT)
        l_ref[...] = jnp.zeros_like(l_ref)
        acc_ref[...] = jnp.zeros_like(acc_ref)
        first_half = lax.broadcasted_iota(I32, (TQ, LANES), 1) < HEAD_DIM

        def attend_tile(c0, width, off, v_cols):
            x = sc_ref[:, pl.ds(c0, width)]
            parts = []
            for a in range(0, width, TIE_W):
                wa = min(TIE_W, width - a)
                sel, off = _topk_mask(x[:, a:a + wa], v, need, off, tri_ref[0:wa, 0:wa])
                parts.append(jnp.where(sel, 0.0, NEG_INF))
            bias = parts[0] if len(parts) == 1 else jnp.concatenate(parts, axis=1)
            n_sl = width // LANES

            def probs(h):
                kt = kt_ref[HEAD_DIM * h:HEAD_DIM * (h + 1), pl.ds(c0, width)]
                s = jnp.dot(q_h[h], kt, preferred_element_type=F32) + bias
                sl = [s[:, LANES * a:LANES * (a + 1)] for a in range(n_sl)]
                pm = sl[0]
                for a in range(1, n_sl):
                    pm = jnp.maximum(pm, sl[a])
                m_old = m_ref[h]
                m_new = jnp.maximum(m_old, jnp.max(pm, axis=1, keepdims=True))
                alpha = jnp.exp(m_old - m_new)
                ps = [jnp.exp(sl[a] - m_new) for a in range(n_sl)]
                psum = ps[0]
                for a in range(1, n_sl):
                    psum = psum + ps[a]
                l_ref[h] = alpha * l_ref[h] + psum
                m_ref[h] = m_new
                return (ps[0] if n_sl == 1 else jnp.concatenate(ps, axis=1)).astype(BF16), alpha

            for j in range(N_HEADS // 2):
                p_e, alpha_e = probs(2 * j)
                p_o, alpha_o = probs(2 * j + 1)
                cols = slice(LANES * j, LANES * (j + 1))
                pv = jnp.dot(jnp.concatenate([p_e, p_o], axis=0), v_cols(cols),
                             preferred_element_type=F32)
                a_pair = jnp.where(first_half, alpha_e, alpha_o)
                pv_pair = jnp.where(first_half, pv[0:TQ], pv[TQ:2 * TQ])
                acc_ref[:, cols] = a_pair * acc_ref[:, cols] + pv_pair
            return off

        head_copy().wait()
        off = attend_tile(HEAD_KEY0, LANES, jnp.zeros((TQ, 1), F32), lambda cols: vhead_ref[:, cols])

        def attend_main(j, off):
            tile_copy(j).wait()
            tile_copy(j + 1).start()
            return attend_tile(main_start(j), TK, off, lambda cols: vbuf_ref[j & 1, :, cols])

        off = lax.fori_loop(1, jd, attend_main, off)

        @pl.when(jd >= 1)
        def _():
            tile_copy(jd).wait()

        on_diag(lambda wd, off: attend_tile(main_start(jd), wd, off,
                                            lambda cols: vbuf_ref[jd & 1, 0:wd, cols]), off)

        for j in range(N_HEADS // 2):
            l_e = jnp.sum(l_ref[2 * j], axis=1, keepdims=True)
            l_o = jnp.sum(l_ref[2 * j + 1], axis=1, keepdims=True)
            l_pair = jnp.where(first_half, l_e, l_o)
            inv = 1.0 / jnp.where(l_pair > 0.0, l_pair, 1.0)
            o_ref[:, LANES * j:LANES * (j + 1)] = (acc_ref[:, LANES * j:LANES * (j + 1)] * inv).astype(o_ref.dtype)


def _prompt_attention(qi, kiw, q, kit, kt, vb, tri):
    n_rows = q.shape[0]

    def row(width):
        return pl.BlockSpec((TQ, width), lambda i: (i, 0))

    return pl.pallas_call(
        functools.partial(_pattn_body, first_block=HEAD_KEY0 // TQ),
        grid=(n_rows // TQ,),
        in_specs=[row(IDX_HEADS * IDX_DIM), row(128), row(ATT_WIDTH),
                  _resident_spec(kit.shape), _resident_spec(kt.shape), pl.BlockSpec(memory_space=pl.ANY),
                  _resident_spec(tri.shape)],
        out_specs=row(ATT_WIDTH),
        out_shape=jax.ShapeDtypeStruct((n_rows, ATT_WIDTH), BF16),
        scratch_shapes=[pltpu.VMEM((TQ, n_rows), F32), pltpu.VMEM((N_HEADS, TQ, LANES), F32),
                        pltpu.VMEM((N_HEADS, TQ, LANES), F32), pltpu.VMEM((TQ, ATT_WIDTH), F32),
                        pltpu.VMEM((LANES, ATT_WIDTH), BF16),
                        pltpu.VMEM((2, TK, ATT_WIDTH), BF16), pltpu.SemaphoreType.DMA((3,))],
        compiler_params=_cparams(1, VMEM_LIMIT),
        name="prompt_attention",
    )(qi, kiw, q, kit, kt, vb, tri)


PAGES_PER_STEP = 32
SCORE_PAGES_PER_STEP = 64
S_CHUNK = PAGES_PER_STEP * PAGE
S_ROWS = 8


def _idx_scores_rows(qi, w, keys_bf16):
    z = lax.dot_general(qi, keys_bf16, (((1,), (1,)), ((), ())), preferred_element_type=F32)
    r = jnp.maximum(z, 0.0) * w
    tot = r[0:S_ROWS]
    for h in range(1, IDX_HEADS):
        tot = tot + r[S_ROWS * h:S_ROWS * (h + 1)]
    return tot


def _sscore_body(pt_ref, qi_ref, w_ref, kinew_ref, *refs, n_chunks, past, pages):
    page_refs = refs[:pages]
    sc_ref, kic_ref = refs[pages:]
    chunk = pages * PAGE
    c = pl.program_id(1)
    qi = qi_ref[...]
    w = w_ref[...]
    for r in range(pages):
        kic_ref[:, r * PAGE:(r + 1) * PAGE] = page_refs[r][...].astype(BF16)
    rr = jnp.maximum(jnp.dot(qi, kic_ref[...], preferred_element_type=F32), 0.0) * w
    tot = rr[0:S_ROWS]
    for h in range(1, IDX_HEADS):
        tot = tot + rr[S_ROWS * h:S_ROWS * (h + 1)]
    sc_ref[:, pl.ds(pl.multiple_of(c * chunk, chunk), chunk)] = tot

    @pl.when(c == n_chunks - 1)
    def _():
        tot = _idx_scores_rows(qi, w, kinew_ref[...])
        lane = lax.broadcasted_iota(I32, (S_ROWS, PAGE), 1)
        row = lax.broadcasted_iota(I32, (S_ROWS, PAGE), 0)
        sc_ref[:, past:past + PAGE] = jnp.where(lane <= row, tot, NEG_INF)


def _sselect_body(sc_ref, par_ref, *, sel_tile):
    n_tiles = sc_ref.shape[1] // sel_tile

    def tiles(fn, init):
        def body(j, acc):
            c0 = pl.multiple_of(j * sel_tile, LANES)
            for a in range(sel_tile // LANES):
                acc = fn(acc, sc_ref[:, pl.ds(c0 + a * LANES, LANES)])
            return acc

        return lax.fori_loop(0, n_tiles, body, init)

    def count_fn(th, strict):
        return tiles(lambda cnt, x: cnt + jnp.where((x > th) if strict else (x >= th), 1.0, 0.0),
                     jnp.zeros((LANES, LANES), F32))

    lane_max = tiles(jnp.maximum, jnp.full((LANES, LANES), NEG_INF, F32))
    v, need = _kth_largest(count_fn, float(sc_ref.shape[1]), lane_max)
    lane = lax.broadcasted_iota(I32, (LANES, LANES), 1)
    par_ref[...] = jnp.where(lane == 0, v, jnp.where(lane == 1, need, 0.0))


def _sample_select(scores2d):
    n_rows, width = scores2d.shape
    sel_tile = max(t for t in range(LANES, 6 * LANES, LANES) if width % t == 0)
    pad = -n_rows % LANES
    if pad:
        scores2d = jnp.pad(scores2d, [(0, pad), (0, 0)])
    par = pl.pallas_call(
        functools.partial(_sselect_body, sel_tile=sel_tile),
        grid=((n_rows + pad) // LANES,),
        in_specs=[pl.BlockSpec((LANES, width), lambda i: (i, 0))],
        out_specs=pl.BlockSpec((LANES, LANES), lambda i: (i, 0)),
        out_shape=jax.ShapeDtypeStruct((n_rows + pad, LANES), F32),
        compiler_params=_cparams(1, VMEM_LIMIT),
        name="sample_select",
    )(scores2d)
    return par[:n_rows]


def _sample_scores(page_table, qi_s, w_s, kinew, cache_kidx):
    n_b, n_pages = page_table.shape
    pages = math.gcd(n_pages, SCORE_PAGES_PER_STEP)
    n_chunks = n_pages // pages
    past = n_pages * PAGE
    width = past + PAGE

    def page_spec(r):
        return pl.BlockSpec((None, IDX_DIM, PAGE), lambda b, c, pt: (pt[b, c * pages + r], 0, 0))

    def per_b(shape):
        return pl.BlockSpec((None,) + shape, lambda b, c, pt: (b, 0, 0))

    grid_spec = pltpu.PrefetchScalarGridSpec(
        num_scalar_prefetch=1,
        grid=(n_b, n_chunks),
        in_specs=[per_b((IDX_HEADS * S_ROWS, IDX_DIM)), per_b((IDX_HEADS * S_ROWS, 1)), per_b((PAGE, IDX_DIM))]
                 + [page_spec(r) for r in range(pages)],
        out_specs=per_b((S_ROWS, width)),
        scratch_shapes=[pltpu.VMEM((IDX_DIM, pages * PAGE), BF16)],
    )
    scores = pl.pallas_call(
        functools.partial(_sscore_body, n_chunks=n_chunks, past=past, pages=pages),
        grid_spec=grid_spec,
        out_shape=jax.ShapeDtypeStruct((n_b, S_ROWS, width), F32),
        compiler_params=_cparams(2, VMEM_LIMIT),
        name="sample_scores",
    )(page_table, qi_s, w_s, kinew, *([cache_kidx] * pages))
    par = _sample_select(scores.reshape(n_b * S_ROWS, width)).reshape(n_b, S_ROWS, LANES)
    return scores, par


S_TIE = 256


def _sattn_body(pt_ref, q_ref, sc_ref, scn_ref, par_ref, knew_ref, vnew_ref, tri_ref, *refs, n_chunks):
    k_refs = refs[:PAGES_PER_STEP]
    v_refs = refs[PAGES_PER_STEP:2 * PAGES_PER_STEP]
    o_ref, qbd_ref, kc_ref, vc_ref, m_ref, l_ref, acc_ref, off_ref = refs[2 * PAGES_PER_STEP:]
    c = pl.program_id(1)
    n_q = 4
    n_rows = n_q * N_HEADS
    head_of_col = lax.broadcasted_iota(I32, (N_HEADS, ATT_WIDTH), 1) // HEAD_DIM
    own_head = head_of_col == lax.broadcasted_iota(I32, (N_HEADS, ATT_WIDTH), 0)

    @pl.when(c == 0)
    def _():
        for s in range(n_q):
            qs = jnp.broadcast_to(q_ref[s:s + 1, :], (N_HEADS, ATT_WIDTH))
            qbd_ref[N_HEADS * s:N_HEADS * (s + 1), :] = jnp.where(own_head, qs, 0.0)
        m_ref[...] = jnp.full_like(m_ref, M_INIT)
        l_ref[...] = jnp.zeros_like(l_ref)
        acc_ref[...] = jnp.zeros_like(acc_ref)
        off_ref[...] = jnp.zeros_like(off_ref)

    v_thr = par_ref[:, 0:1]
    need = par_ref[:, 1:2]
    qbd = qbd_ref[...].astype(BF16)

    nt_dims = (((1,), (1,)), ((), ()))

    def attend(s_raw, pv_fn, x):
        width = x.shape[1]
        off = off_ref[...]
        parts = []
        for a in range(0, width, S_TIE):
            wa = min(S_TIE, width - a)
            sel, off = _topk_mask(x[:, a:a + wa], v_thr, need, off, tri_ref[0:wa, 0:wa])
            parts.append(jnp.where(sel, 0.0, NEG_INF))
        off_ref[...] = off
        bias8 = parts[0] if len(parts) == 1 else jnp.concatenate(parts, axis=1)
        bias = jnp.concatenate([jnp.broadcast_to(bias8[s:s + 1, :], (N_HEADS, width)) for s in range(n_q)], axis=0)
        s_mat = s_raw + bias
        m_old = m_ref[...]
        m_new = jnp.maximum(m_old, jnp.max(s_mat, axis=1, keepdims=True))
        p = jnp.exp(s_mat - m_new)
        alpha = jnp.exp(m_old - m_new)
        l_ref[...] = alpha * l_ref[...] + jnp.sum(p, axis=1, keepdims=True)
        acc_ref[...] = alpha * acc_ref[...] + pv_fn(p.astype(BF16))
        m_ref[...] = m_new

    for r in range(PAGES_PER_STEP):
        kc_ref[:, r * PAGE:(r + 1) * PAGE] = k_refs[r][...].astype(BF16)
        vc_ref[:, r * PAGE:(r + 1) * PAGE] = v_refs[r][...].astype(BF16)
    attend(jnp.dot(qbd, kc_ref[...], preferred_element_type=F32),
           lambda p: lax.dot_general(p, vc_ref[...], nt_dims, preferred_element_type=F32), sc_ref[...])

    @pl.when(c == n_chunks - 1)
    def _():
        pad = jnp.zeros((PAGE - S_ROWS, ATT_WIDTH), F32)
        k_new = jnp.concatenate([knew_ref[...], pad], axis=0).astype(BF16)
        v_new = jnp.concatenate([vnew_ref[...], pad], axis=0).astype(BF16)
        attend(lax.dot_general(qbd, k_new, nt_dims, preferred_element_type=F32),
               lambda p: jnp.dot(p, v_new, preferred_element_type=F32), scn_ref[...])
        l = l_ref[...]
        o_all = acc_ref[...] * (1.0 / jnp.where(l > 0.0, l, 1.0))
        row = lax.broadcasted_iota(I32, (S_ROWS, ATT_WIDTH), 0)
        out = jnp.zeros((S_ROWS, ATT_WIDTH), F32)
        for s in range(n_q):
            o_s = jnp.sum(jnp.where(own_head, o_all[N_HEADS * s:N_HEADS * (s + 1), :], 0.0), axis=0, keepdims=True)
            out = jnp.where(row == s, jnp.broadcast_to(o_s, (S_ROWS, ATT_WIDTH)), out)
        o_ref[...] = out


def _sample_attention(page_table, q_s, scores, par, knew, vnew, tri, cache_k, cache_v):
    n_b, n_pages = page_table.shape
    n_chunks = n_pages // PAGES_PER_STEP
    n_rows = 4 * N_HEADS

    def page_spec(r):
        return pl.BlockSpec((None, ATT_WIDTH, PAGE), lambda b, c, pt: (pt[b, c * PAGES_PER_STEP + r], 0, 0))

    def per_b(shape):
        return pl.BlockSpec((None,) + shape, lambda b, c, pt: (b, 0, 0))

    grid_spec = pltpu.PrefetchScalarGridSpec(
        num_scalar_prefetch=1,
        grid=(n_b, n_chunks),
        in_specs=[per_b((S_ROWS, ATT_WIDTH)),
                  pl.BlockSpec((None, S_ROWS, S_CHUNK), lambda b, c, pt: (b, 0, c)),
                  pl.BlockSpec((None, S_ROWS, PAGE), lambda b, c, pt: (b, 0, n_pages)),
                  per_b((S_ROWS, PAGE)), per_b((S_ROWS, ATT_WIDTH)), per_b((S_ROWS, ATT_WIDTH)),
                  pl.BlockSpec((S_TIE, S_TIE), lambda b, c, pt: (0, 0))]
                 + [page_spec(r) for r in range(PAGES_PER_STEP)] * 2,
        out_specs=per_b((S_ROWS, ATT_WIDTH)),
        scratch_shapes=[pltpu.VMEM((n_rows, ATT_WIDTH), F32),
                        pltpu.VMEM((ATT_WIDTH, S_CHUNK), BF16), pltpu.VMEM((ATT_WIDTH, S_CHUNK), BF16),
                        pltpu.VMEM((n_rows, 1), F32), pltpu.VMEM((n_rows, 1), F32),
                        pltpu.VMEM((n_rows, ATT_WIDTH), F32), pltpu.VMEM((S_ROWS, 1), F32)],
    )
    return pl.pallas_call(
        functools.partial(_sattn_body, n_chunks=n_chunks),
        grid_spec=grid_spec,
        out_shape=jax.ShapeDtypeStruct((n_b, S_ROWS, ATT_WIDTH), F32),
        compiler_params=_cparams(2, VMEM_LIMIT),
        name="sample_attention",
    )(page_table, q_s, scores, scores, par, knew, vnew, tri,
      *([cache_k] * PAGES_PER_STEP), *([cache_v] * PAGES_PER_STEP))


R_LANES = 128
R_E0 = N_EGROUPS


def _route(logits):
    lane = lax.broadcasted_iota(I32, logits.shape, 1)
    lane_f = lane.astype(F32)
    big = float(R_LANES)
    is_g = lane < N_EGROUPS
    gl = jnp.where(is_g, logits, NEG_INF)
    g_max = jnp.max(gl, axis=1, keepdims=True)
    g_sel = jnp.min(jnp.where(gl == g_max, lane_f, big), axis=1, keepdims=True)
    p_g = 1.0 / jnp.sum(jnp.exp(gl - g_max), axis=1, keepdims=True)
    member = (lane >= R_E0) & (lane < R_E0 + N_EXPERTS) & (((lane - R_E0) >> 2).astype(F32) == g_sel)
    el = jnp.where(member, logits, NEG_INF)
    e_max = jnp.max(el, axis=1, keepdims=True)
    pe = jnp.exp(el - e_max)
    pe = pe / jnp.sum(pe, axis=1, keepdims=True)
    p1 = jnp.max(pe, axis=1, keepdims=True)
    i1 = jnp.min(jnp.where(member & (pe == p1), lane_f, big), axis=1, keepdims=True)
    rest = member & (lane_f != i1)
    p2 = jnp.max(jnp.where(rest, pe, -1.0), axis=1, keepdims=True)
    i2 = jnp.min(jnp.where(rest & (pe == p2), lane_f, big), axis=1, keepdims=True)
    tot = p1 + p2
    return jnp.where(lane_f == i1, p1 / tot * p_g, 0.0) + jnp.where(lane_f == i2, p2 / tot * p_g, 0.0)


def _mix_body(x_ref, h0_ref, ys_ref, oa_ref, ga_ref, gb_ref, wso_ref, wao_ref, wo_ref, g1_ref, b1_ref,
              wrh_ref, wrl_ref, br_ref, h1_ref, h1b_ref, comb_ref, *, n_head_tiles):
    if n_head_tiles:
        h = jnp.where(pl.program_id(0) < n_head_tiles, h0_ref[...], x_ref[...])
    else:
        h = x_ref[...]
    a = jnp.dot(ys_ref[...], wso_ref[...], preferred_element_type=F32)
    b = jnp.dot(oa_ref[...], wao_ref[...], preferred_element_type=F32)
    m = _sigmoid(ga_ref[...]) * a + _sigmoid(gb_ref[...]) * b
    mix = jnp.dot(m.astype(BF16), wo_ref[...], preferred_element_type=F32)
    h1 = _layernorm(DN_ALPHA * h + mix, g1_ref[...], b1_ref[...])
    h1_ref[...] = h1
    h_hi = h1.astype(BF16)
    h1b_ref[...] = h_hi
    h_lo = (h1 - h_hi.astype(F32)).astype(BF16)
    logits = (jnp.dot(h_hi, wrh_ref[...], preferred_element_type=F32)
              + (jnp.dot(h_lo, wrh_ref[...], preferred_element_type=F32)
                 + jnp.dot(h_hi, wrl_ref[...], preferred_element_type=F32))) + br_ref[...]
    comb_ref[...] = _route(logits)


def _mix_ln_route(x, h0, ys, oa, ga, gb, wso, wao, wo, g1, b1, wrh, wrl, br, *, tm, head_rows):
    n_head_tiles = head_rows // tm
    n_rows = head_rows + x.shape[0]
    if n_head_tiles:
        x_spec = pl.BlockSpec((tm, D_MODEL), lambda i: (jnp.maximum(i - n_head_tiles, 0), 0))
        h_spec = pl.BlockSpec((tm, D_MODEL), lambda i: (jnp.minimum(i, n_head_tiles - 1), 0))
    else:
        h0 = x
        x_spec = pl.BlockSpec((tm, D_MODEL), lambda i: (i, 0))
        h_spec = pl.BlockSpec((tm, D_MODEL), lambda i: (0, 0))

    def row(width):
        return pl.BlockSpec((tm, width), lambda i: (i, 0))

    consts = (wso, wao, wo, g1, b1, wrh, wrl, br)
    return pl.pallas_call(
        functools.partial(_mix_body, n_head_tiles=n_head_tiles),
        grid=(n_rows // tm,),
        in_specs=[x_spec, h_spec, row(SSM_WIDTH), row(ATT_WIDTH), row(D_MODEL), row(D_MODEL)]
                 + [_const_spec(c.shape) for c in consts],
        out_specs=[row(D_MODEL), row(D_MODEL), row(R_LANES)],
        out_shape=[jax.ShapeDtypeStruct((n_rows, D_MODEL), F32), jax.ShapeDtypeStruct((n_rows, D_MODEL), BF16),
                   jax.ShapeDtypeStruct((n_rows, R_LANES), F32)],
        compiler_params=_cparams(1, VMEM_LIMIT),
        name="mix_ln_route",
    )(x, h0, ys, oa, ga, gb, *consts)


def _moe_body(h1_ref, h1b_ref, comb_ref, wgu_ref, wd_ref, g2_ref, b2_ref, y_ref, acc_ref):
    e = pl.program_id(1)

    @pl.when(e == 0)
    def _():
        acc_ref[...] = jnp.zeros_like(acc_ref)

    gu = jnp.dot(h1b_ref[...], wgu_ref[...], preferred_element_type=F32)
    gate = gu[:, 0:D_EXPERT]
    up = gu[:, D_EXPERT:2 * D_EXPERT]
    act = gate * _sigmoid(gate) * up
    comb = comb_ref[...]
    lane = lax.broadcasted_iota(I32, comb.shape, 1)
    c_e = jnp.sum(jnp.where(lane == e + R_E0, comb, 0.0), axis=1, keepdims=True)
    acc_ref[...] += jnp.dot((act * c_e).astype(BF16), wd_ref[...], preferred_element_type=F32)

    @pl.when(e == N_EXPERTS - 1)
    def _():
        y_ref[...] = _layernorm(DN_ALPHA * h1_ref[...] + acc_ref[...], g2_ref[...], b2_ref[...])


def _moe_ln(h1, h1b, comb, wgu, wd, g2, b2, *, tm, head_rows):
    n_head_tiles = head_rows // tm
    n_rows = h1.shape[0]
    out_rows = n_rows - head_rows

    def row(width):
        return pl.BlockSpec((tm, width), lambda i, e: (i, 0))

    return pl.pallas_call(
        _moe_body,
        grid=(n_rows // tm, N_EXPERTS),
        in_specs=[row(D_MODEL), row(D_MODEL), row(R_LANES),
                  pl.BlockSpec((None, D_MODEL, 2 * D_EXPERT), lambda i, e: (e, 0, 0)),
                  pl.BlockSpec((None, D_EXPERT, D_MODEL), lambda i, e: (e, 0, 0)),
                  pl.BlockSpec((1, D_MODEL), lambda i, e: (0, 0)), pl.BlockSpec((1, D_MODEL), lambda i, e: (0, 0))],
        out_specs=pl.BlockSpec((tm, D_MODEL), lambda i, e: (jnp.maximum(i - n_head_tiles, 0), 0)),
        out_shape=jax.ShapeDtypeStruct((out_rows, D_MODEL), F32),
        scratch_shapes=[pltpu.VMEM((tm, D_MODEL), F32)],
        compiler_params=_cparams(2, VMEM_LIMIT),
        name="moe_ln",
    )(h1, h1b, comb, wgu, wd, g2, b2)


def _ssm_tables(a_re, a_im, log_dt, b_re, b_im, c_re, c_im):
    dt = jnp.exp(log_dt)[:, None]
    mag = jnp.exp(a_re * dt)
    ab_re = mag * jnp.cos(a_im * dt)
    ab_im = mag * jnp.sin(a_im * dt)
    den = a_re * a_re + a_im * a_im
    nr = ab_re - 1.0
    f_re = (nr * a_re + ab_im * a_im) / den
    f_im = (ab_im * a_re - nr * a_im) / den
    bb_re = f_re[..., None] * b_re - f_im[..., None] * b_im
    bb_im = f_re[..., None] * b_im + f_im[..., None] * b_re
    eye = jnp.eye(SSM_GROUPS, dtype=F32)

    def in_mat(bb):
        return jnp.einsum("gpm,gh->gmhp", bb, eye).reshape(SSM_WIDTH, N_STATE)

    def out_mat(cc):
        return jnp.einsum("gmp,gh->gphm", cc, eye).reshape(N_STATE, SSM_WIDTH)

    bbig = jnp.concatenate([in_mat(bb_re), in_mat(bb_im)], axis=1)
    cbig = jnp.concatenate([out_mat(c_re), -out_mat(c_im)], axis=0)
    ar = ab_re.reshape(1, N_STATE)
    ai = ab_im.reshape(1, N_STATE)

    def cmul(x, y):
        return x[0] * y[0] - x[1] * y[1], x[0] * y[1] + x[1] * y[0]

    a1 = (ar, ai)
    pows = [a1]
    for _ in range(7):
        pows.append(cmul(pows[-1], a1))
    row = jnp.arange(8)[:, None]
    tabs = []
    for sh, pw in ((1, pows[0]), (2, pows[1]), (4, pows[3])):
        keep = (row >= sh).astype(F32)
        tabs += [keep * pw[0], keep * pw[1]]
    tabs += [jnp.concatenate([p[0] for p in pows], axis=0), jnp.concatenate([p[1] for p in pows], axis=0)]
    apw = jnp.stack(tabs)
    a_pair = jnp.concatenate([ar, ai], axis=0)
    return bbig, cbig, apw, a_pair


def _split_bf16(x):
    hi = x.astype(BF16)
    return hi, (x - hi.astype(F32)).astype(BF16)


def kernel(x_prompt, x_sample, cache_k, cache_v, cache_kidx, state_ssm_re, state_ssm_im, page_table,
           meta_tokens, w_in, ssm_a_re, ssm_a_im, ssm_log_dt, ssm_b_re, ssm_b_im, ssm_c_re, ssm_c_im,
           ssm_d, w_glu, w_ssm_out, w_att_out, w_o, ln1_g, ln1_b, w_route_group, b_route_group,
           w_route_expert, b_route_expert, w_exp_gate, w_exp_up, w_exp_down, ln2_g, ln2_b):
    depth = w_in.shape[0]
    assert depth == 1 and x_prompt.shape[0] == 1
    n_b, n_s, _ = x_sample.shape
    seq = x_prompt.shape[1]
    t_len = seq + N_META
    lyr = 0

    cuts = [0]
    for c in IN_SPLITS:
        cuts.append(cuts[-1] + c)
    w = w_in[lyr]
    wp = jnp.concatenate([w[:, :cuts[7]], jnp.zeros((D_MODEL, C_GA - C_KIW - IDX_DIM - IDX_HEADS), F32),
                          w[:, cuts[7]:]], axis=1).astype(BF16)
    bbig, cbig, apw, a_pair = _ssm_tables(ssm_a_re[lyr], ssm_a_im[lyr], ssm_log_dt[lyr], ssm_b_re[lyr],
                                          ssm_b_im[lyr], ssm_c_re[lyr], ssm_c_im[lyr])
    bbig_hi, bbig_lo = _split_bf16(bbig)
    cbig_b = cbig.astype(BF16)
    dskip = ssm_d[lyr].reshape(1, SSM_WIDTH)
    wglu_b = w_glu[lyr].astype(BF16)
    wso = w_ssm_out[lyr].astype(BF16)
    wao = w_att_out[lyr].astype(BF16)
    wo = w_o[lyr].astype(BF16)
    g1 = ln1_g[lyr].reshape(1, D_MODEL)
    b1 = ln1_b[lyr].reshape(1, D_MODEL)
    g2 = ln2_g[lyr].reshape(1, D_MODEL)
    b2 = ln2_b[lyr].reshape(1, D_MODEL)
    r_pad = R_LANES - N_EGROUPS - N_EXPERTS
    wr = jnp.concatenate([w_route_group[lyr], w_route_expert[lyr], jnp.zeros((D_MODEL, r_pad), F32)], axis=1)
    wrh, wrl = _split_bf16(wr)
    br = jnp.concatenate([b_route_group[lyr], b_route_expert[lyr], jnp.zeros((r_pad,), F32)]).reshape(1, R_LANES)
    wgu = jnp.concatenate([w_exp_gate[lyr], w_exp_up[lyr]], axis=2).astype(BF16)
    wd = w_exp_down[lyr].astype(BF16)
    tri = (jnp.arange(TIE_W)[:, None] <= jnp.arange(TIE_W)[None, :])

    xp = x_prompt[0]
    head = jnp.concatenate([jnp.zeros((PADF, D_MODEL), F32), meta_tokens.astype(F32)], axis=0)
    u, q, k, v, kb, vb, qi, kiw, ga, gb = _in_proj(xp, head, wp, tm=256, head_rows=HEAD_ROWS)
    y_ssm, st = _ssm_prompt(u, bbig_hi, apw, cbig_b, dskip, wglu_b)
    kit = kiw[:, :IDX_DIM].astype(BF16).T
    o_att = _prompt_attention(qi, kiw, q, kit, kb.T, vb, tri.astype(BF16))
    h1, h1b, comb = _mix_ln_route(xp, head, y_ssm, o_att, ga, gb, wso, wao, wo, g1, b1, wrh, wrl, br,
                                  tm=ROW_TILE, head_rows=HEAD_ROWS)
    y_prompt = _moe_ln(h1, h1b, comb, wgu, wd, g2, b2, tm=HEAD_ROWS, head_rows=HEAD_ROWS)

    n_tok = n_b * n_s
    xs = x_sample.reshape(n_tok, D_MODEL)
    us, qs, ks, vs, _, _, qis, kiws, gas, gbs = _in_proj(xs, None, wp, tm=n_tok, head_rows=0)
    u_s = us.reshape(n_b, n_s, SSM_WIDTH).transpose(1, 0, 2)
    ys_s, hre_s, him_s = _ssm_sample(u_s, state_ssm_re[lyr].reshape(n_b, N_STATE),
                                     state_ssm_im[lyr].reshape(n_b, N_STATE),
                                     bbig_hi, bbig_lo, a_pair, cbig_b, dskip, wglu_b)
    ys_s = ys_s.transpose(1, 0, 2).reshape(n_tok, SSM_WIDTH)

    def pad_rows(a):
        return jnp.pad(a, [(0, 0), (0, S_ROWS - n_s)] + [(0, 0)] * (a.ndim - 2))

    qi4 = pad_rows(qis.reshape(n_b, n_s, IDX_HEADS, IDX_DIM)).transpose(0, 2, 1, 3)
    qi_s = qi4.reshape(n_b, IDX_HEADS * S_ROWS, IDX_DIM)
    w4 = kiws[:, IDX_DIM:IDX_DIM + IDX_HEADS] * (IDX_HEADS ** -0.5 * IDX_DIM ** -0.5)
    w_s = pad_rows(w4.reshape(n_b, n_s, IDX_HEADS)).transpose(0, 2, 1).reshape(n_b, IDX_HEADS * S_ROWS, 1)
    ki_new = kiws[:, :IDX_DIM].reshape(n_b, n_s, IDX_DIM)
    kinew = jnp.pad(ki_new, [(0, 0), (0, PAGE - n_s), (0, 0)]).astype(BF16)
    n_phys = cache_k.shape[1]
    kidx_t = cache_kidx[lyr].transpose(0, 2, 1)
    ck_t = cache_k[lyr].transpose(0, 2, 3, 1).reshape(n_phys, ATT_WIDTH, PAGE)
    cv_t = cache_v[lyr].transpose(0, 2, 3, 1).reshape(n_phys, ATT_WIDTH, PAGE)
    scores, par = _sample_scores(page_table, qi_s, w_s, kinew, kidx_t)
    o_s = _sample_attention(page_table, pad_rows(qs.astype(F32).reshape(n_b, n_s, ATT_WIDTH)), scores, par,
                            pad_rows(ks.reshape(n_b, n_s, ATT_WIDTH)), pad_rows(vs.reshape(n_b, n_s, ATT_WIDTH)),
                            tri[:S_TIE, :S_TIE].astype(F32), ck_t, cv_t)
    o_s = o_s[:, :n_s].reshape(n_tok, ATT_WIDTH).astype(BF16)
    h1s, h1bs, combs = _mix_ln_route(xs, None, ys_s, o_s, gas, gbs, wso, wao, wo, g1, b1, wrh, wrl, br,
                                     tm=n_tok, head_rows=0)
    y_sample = _moe_ln(h1s, h1bs, combs, wgu, wd, g2, b2, tm=n_tok, head_rows=0)

    def heads(a, lead):
        return a.reshape((depth,) + lead + (N_HEADS, HEAD_DIM))

    return (y_prompt.reshape(1, seq, D_MODEL),
            y_sample.reshape(n_b, n_s, D_MODEL),
            heads(k[PADF:], (1, t_len)), heads(v[PADF:], (1, t_len)),
            kiw[PADF:, :IDX_DIM].reshape(depth, 1, t_len, IDX_DIM),
            st[0].reshape(depth, 1, SSM_GROUPS, SSM_STATE), st[1].reshape(depth, 1, SSM_GROUPS, SSM_STATE),
            heads(ks, (n_b, n_s)), heads(vs, (n_b, n_s)),
            kiws[:, :IDX_DIM].reshape(depth, n_b, n_s, IDX_DIM),
            hre_s.reshape(depth, n_b, SSM_GROUPS, SSM_STATE), him_s.reshape(depth, n_b, SSM_GROUPS, SSM_STATE))
```

```python
import functools
import math

import jax
import jax.numpy as jnp
from jax import lax
from jax.experimental import pallas as pl
from jax.experimental.pallas import tpu as pltpu

F32 = jnp.float32
BF16 = jnp.bfloat16
I32 = jnp.int32

D_MODEL = 1024
N_META = 16
SSM_WIDTH = 512
SSM_GROUP_CH = 16
SSM_GROUPS = 32
SSM_STATE = 64
N_STATE = SSM_GROUPS * SSM_STATE
N_HEADS = 8
HEAD_DIM = 64
ATT_WIDTH = N_HEADS * HEAD_DIM
IDX_HEADS = 4
IDX_DIM = 64
TOPK = 256
PAGE = 128
N_EGROUPS = 4
EXPERTS_PER_GROUP = 4
N_EXPERTS = 16
D_EXPERT = 256
DN_ALPHA = 2.0 ** 0.25
LN_EPS = 1e-5
NEG_INF = float("-inf")

C_U, C_Q, C_K, C_V, C_QI, C_KIW, C_GA, C_GB, C_END = 0, 512, 1024, 1536, 2048, 2304, 2432, 3456, 4480
IN_SPLITS = (512, 512, 512, 512, 256, 64, 4, 1024, 1024)

ROW_TILE = 512
HEAD_ROWS = 1024
PADF = HEAD_ROWS - N_META
TQ = 256
TK = 1024
LANES = 128
HEAD_KEY0 = HEAD_ROWS - LANES
TIE_W = 256
M_INIT = -1e30
VMEM_LIMIT = 56 * 1024 * 1024

KEY_NEG_INF = -2139095041
KEY_POS_INF = 2139095040


def _cparams(n_axes, vmem=None):
    return pltpu.CompilerParams(dimension_semantics=("arbitrary",) * n_axes, vmem_limit_bytes=vmem)


def _const_spec(shape):
    nd = len(shape)
    return pl.BlockSpec(shape, lambda *_: (0,) * nd)


def _resident_spec(shape):
    nd = len(shape)
    return pl.BlockSpec(shape, lambda *_: (0,) * nd, pipeline_mode=pl.Buffered(1))


def _sigmoid(x):
    return 1.0 / (1.0 + jnp.exp(-x))


def _gelu_tanh(x):
    return 0.5 * x * (1.0 + jnp.tanh(math.sqrt(2.0 / math.pi) * (x + 0.044715 * (x * x * x))))


def _layernorm(z, g, b):
    mu = jnp.mean(z, axis=-1, keepdims=True)
    zc = z - mu
    var = jnp.mean(zc * zc, axis=-1, keepdims=True)
    return zc * lax.rsqrt(var + LN_EPS) * g + b


def _inproj_body(x_ref, h0_ref, w_ref, u_ref, q_ref, k_ref, v_ref, kb_ref, vb_ref, qi_ref, kiw_ref,
                 ga_ref, gb_ref, *, n_head_tiles):
    if n_head_tiles:
        h = jnp.where(pl.program_id(0) < n_head_tiles, h0_ref[...], x_ref[...])
    else:
        h = x_ref[...]
    hb = h.astype(BF16)

    def seg(a, b):
        return jnp.dot(hb, w_ref[:, a:b], preferred_element_type=F32)

    u_ref[...] = seg(C_U, C_Q)
    q_ref[...] = (seg(C_Q, C_K) * HEAD_DIM ** -0.5).astype(BF16)
    k = seg(C_K, C_V)
    k_ref[...] = k
    kb_ref[...] = k.astype(BF16)
    v = seg(C_V, C_QI)
    v_ref[...] = v
    vb_ref[...] = v.astype(BF16)
    qi_ref[...] = seg(C_QI, C_KIW).astype(BF16)
    kiw_ref[...] = seg(C_KIW, C_GA)
    ga_ref[...] = seg(C_GA, C_GB)
    gb_ref[...] = seg(C_GB, C_END)


def _in_proj(x, h0, wp, *, tm, head_rows):
    n_head_tiles = head_rows // tm
    n_rows = head_rows + x.shape[0]
    grid = (n_rows // tm,)
    if n_head_tiles:
        x_spec = pl.BlockSpec((tm, D_MODEL), lambda i: (jnp.maximum(i - n_head_tiles, 0), 0))
        h_spec = pl.BlockSpec((tm, D_MODEL), lambda i: (jnp.minimum(i, n_head_tiles - 1), 0))
    else:
        h0 = x
        x_spec = pl.BlockSpec((tm, D_MODEL), lambda i: (i, 0))
        h_spec = pl.BlockSpec((tm, D_MODEL), lambda i: (0, 0))

    def row(width):
        return pl.BlockSpec((tm, width), lambda i: (i, 0))

    widths = (512, 512, 512, 512, 512, 512, 256, 128, 1024, 1024)
    dtypes = (F32, BF16, F32, F32, BF16, BF16, BF16, F32, F32, F32)
    return pl.pallas_call(
        functools.partial(_inproj_body, n_head_tiles=n_head_tiles),
        grid=grid,
        in_specs=[x_spec, h_spec, _resident_spec((D_MODEL, C_END))],
        out_specs=[row(w) for w in widths],
        out_shape=[jax.ShapeDtypeStruct((n_rows, w), d) for w, d in zip(widths, dtypes)],
        compiler_params=_cparams(1, VMEM_LIMIT),
        name="in_proj",
    )(x, h0, wp)


SSM_CHUNK = 256


SSM_C_TILE = 128
SSM_C_DEPTH = (SSM_C_TILE // SSM_GROUP_CH) * SSM_STATE
SSM_B_TILE = 256
SSM_B_DEPTH = (SSM_B_TILE // SSM_STATE) * SSM_GROUP_CH


def _ssm_out(h_cat, u, cbig_ref, d_ref, wglu_ref):
    hb = h_cat.astype(BF16)
    tiles = []
    for t in range(SSM_WIDTH // SSM_C_TILE):
        cols = slice(SSM_C_TILE * t, SSM_C_TILE * (t + 1))
        re = slice(SSM_C_DEPTH * t, SSM_C_DEPTH * (t + 1))
        im = slice(N_STATE + SSM_C_DEPTH * t, N_STATE + SSM_C_DEPTH * (t + 1))
        tiles.append(jnp.dot(hb[:, re], cbig_ref[re, cols], preferred_element_type=F32)
                     + jnp.dot(hb[:, im], cbig_ref[im, cols], preferred_element_type=F32))
    y_lin = jnp.concatenate(tiles, axis=1)
    y = _gelu_tanh(y_lin + d_ref[...] * u)
    z = jnp.dot(y.astype(BF16), wglu_ref[...], preferred_element_type=F32)
    return y * _sigmoid(z)


def _ssm_prompt_body(u_ref, bbig_ref, apw_ref, cbig_ref, d_ref, wglu_ref, y_ref, st_ref,
                     bu_ref, hs_ref, hc_ref):
    @pl.when(pl.program_id(0) == 0)
    def _():
        hc_ref[...] = jnp.zeros_like(hc_ref)

    u = u_ref[...]
    ub = u.astype(BF16)
    for j in range(2 * N_STATE // SSM_B_TILE):
        jj = j % (N_STATE // SSM_B_TILE)
        cols = slice(SSM_B_TILE * j, SSM_B_TILE * (j + 1))
        rows = slice(SSM_B_DEPTH * jj, SSM_B_DEPTH * (jj + 1))
        bu_ref[:, cols] = jnp.dot(ub[:, rows], bbig_ref[rows, cols], preferred_element_type=F32)

    def step(r, carry):
        h_re, h_im = carry
        r8 = pl.multiple_of(r * 8, 8)
        x_re = bu_ref[pl.ds(r8, 8), 0:N_STATE]
        x_im = bu_ref[pl.ds(r8, 8), N_STATE:2 * N_STATE]
        for lvl, sh in enumerate((1, 2, 4)):
            a_re = apw_ref[2 * lvl]
            a_im = apw_ref[2 * lvl + 1]
            s_re = pltpu.roll(x_re, sh, 0)
            s_im = pltpu.roll(x_im, sh, 0)
            x_re, x_im = (x_re + (a_re * s_re - a_im * s_im), x_im + (a_re * s_im + a_im * s_re))
        c_re = apw_ref[6]
        c_im = apw_ref[7]
        x_re, x_im = (x_re + (c_re * h_re - c_im * h_im), x_im + (c_re * h_im + c_im * h_re))
        hs_ref[pl.ds(r8, 8), 0:N_STATE] = x_re
        hs_ref[pl.ds(r8, 8), N_STATE:2 * N_STATE] = x_im
        return x_re[7:8, :], x_im[7:8, :]

    h_re, h_im = lax.fori_loop(0, SSM_CHUNK // 8, step, (hc_ref[0:1, :], hc_ref[1:2, :]))
    hc_ref[0:1, :] = h_re
    hc_ref[1:2, :] = h_im
    st_ref[...] = hc_ref[...]
    y_ref[...] = _ssm_out(hs_ref[...], u, cbig_ref, d_ref, wglu_ref).astype(BF16)


def _ssm_prompt(u, bbig, apw, cbig, dskip, wglu):
    n_rows = u.shape[0]
    return pl.pallas_call(
        _ssm_prompt_body,
        grid=(n_rows // SSM_CHUNK,),
        in_specs=[pl.BlockSpec((SSM_CHUNK, SSM_WIDTH), lambda i: (i, 0)),
                  _const_spec((SSM_WIDTH, 2 * N_STATE)), _const_spec((8, 8, N_STATE)),
                  _const_spec((2 * N_STATE, SSM_WIDTH)), _const_spec((1, SSM_WIDTH)),
                  _const_spec((SSM_WIDTH, SSM_WIDTH))],
        out_specs=[pl.BlockSpec((SSM_CHUNK, SSM_WIDTH), lambda i: (i, 0)), _const_spec((2, N_STATE))],
        out_shape=[jax.ShapeDtypeStruct((n_rows, SSM_WIDTH), BF16),
                   jax.ShapeDtypeStruct((2, N_STATE), F32)],
        scratch_shapes=[pltpu.VMEM((SSM_CHUNK, 2 * N_STATE), F32), pltpu.VMEM((SSM_CHUNK, 2 * N_STATE), F32),
                        pltpu.VMEM((2, N_STATE), F32)],
        compiler_params=_cparams(1, VMEM_LIMIT),
        name="ssm_prompt",
    )(u, bbig, apw, cbig, dskip, wglu)


def _ssm_sample_body(u_ref, h0re_ref, h0im_ref, bhi_ref, blo_ref, a_ref, cbig_ref, d_ref, wglu_ref,
                     y_ref, hre_ref, him_ref):
    a_re = a_ref[0:1, :]
    a_im = a_ref[1:2, :]
    h_re = h0re_ref[...]
    h_im = h0im_ref[...]
    for s in range(u_ref.shape[0]):
        u = u_ref[s]
        u_hi = u.astype(BF16)
        u_lo = (u - u_hi.astype(F32)).astype(BF16)
        bu = (jnp.dot(u_hi, bhi_ref[...], preferred_element_type=F32)
              + (jnp.dot(u_lo, bhi_ref[...], preferred_element_type=F32)
                 + jnp.dot(u_hi, blo_ref[...], preferred_element_type=F32)))
        h_re, h_im = (a_re * h_re - a_im * h_im + bu[:, 0:N_STATE],
                      a_re * h_im + a_im * h_re + bu[:, N_STATE:2 * N_STATE])
        h_cat = jnp.concatenate([h_re, h_im], axis=1)
        y_ref[s] = _ssm_out(h_cat, u, cbig_ref, d_ref, wglu_ref).astype(BF16)
    hre_ref[...] = h_re
    him_ref[...] = h_im


def _ssm_sample(u_s, h0_re, h0_im, bhi, blo, a_pair, cbig, dskip, wglu):
    n_s, n_b, _ = u_s.shape
    return pl.pallas_call(
        _ssm_sample_body,
        grid=(1,),
        in_specs=[_const_spec(u_s.shape), _const_spec(h0_re.shape), _const_spec(h0_im.shape),
                  _const_spec(bhi.shape), _const_spec(blo.shape), _const_spec(a_pair.shape),
                  _const_spec(cbig.shape), _const_spec(dskip.shape), _const_spec(wglu.shape)],
        out_specs=[_const_spec((n_s, n_b, SSM_WIDTH)), _const_spec((n_b, N_STATE)), _const_spec((n_b, N_STATE))],
        out_shape=[jax.ShapeDtypeStruct((n_s, n_b, SSM_WIDTH), BF16),
                   jax.ShapeDtypeStruct((n_b, N_STATE), F32), jax.ShapeDtypeStruct((n_b, N_STATE), F32)],
        compiler_params=_cparams(1, VMEM_LIMIT),
        name="ssm_sample",
    )(u_s, h0_re, h0_im, bhi, blo, a_pair, cbig, dskip, wglu)


def _key_to_float(key):
    bits = key ^ ((key >> 31) & 0x7FFFFFFF)
    return lax.bitcast_convert_type(bits, F32)


def _float_to_key(x):
    bits = lax.bitcast_convert_type(x, I32)
    return bits ^ ((bits >> 31) & 0x7FFFFFFF)


BISECT_EVERY = 8
SELECT_MAX_PASSES = 32 * BISECT_EVERY + 8


def _kth_largest(count_fn, n_all, lane_max):
    kk = float(TOPK)
    n_grp = lane_max.shape[0] // LANES

    def to_cols(vec):
        cols = [jnp.broadcast_to(vec[g:g + 1], (LANES, LANES)).T for g in range(n_grp)]
        return cols[0] if n_grp == 1 else jnp.concatenate(cols, axis=0)

    def per_row(part, reduce):
        red = [reduce(part[LANES * g:LANES * (g + 1)].T, axis=0, keepdims=True) for g in range(n_grp)]
        return red[0] if n_grp == 1 else jnp.concatenate(red, axis=0)

    def total(th_vec):
        return per_row(count_fn(to_cols(th_vec), False), jnp.sum)

    row_max = per_row(lane_max, jnp.max)
    zero = jnp.zeros((n_grp, LANES), F32)
    both0 = count_fn(to_cols(zero), True)
    c_ge0 = per_row((both0 & 0xFFFF).astype(F32), jnp.sum)
    c_gt0 = per_row((both0 >> 16).astype(F32), jnp.sum)
    pos = c_gt0 > kk
    at0 = (c_ge0 >= kk) & (c_gt0 <= kk)
    lo0 = jnp.where(pos | at0, 0, KEY_NEG_INF).astype(I32)
    hi0 = jnp.where(pos, _float_to_key(row_max) + 1, jnp.where(at0, 0, -1)).astype(I32)
    c_lo0 = jnp.where(pos, c_gt0, jnp.where(at0, c_ge0, n_all))
    c_hi0 = jnp.where(pos, 0.0, jnp.where(at0, c_gt0, c_ge0))
    one = jnp.ones((n_grp, LANES), F32)

    def mid_key(lo, hi):
        return (lo >> 1) + (hi >> 1) + (lo & hi & 1)

    def finished(lo, hi, c_lo):
        return (c_lo == kk) | (mid_key(lo, hi) == lo)

    def cond(st):
        it, lo, hi, c_lo = st[0], st[1], st[2], st[3]
        open_rows = jnp.where(finished(lo, hi, c_lo), 0.0, 1.0)
        return (it < SELECT_MAX_PASSES) & (jnp.max(open_rows) > 0.0)

    def body(st):
        it, lo, hi, c_lo, c_hi, w_lo, w_hi, side = st
        done = finished(lo, hi, c_lo)
        lo_v = _key_to_float(lo)
        hi_v = _key_to_float(hi)
        g_lo = jnp.log(c_lo * (1.0 / (kk - 0.5))) * w_lo
        g_hi = jnp.log(jnp.maximum(c_hi, 0.5) * (1.0 / (kk - 0.5))) * w_hi
        th_i = lo_v + (g_lo / (g_lo - g_hi)) * (hi_v - lo_v)
        bounded = (lo_v > NEG_INF) & (hi_v < float("inf"))
        k_i = jnp.minimum(jnp.maximum(_float_to_key(th_i), lo + 1), hi - 1)
        k_mid = mid_key(lo, hi)
        probe = jnp.where((it & (BISECT_EVERY - 1)) == BISECT_EVERY - 1, k_mid, jnp.where(bounded, k_i, k_mid))
        c = total(_key_to_float(probe))
        ge = c >= kk
        mv_lo = (~done) & ge
        mv_hi = (~done) & (~ge)
        w_hi = jnp.where(mv_lo, jnp.where(side > 0.0, w_hi * 0.5, one), jnp.where(mv_hi, one, w_hi))
        w_lo = jnp.where(mv_hi, jnp.where(side < 0.0, w_lo * 0.5, one), jnp.where(mv_lo, one, w_lo))
        side = jnp.where(mv_lo, 1.0, jnp.where(mv_hi, -1.0, side))
        return (it + 1, jnp.where(mv_lo, probe, lo), jnp.where(mv_hi, probe, hi),
                jnp.where(mv_lo, c, c_lo), jnp.where(mv_hi, c, c_hi), w_lo, w_hi, side)

    st = lax.while_loop(cond, lambda s: body(body(s)), (jnp.int32(0), lo0, hi0, c_lo0, c_hi0, one, one, zero))
    lo, c_lo, c_hi = st[1], st[3], st[4]
    v = _key_to_float(lo)
    need = jnp.where(v == NEG_INF, 0.0, jnp.where(c_lo == kk, kk, kk - c_hi))
    return to_cols(v)[:, 0:1], to_cols(need)[:, 0:1]


def _count_step(cnt, x, th, both):
    if both:
        return cnt + jnp.where(x >= th, jnp.where(x > th, 65537, 1), 0)
    return cnt + jnp.where(x >= th, 1.0, 0.0)


def _topk_mask(x, v, need, off, tri):
    eq = x == v
    cum = jnp.dot(jnp.where(eq, 1.0, 0.0).astype(tri.dtype), tri, preferred_element_type=F32)
    sel = (x > v) | (eq & ((cum + off) <= need))
    w = x.shape[1]
    return sel, off + cum[:, w - 1:w]


def _pattn_body(qi_ref, kiw_ref, q_ref, kit_ref, kt_ref, v_hbm, tri_ref, o_ref,
                sc_ref, m_ref, l_ref, acc_ref, vhead_ref, vbuf_ref, vsem, *, first_block):
    i = pl.program_id(0)
    row0 = i * TQ

    @pl.when(i < first_block)
    def _():
        o_ref[...] = jnp.zeros_like(o_ref)

    @pl.when(i >= first_block)
    def _():
        jd = row0 // TK

        def head_copy():
            return pltpu.make_async_copy(v_hbm.at[pl.ds(HEAD_KEY0, LANES), :], vhead_ref, vsem.at[2])

        def tile_copy(j):
            return pltpu.make_async_copy(v_hbm.at[pl.ds(pl.multiple_of(j * TK, TK), TK), :],
                                         vbuf_ref.at[j & 1], vsem.at[j & 1])

        head_copy().start()

        @pl.when(jd >= 1)
        def _():
            tile_copy(1).start()
        w = kiw_ref[:, IDX_DIM:IDX_DIM + IDX_HEADS] * (IDX_HEADS ** -0.5 * IDX_DIM ** -0.5)
        qi_h = [qi_ref[:, IDX_DIM * h:IDX_DIM * (h + 1)] for h in range(IDX_HEADS)]
        q_h = [q_ref[:, HEAD_DIM * h:HEAD_DIM * (h + 1)] for h in range(N_HEADS)]

        def score_tile(c0, width, masked, mx):
            kit = kit_ref[:, pl.ds(c0, width)]
            tot = None
            for h in range(IDX_HEADS):
                z = jnp.dot(qi_h[h], kit, preferred_element_type=F32)
                r = jnp.maximum(z, 0.0) * w[:, h:h + 1]
                tot = r if tot is None else tot + r
            if masked:
                s_idx = c0 + lax.broadcasted_iota(I32, (TQ, width), 1)
                t_idx = row0 + lax.broadcasted_iota(I32, (TQ, width), 0)
                tot = jnp.where((s_idx <= t_idx) & (s_idx >= PADF), tot, NEG_INF)
            sc_ref[:, pl.ds(c0, width)] = tot
            for a in range(width // LANES):
                mx = jnp.maximum(mx, tot[:, LANES * a:LANES * (a + 1)])
            return mx

        def main_start(j):
            return pl.multiple_of(j * TK, TK)

        n_var = TK // TQ
        d_var = (row0 // TQ) % n_var
        d_widths = [TQ * (d + 1) for d in range(n_var)]
        d_sel = jnp.where(jd >= 1, d_var, n_var)

        def on_diag(fn, carry):
            return lax.switch(d_sel, [functools.partial(fn, wd) for wd in d_widths] + [lambda c: c], carry)

        mx = score_tile(HEAD_KEY0, LANES, True, jnp.full((TQ, LANES), NEG_INF, F32))
        mx = lax.fori_loop(1, jd, lambda j, mx: score_tile(main_start(j), TK, False, mx), mx)
        mx = on_diag(lambda wd, mx: score_tile(main_start(jd), wd, True, mx), mx)

        def count_fn(th, both):
            groups = []
            for g in range(TQ // LANES):
                rows = pl.ds(LANES * g, LANES)
                th_g = th[LANES * g:LANES * (g + 1)]

                def add(cnt, x, th_g=th_g):
                    return _count_step(cnt, x, th_g, both)

                def count_tile(width, j, cnt, rows=rows, add=add):
                    c0 = main_start(j)
                    for a in range(width // LANES):
                        cnt = add(cnt, sc_ref[rows, pl.ds(c0 + a * LANES, LANES)])
                    return cnt

                cnt = add(jnp.zeros((LANES, LANES), I32 if both else F32), sc_ref[rows, HEAD_KEY0:HEAD_ROWS])
                cnt = lax.fori_loop(1, jd, functools.partial(count_tile, TK), cnt)
                groups.append(on_diag(lambda wd, cnt, count_tile=count_tile: count_tile(wd, jd, cnt), cnt))
            return jnp.concatenate(groups, axis=0)

        n_all = jnp.where(jd >= 1, LANES + TK * (jd - 1) + TQ * (d_var + 1), LANES).astype(F32)
        v, need = _kth_largest(count_fn, n_all, mx)

        m_ref[...] = jnp.full_like(m_ref, M_INIT)
        l_ref[...] = jnp.zeros_like(l_ref)
        acc_ref[...] = jnp.zeros_like(acc_ref)
        first_half = lax.broadcasted_iota(I32, (TQ, LANES), 1) < HEAD_DIM

        def attend_tile(c0, width, off, v_cols):
            x = sc_ref[:, pl.ds(c0, width)]
            parts = []
            for a in range(0, width, TIE_W):
                wa = min(TIE_W, width - a)
                sel, off = _topk_mask(x[:, a:a + wa], v, need, off, tri_ref[0:wa, 0:wa])
                parts.append(jnp.where(sel, 0.0, NEG_INF))
            bias = parts[0] if len(parts) == 1 else jnp.concatenate(parts, axis=1)
            n_sl = width // LANES

            def probs(h):
                kt = kt_ref[HEAD_DIM * h:HEAD_DIM * (h + 1), pl.ds(c0, width)]
                s = jnp.dot(q_h[h], kt, preferred_element_type=F32) + bias
                sl = [s[:, LANES * a:LANES * (a + 1)] for a in range(n_sl)]
                pm = sl[0]
                for a in range(1, n_sl):
                    pm = jnp.maximum(pm, sl[a])
                m_old = m_ref[h]
                m_new = jnp.maximum(m_old, jnp.max(pm, axis=1, keepdims=True))
                alpha = jnp.exp(m_old - m_new)
                ps = [jnp.exp(sl[a] - m_new) for a in range(n_sl)]
                psum = ps[0]
                for a in range(1, n_sl):
                    psum = psum + ps[a]
                l_ref[h] = alpha * l_ref[h] + psum
                m_ref[h] = m_new
                return (ps[0] if n_sl == 1 else jnp.concatenate(ps, axis=1)).astype(BF16), alpha

            for j in range(N_HEADS // 2):
                p_e, alpha_e = probs(2 * j)
                p_o, alpha_o = probs(2 * j + 1)
                cols = slice(LANES * j, LANES * (j + 1))
                pv = jnp.dot(jnp.concatenate([p_e, p_o], axis=0), v_cols(cols),
                             preferred_element_type=F32)
                a_pair = jnp.where(first_half, alpha_e, alpha_o)
                pv_pair = jnp.where(first_half, pv[0:TQ], pv[TQ:2 * TQ])
                acc_ref[:, cols] = a_pair * acc_ref[:, cols] + pv_pair
            return off

        head_copy().wait()
        off = attend_tile(HEAD_KEY0, LANES, jnp.zeros((TQ, 1), F32), lambda cols: vhead_ref[:, cols])

        def attend_main(j, off):
            tile_copy(j).wait()
            tile_copy(j + 1).start()
            return attend_tile(main_start(j), TK, off, lambda cols: vbuf_ref[j & 1, :, cols])

        off = lax.fori_loop(1, jd, attend_main, off)

        @pl.when(jd >= 1)
        def _():
            tile_copy(jd).wait()

        on_diag(lambda wd, off: attend_tile(main_start(jd), wd, off,
                                            lambda cols: vbuf_ref[jd & 1, 0:wd, cols]), off)

        for j in range(N_HEADS // 2):
            l_e = jnp.sum(l_ref[2 * j], axis=1, keepdims=True)
            l_o = jnp.sum(l_ref[2 * j + 1], axis=1, keepdims=True)
            l_pair = jnp.where(first_half, l_e, l_o)
            inv = 1.0 / jnp.where(l_pair > 0.0, l_pair, 1.0)
            o_ref[:, LANES * j:LANES * (j + 1)] = (acc_ref[:, LANES * j:LANES * (j + 1)] * inv).astype(o_ref.dtype)


def _prompt_attention(qi, kiw, q, kit, kt, vb, tri):
    n_rows = q.shape[0]

    def row(width):
        return pl.BlockSpec((TQ, width), lambda i: (i, 0))

    return pl.pallas_call(
        functools.partial(_pattn_body, first_block=HEAD_KEY0 // TQ),
        grid=(n_rows // TQ,),
        in_specs=[row(IDX_HEADS * IDX_DIM), row(128), row(ATT_WIDTH),
                  _resident_spec(kit.shape), _resident_spec(kt.shape), pl.BlockSpec(memory_space=pl.ANY),
                  _resident_spec(tri.shape)],
        out_specs=row(ATT_WIDTH),
        out_shape=jax.ShapeDtypeStruct((n_rows, ATT_WIDTH), BF16),
        scratch_shapes=[pltpu.VMEM((TQ, n_rows), F32), pltpu.VMEM((N_HEADS, TQ, LANES), F32),
                        pltpu.VMEM((N_HEADS, TQ, LANES), F32), pltpu.VMEM((TQ, ATT_WIDTH), F32),
                        pltpu.VMEM((LANES, ATT_WIDTH), BF16),
                        pltpu.VMEM((2, TK, ATT_WIDTH), BF16), pltpu.SemaphoreType.DMA((3,))],
        compiler_params=_cparams(1, VMEM_LIMIT),
        name="prompt_attention",
    )(qi, kiw, q, kit, kt, vb, tri)


PAGES_PER_STEP = 32
SCORE_PAGES_PER_STEP = 64
S_CHUNK = PAGES_PER_STEP * PAGE
S_ROWS = 8


def _idx_scores_rows(qi, w, keys_bf16):
    z = lax.dot_general(qi, keys_bf16, (((1,), (1,)), ((), ())), preferred_element_type=F32)
    r = jnp.maximum(z, 0.0) * w
    tot = r[0:S_ROWS]
    for h in range(1, IDX_HEADS):
        tot = tot + r[S_ROWS * h:S_ROWS * (h + 1)]
    return tot


def _sscore_body(pt_ref, qi_ref, w_ref, kinew_ref, *refs, n_chunks, past, pages):
    page_refs = refs[:pages]
    sc_ref, kic_ref = refs[pages:]
    chunk = pages * PAGE
    c = pl.program_id(1)
    qi = qi_ref[...]
    w = w_ref[...]
    for r in range(pages):
        kic_ref[:, r * PAGE:(r + 1) * PAGE] = page_refs[r][...].astype(BF16)
    rr = jnp.maximum(jnp.dot(qi, kic_ref[...], preferred_element_type=F32), 0.0) * w
    tot = rr[0:S_ROWS]
    for h in range(1, IDX_HEADS):
        tot = tot + rr[S_ROWS * h:S_ROWS * (h + 1)]
    sc_ref[:, pl.ds(pl.multiple_of(c * chunk, chunk), chunk)] = tot

    @pl.when(c == n_chunks - 1)
    def _():
        tot = _idx_scores_rows(qi, w, kinew_ref[...])
        lane = lax.broadcasted_iota(I32, (S_ROWS, PAGE), 1)
        row = lax.broadcasted_iota(I32, (S_ROWS, PAGE), 0)
        sc_ref[:, past:past + PAGE] = jnp.where(lane <= row, tot, NEG_INF)


def _sselect_body(sc_ref, par_ref, *, sel_tile):
    n_tiles = sc_ref.shape[1] // sel_tile

    def tiles(fn, init):
        def body(j, acc):
            c0 = pl.multiple_of(j * sel_tile, LANES)
            for a in range(sel_tile // LANES):
                acc = fn(acc, sc_ref[:, pl.ds(c0 + a * LANES, LANES)])
            return acc

        return lax.fori_loop(0, n_tiles, body, init)

    def count_fn(th, both):
        return tiles(lambda cnt, x: _count_step(cnt, x, th, both), jnp.zeros((LANES, LANES), I32 if both else F32))

    lane_max = tiles(jnp.maximum, jnp.full((LANES, LANES), NEG_INF, F32))
    v, need = _kth_largest(count_fn, float(sc_ref.shape[1]), lane_max)
    lane = lax.broadcasted_iota(I32, (LANES, LANES), 1)
    par_ref[...] = jnp.where(lane == 0, v, jnp.where(lane == 1, need, 0.0))


def _sample_select(scores2d):
    n_rows, width = scores2d.shape
    sel_tile = max(t for t in range(LANES, 6 * LANES, LANES) if width % t == 0)
    pad = -n_rows % LANES
    if pad:
        scores2d = jnp.pad(scores2d, [(0, pad), (0, 0)])
    par = pl.pallas_call(
        functools.partial(_sselect_body, sel_tile=sel_tile),
        grid=((n_rows + pad) // LANES,),
        in_specs=[pl.BlockSpec((LANES, width), lambda i: (i, 0))],
        out_specs=pl.BlockSpec((LANES, LANES), lambda i: (i, 0)),
        out_shape=jax.ShapeDtypeStruct((n_rows + pad, LANES), F32),
        compiler_params=_cparams(1, VMEM_LIMIT),
        name="sample_select",
    )(scores2d)
    return par[:n_rows]


def _sample_scores(page_table, qi_s, w_s, kinew, cache_kidx):
    n_b, n_pages = page_table.shape
    pages = math.gcd(n_pages, SCORE_PAGES_PER_STEP)
    n_chunks = n_pages // pages
    past = n_pages * PAGE
    width = past + PAGE

    def page_spec(r):
        return pl.BlockSpec((None, IDX_DIM, PAGE), lambda b, c, pt: (pt[b, c * pages + r], 0, 0))

    def per_b(shape):
        return pl.BlockSpec((None,) + shape, lambda b, c, pt: (b, 0, 0))

    grid_spec = pltpu.PrefetchScalarGridSpec(
        num_scalar_prefetch=1,
        grid=(n_b, n_chunks),
        in_specs=[per_b((IDX_HEADS * S_ROWS, IDX_DIM)), per_b((IDX_HEADS * S_ROWS, 1)), per_b((PAGE, IDX_DIM))]
                 + [page_spec(r) for r in range(pages)],
        out_specs=per_b((S_ROWS, width)),
        scratch_shapes=[pltpu.VMEM((IDX_DIM, pages * PAGE), BF16)],
    )
    scores = pl.pallas_call(
        functools.partial(_sscore_body, n_chunks=n_chunks, past=past, pages=pages),
        grid_spec=grid_spec,
        out_shape=jax.ShapeDtypeStruct((n_b, S_ROWS, width), F32),
        compiler_params=_cparams(2, VMEM_LIMIT),
        name="sample_scores",
    )(page_table, qi_s, w_s, kinew, *([cache_kidx] * pages))
    par = _sample_select(scores.reshape(n_b * S_ROWS, width)).reshape(n_b, S_ROWS, LANES)
    return scores, par


S_TIE = 256


def _sattn_body(pt_ref, q_ref, sc_ref, scn_ref, par_ref, knew_ref, vnew_ref, tri_ref, *refs, n_chunks):
    k_refs = refs[:PAGES_PER_STEP]
    v_refs = refs[PAGES_PER_STEP:2 * PAGES_PER_STEP]
    o_ref, qbd_ref, kc_ref, vc_ref, m_ref, l_ref, acc_ref, off_ref = refs[2 * PAGES_PER_STEP:]
    c = pl.program_id(1)
    n_q = 4
    n_rows = n_q * N_HEADS
    head_of_col = lax.broadcasted_iota(I32, (N_HEADS, ATT_WIDTH), 1) // HEAD_DIM
    own_head = head_of_col == lax.broadcasted_iota(I32, (N_HEADS, ATT_WIDTH), 0)

    @pl.when(c == 0)
    def _():
        for s in range(n_q):
            qs = jnp.broadcast_to(q_ref[s:s + 1, :], (N_HEADS, ATT_WIDTH))
            qbd_ref[N_HEADS * s:N_HEADS * (s + 1), :] = jnp.where(own_head, qs, 0.0)
        m_ref[...] = jnp.full_like(m_ref, M_INIT)
        l_ref[...] = jnp.zeros_like(l_ref)
        acc_ref[...] = jnp.zeros_like(acc_ref)
        off_ref[...] = jnp.zeros_like(off_ref)

    v_thr = par_ref[:, 0:1]
    need = par_ref[:, 1:2]
    qbd = qbd_ref[...].astype(BF16)

    nt_dims = (((1,), (1,)), ((), ()))

    def attend(s_raw, pv_fn, x):
        width = x.shape[1]
        off = off_ref[...]
        parts = []
        for a in range(0, width, S_TIE):
            wa = min(S_TIE, width - a)
            sel, off = _topk_mask(x[:, a:a + wa], v_thr, need, off, tri_ref[0:wa, 0:wa])
            parts.append(jnp.where(sel, 0.0, NEG_INF))
        off_ref[...] = off
        bias8 = parts[0] if len(parts) == 1 else jnp.concatenate(parts, axis=1)
        bias = jnp.concatenate([jnp.broadcast_to(bias8[s:s + 1, :], (N_HEADS, width)) for s in range(n_q)], axis=0)
        s_mat = s_raw + bias
        m_old = m_ref[...]
        m_new = jnp.maximum(m_old, jnp.max(s_mat, axis=1, keepdims=True))
        p = jnp.exp(s_mat - m_new)
        alpha = jnp.exp(m_old - m_new)
        l_ref[...] = alpha * l_ref[...] + jnp.sum(p, axis=1, keepdims=True)
        acc_ref[...] = alpha * acc_ref[...] + pv_fn(p.astype(BF16))
        m_ref[...] = m_new

    for r in range(PAGES_PER_STEP):
        kc_ref[:, r * PAGE:(r + 1) * PAGE] = k_refs[r][...].astype(BF16)
        vc_ref[:, r * PAGE:(r + 1) * PAGE] = v_refs[r][...].astype(BF16)
    attend(jnp.dot(qbd, kc_ref[...], preferred_element_type=F32),
           lambda p: lax.dot_general(p, vc_ref[...], nt_dims, preferred_element_type=F32), sc_ref[...])

    @pl.when(c == n_chunks - 1)
    def _():
        pad = jnp.zeros((PAGE - S_ROWS, ATT_WIDTH), F32)
        k_new = jnp.concatenate([knew_ref[...], pad], axis=0).astype(BF16)
        v_new = jnp.concatenate([vnew_ref[...], pad], axis=0).astype(BF16)
        attend(lax.dot_general(qbd, k_new, nt_dims, preferred_element_type=F32),
               lambda p: jnp.dot(p, v_new, preferred_element_type=F32), scn_ref[...])
        l = l_ref[...]
        o_all = acc_ref[...] * (1.0 / jnp.where(l > 0.0, l, 1.0))
        row = lax.broadcasted_iota(I32, (S_ROWS, ATT_WIDTH), 0)
        out = jnp.zeros((S_ROWS, ATT_WIDTH), F32)
        for s in range(n_q):
            o_s = jnp.sum(jnp.where(own_head, o_all[N_HEADS * s:N_HEADS * (s + 1), :], 0.0), axis=0, keepdims=True)
            out = jnp.where(row == s, jnp.broadcast_to(o_s, (S_ROWS, ATT_WIDTH)), out)
        o_ref[...] = out


def _sample_attention(page_table, q_s, scores, par, knew, vnew, tri, cache_k, cache_v):
    n_b, n_pages = page_table.shape
    n_chunks = n_pages // PAGES_PER_STEP
    n_rows = 4 * N_HEADS

    def page_spec(r):
        return pl.BlockSpec((None, ATT_WIDTH, PAGE), lambda b, c, pt: (pt[b, c * PAGES_PER_STEP + r], 0, 0))

    def per_b(shape):
        return pl.BlockSpec((None,) + shape, lambda b, c, pt: (b, 0, 0))

    grid_spec = pltpu.PrefetchScalarGridSpec(
        num_scalar_prefetch=1,
        grid=(n_b, n_chunks),
        in_specs=[per_b((S_ROWS, ATT_WIDTH)),
                  pl.BlockSpec((None, S_ROWS, S_CHUNK), lambda b, c, pt: (b, 0, c)),
                  pl.BlockSpec((None, S_ROWS, PAGE), lambda b, c, pt: (b, 0, n_pages)),
                  per_b((S_ROWS, PAGE)), per_b((S_ROWS, ATT_WIDTH)), per_b((S_ROWS, ATT_WIDTH)),
                  pl.BlockSpec((S_TIE, S_TIE), lambda b, c, pt: (0, 0))]
                 + [page_spec(r) for r in range(PAGES_PER_STEP)] * 2,
        out_specs=per_b((S_ROWS, ATT_WIDTH)),
        scratch_shapes=[pltpu.VMEM((n_rows, ATT_WIDTH), F32),
                        pltpu.VMEM((ATT_WIDTH, S_CHUNK), BF16), pltpu.VMEM((ATT_WIDTH, S_CHUNK), BF16),
                        pltpu.VMEM((n_rows, 1), F32), pltpu.VMEM((n_rows, 1), F32),
                        pltpu.VMEM((n_rows, ATT_WIDTH), F32), pltpu.VMEM((S_ROWS, 1), F32)],
    )
    return pl.pallas_call(
        functools.partial(_sattn_body, n_chunks=n_chunks),
        grid_spec=grid_spec,
        out_shape=jax.ShapeDtypeStruct((n_b, S_ROWS, ATT_WIDTH), F32),
        compiler_params=_cparams(2, VMEM_LIMIT),
        name="sample_attention",
    )(page_table, q_s, scores, scores, par, knew, vnew, tri,
      *([cache_k] * PAGES_PER_STEP), *([cache_v] * PAGES_PER_STEP))


R_LANES = 128
R_E0 = N_EGROUPS


def _route(logits):
    lane = lax.broadcasted_iota(I32, logits.shape, 1)
    lane_f = lane.astype(F32)
    big = float(R_LANES)
    is_g = lane < N_EGROUPS
    gl = jnp.where(is_g, logits, NEG_INF)
    g_max = jnp.max(gl, axis=1, keepdims=True)
    g_sel = jnp.min(jnp.where(gl == g_max, lane_f, big), axis=1, keepdims=True)
    p_g = 1.0 / jnp.sum(jnp.exp(gl - g_max), axis=1, keepdims=True)
    member = (lane >= R_E0) & (lane < R_E0 + N_EXPERTS) & (((lane - R_E0) >> 2).astype(F32) == g_sel)
    el = jnp.where(member, logits, NEG_INF)
    e_max = jnp.max(el, axis=1, keepdims=True)
    pe = jnp.exp(el - e_max)
    pe = pe / jnp.sum(pe, axis=1, keepdims=True)
    p1 = jnp.max(pe, axis=1, keepdims=True)
    i1 = jnp.min(jnp.where(member & (pe == p1), lane_f, big), axis=1, keepdims=True)
    rest = member & (lane_f != i1)
    p2 = jnp.max(jnp.where(rest, pe, -1.0), axis=1, keepdims=True)
    i2 = jnp.min(jnp.where(rest & (pe == p2), lane_f, big), axis=1, keepdims=True)
    tot = p1 + p2
    return jnp.where(lane_f == i1, p1 / tot * p_g, 0.0) + jnp.where(lane_f == i2, p2 / tot * p_g, 0.0)


def _mix_body(x_ref, h0_ref, ys_ref, oa_ref, ga_ref, gb_ref, wso_ref, wao_ref, wo_ref, g1_ref, b1_ref,
              wrh_ref, wrl_ref, br_ref, h1_ref, h1b_ref, comb_ref, *, n_head_tiles):
    if n_head_tiles:
        h = jnp.where(pl.program_id(0) < n_head_tiles, h0_ref[...], x_ref[...])
    else:
        h = x_ref[...]
    a = jnp.dot(ys_ref[...], wso_ref[...], preferred_element_type=F32)
    b = jnp.dot(oa_ref[...], wao_ref[...], preferred_element_type=F32)
    m = _sigmoid(ga_ref[...]) * a + _sigmoid(gb_ref[...]) * b
    mix = jnp.dot(m.astype(BF16), wo_ref[...], preferred_element_type=F32)
    h1 = _layernorm(DN_ALPHA * h + mix, g1_ref[...], b1_ref[...])
    h1_ref[...] = h1
    h_hi = h1.astype(BF16)
    h1b_ref[...] = h_hi
    h_lo = (h1 - h_hi.astype(F32)).astype(BF16)
    logits = (jnp.dot(h_hi, wrh_ref[...], preferred_element_type=F32)
              + (jnp.dot(h_lo, wrh_ref[...], preferred_element_type=F32)
                 + jnp.dot(h_hi, wrl_ref[...], preferred_element_type=F32))) + br_ref[...]
    comb_ref[...] = _route(logits)


def _mix_ln_route(x, h0, ys, oa, ga, gb, wso, wao, wo, g1, b1, wrh, wrl, br, *, tm, head_rows):
    n_head_tiles = head_rows // tm
    n_rows = head_rows + x.shape[0]
    if n_head_tiles:
        x_spec = pl.BlockSpec((tm, D_MODEL), lambda i: (jnp.maximum(i - n_head_tiles, 0), 0))
        h_spec = pl.BlockSpec((tm, D_MODEL), lambda i: (jnp.minimum(i, n_head_tiles - 1), 0))
    else:
        h0 = x
        x_spec = pl.BlockSpec((tm, D_MODEL), lambda i: (i, 0))
        h_spec = pl.BlockSpec((tm, D_MODEL), lambda i: (0, 0))

    def row(width):
        return pl.BlockSpec((tm, width), lambda i: (i, 0))

    consts = (wso, wao, wo, g1, b1, wrh, wrl, br)
    return pl.pallas_call(
        functools.partial(_mix_body, n_head_tiles=n_head_tiles),
        grid=(n_rows // tm,),
        in_specs=[x_spec, h_spec, row(SSM_WIDTH), row(ATT_WIDTH), row(D_MODEL), row(D_MODEL)]
                 + [_const_spec(c.shape) for c in consts],
        out_specs=[row(D_MODEL), row(D_MODEL), row(R_LANES)],
        out_shape=[jax.ShapeDtypeStruct((n_rows, D_MODEL), F32), jax.ShapeDtypeStruct((n_rows, D_MODEL), BF16),
                   jax.ShapeDtypeStruct((n_rows, R_LANES), F32)],
        compiler_params=_cparams(1, VMEM_LIMIT),
        name="mix_ln_route",
    )(x, h0, ys, oa, ga, gb, *consts)


def _moe_body(h1_ref, h1b_ref, comb_ref, wgu_ref, wd_ref, g2_ref, b2_ref, y_ref, acc_ref):
    e = pl.program_id(1)

    @pl.when(e == 0)
    def _():
        acc_ref[...] = jnp.zeros_like(acc_ref)

    gu = jnp.dot(h1b_ref[...], wgu_ref[...], preferred_element_type=F32)
    gate = gu[:, 0:D_EXPERT]
    up = gu[:, D_EXPERT:2 * D_EXPERT]
    act = gate * _sigmoid(gate) * up
    comb = comb_ref[...]
    lane = lax.broadcasted_iota(I32, comb.shape, 1)
    c_e = jnp.sum(jnp.where(lane == e + R_E0, comb, 0.0), axis=1, keepdims=True)
    acc_ref[...] += jnp.dot((act * c_e).astype(BF16), wd_ref[...], preferred_element_type=F32)

    @pl.when(e == N_EXPERTS - 1)
    def _():
        y_ref[...] = _layernorm(DN_ALPHA * h1_ref[...] + acc_ref[...], g2_ref[...], b2_ref[...])


def _moe_ln(h1, h1b, comb, wgu, wd, g2, b2, *, tm, head_rows):
    n_head_tiles = head_rows // tm
    n_rows = h1.shape[0]
    out_rows = n_rows - head_rows

    def row(width):
        return pl.BlockSpec((tm, width), lambda i, e: (i, 0))

    return pl.pallas_call(
        _moe_body,
        grid=(n_rows // tm, N_EXPERTS),
        in_specs=[row(D_MODEL), row(D_MODEL), row(R_LANES),
                  pl.BlockSpec((None, D_MODEL, 2 * D_EXPERT), lambda i, e: (e, 0, 0)),
                  pl.BlockSpec((None, D_EXPERT, D_MODEL), lambda i, e: (e, 0, 0)),
                  pl.BlockSpec((1, D_MODEL), lambda i, e: (0, 0)), pl.BlockSpec((1, D_MODEL), lambda i, e: (0, 0))],
        out_specs=pl.BlockSpec((tm, D_MODEL), lambda i, e: (jnp.maximum(i - n_head_tiles, 0), 0)),
        out_shape=jax.ShapeDtypeStruct((out_rows, D_MODEL), F32),
        scratch_shapes=[pltpu.VMEM((tm, D_MODEL), F32)],
        compiler_params=_cparams(2, VMEM_LIMIT),
        name="moe_ln",
    )(h1, h1b, comb, wgu, wd, g2, b2)


def _ssm_tables(a_re, a_im, log_dt, b_re, b_im, c_re, c_im):
    dt = jnp.exp(log_dt)[:, None]
    mag = jnp.exp(a_re * dt)
    ab_re = mag * jnp.cos(a_im * dt)
    ab_im = mag * jnp.sin(a_im * dt)
    den = a_re * a_re + a_im * a_im
    nr = ab_re - 1.0
    f_re = (nr * a_re + ab_im * a_im) / den
    f_im = (ab_im * a_re - nr * a_im) / den
    bb_re = f_re[..., None] * b_re - f_im[..., None] * b_im
    bb_im = f_re[..., None] * b_im + f_im[..., None] * b_re
    eye = jnp.eye(SSM_GROUPS, dtype=F32)

    def in_mat(bb):
        return jnp.einsum("gpm,gh->gmhp", bb, eye).reshape(SSM_WIDTH, N_STATE)

    def out_mat(cc):
        return jnp.einsum("gmp,gh->gphm", cc, eye).reshape(N_STATE, SSM_WIDTH)

    bbig = jnp.concatenate([in_mat(bb_re), in_mat(bb_im)], axis=1)
    cbig = jnp.concatenate([out_mat(c_re), -out_mat(c_im)], axis=0)
    ar = ab_re.reshape(1, N_STATE)
    ai = ab_im.reshape(1, N_STATE)

    def cmul(x, y):
        return x[0] * y[0] - x[1] * y[1], x[0] * y[1] + x[1] * y[0]

    a1 = (ar, ai)
    pows = [a1]
    for _ in range(7):
        pows.append(cmul(pows[-1], a1))
    row = jnp.arange(8)[:, None]
    tabs = []
    for sh, pw in ((1, pows[0]), (2, pows[1]), (4, pows[3])):
        keep = (row >= sh).astype(F32)
        tabs += [keep * pw[0], keep * pw[1]]
    tabs += [jnp.concatenate([p[0] for p in pows], axis=0), jnp.concatenate([p[1] for p in pows], axis=0)]
    apw = jnp.stack(tabs)
    a_pair = jnp.concatenate([ar, ai], axis=0)
    return bbig, cbig, apw, a_pair


def _split_bf16(x):
    hi = x.astype(BF16)
    return hi, (x - hi.astype(F32)).astype(BF16)


def kernel(x_prompt, x_sample, cache_k, cache_v, cache_kidx, state_ssm_re, state_ssm_im, page_table,
           meta_tokens, w_in, ssm_a_re, ssm_a_im, ssm_log_dt, ssm_b_re, ssm_b_im, ssm_c_re, ssm_c_im,
           ssm_d, w_glu, w_ssm_out, w_att_out, w_o, ln1_g, ln1_b, w_route_group, b_route_group,
           w_route_expert, b_route_expert, w_exp_gate, w_exp_up, w_exp_down, ln2_g, ln2_b):
    depth = w_in.shape[0]
    assert depth == 1 and x_prompt.shape[0] == 1
    n_b, n_s, _ = x_sample.shape
    seq = x_prompt.shape[1]
    t_len = seq + N_META
    lyr = 0

    cuts = [0]
    for c in IN_SPLITS:
        cuts.append(cuts[-1] + c)
    w = w_in[lyr]
    wp = jnp.concatenate([w[:, :cuts[7]], jnp.zeros((D_MODEL, C_GA - C_KIW - IDX_DIM - IDX_HEADS), F32),
                          w[:, cuts[7]:]], axis=1).astype(BF16)
    bbig, cbig, apw, a_pair = _ssm_tables(ssm_a_re[lyr], ssm_a_im[lyr], ssm_log_dt[lyr], ssm_b_re[lyr],
                                          ssm_b_im[lyr], ssm_c_re[lyr], ssm_c_im[lyr])
    bbig_hi, bbig_lo = _split_bf16(bbig)
    cbig_b = cbig.astype(BF16)
    dskip = ssm_d[lyr].reshape(1, SSM_WIDTH)
    wglu_b = w_glu[lyr].astype(BF16)
    wso = w_ssm_out[lyr].astype(BF16)
    wao = w_att_out[lyr].astype(BF16)
    wo = w_o[lyr].astype(BF16)
    g1 = ln1_g[lyr].reshape(1, D_MODEL)
    b1 = ln1_b[lyr].reshape(1, D_MODEL)
    g2 = ln2_g[lyr].reshape(1, D_MODEL)
    b2 = ln2_b[lyr].reshape(1, D_MODEL)
    r_pad = R_LANES - N_EGROUPS - N_EXPERTS
    wr = jnp.concatenate([w_route_group[lyr], w_route_expert[lyr], jnp.zeros((D_MODEL, r_pad), F32)], axis=1)
    wrh, wrl = _split_bf16(wr)
    br = jnp.concatenate([b_route_group[lyr], b_route_expert[lyr], jnp.zeros((r_pad,), F32)]).reshape(1, R_LANES)
    wgu = jnp.concatenate([w_exp_gate[lyr], w_exp_up[lyr]], axis=2).astype(BF16)
    wd = w_exp_down[lyr].astype(BF16)
    tri = (jnp.arange(TIE_W)[:, None] <= jnp.arange(TIE_W)[None, :])

    xp = x_prompt[0]
    head = jnp.concatenate([jnp.zeros((PADF, D_MODEL), F32), meta_tokens.astype(F32)], axis=0)
    u, q, k, v, kb, vb, qi, kiw, ga, gb = _in_proj(xp, head, wp, tm=256, head_rows=HEAD_ROWS)
    y_ssm, st = _ssm_prompt(u, bbig_hi, apw, cbig_b, dskip, wglu_b)
    kit = kiw[:, :IDX_DIM].astype(BF16).T
    o_att = _prompt_attention(qi, kiw, q, kit, kb.T, vb, tri.astype(BF16))
    h1, h1b, comb = _mix_ln_route(xp, head, y_ssm, o_att, ga, gb, wso, wao, wo, g1, b1, wrh, wrl, br,
                                  tm=ROW_TILE, head_rows=HEAD_ROWS)
    y_prompt = _moe_ln(h1, h1b, comb, wgu, wd, g2, b2, tm=HEAD_ROWS, head_rows=HEAD_ROWS)

    n_tok = n_b * n_s
    xs = x_sample.reshape(n_tok, D_MODEL)
    us, qs, ks, vs, _, _, qis, kiws, gas, gbs = _in_proj(xs, None, wp, tm=n_tok, head_rows=0)
    u_s = us.reshape(n_b, n_s, SSM_WIDTH).transpose(1, 0, 2)
    ys_s, hre_s, him_s = _ssm_sample(u_s, state_ssm_re[lyr].reshape(n_b, N_STATE),
                                     state_ssm_im[lyr].reshape(n_b, N_STATE),
                                     bbig_hi, bbig_lo, a_pair, cbig_b, dskip, wglu_b)
    ys_s = ys_s.transpose(1, 0, 2).reshape(n_tok, SSM_WIDTH)

    def pad_rows(a):
        return jnp.pad(a, [(0, 0), (0, S_ROWS - n_s)] + [(0, 0)] * (a.ndim - 2))

    qi4 = pad_rows(qis.reshape(n_b, n_s, IDX_HEADS, IDX_DIM)).transpose(0, 2, 1, 3)
    qi_s = qi4.reshape(n_b, IDX_HEADS * S_ROWS, IDX_DIM)
    w4 = kiws[:, IDX_DIM:IDX_DIM + IDX_HEADS] * (IDX_HEADS ** -0.5 * IDX_DIM ** -0.5)
    w_s = pad_rows(w4.reshape(n_b, n_s, IDX_HEADS)).transpose(0, 2, 1).reshape(n_b, IDX_HEADS * S_ROWS, 1)
    ki_new = kiws[:, :IDX_DIM].reshape(n_b, n_s, IDX_DIM)
    kinew = jnp.pad(ki_new, [(0, 0), (0, PAGE - n_s), (0, 0)]).astype(BF16)
    n_phys = cache_k.shape[1]
    kidx_t = cache_kidx[lyr].transpose(0, 2, 1)
    ck_t = cache_k[lyr].transpose(0, 2, 3, 1).reshape(n_phys, ATT_WIDTH, PAGE)
    cv_t = cache_v[lyr].transpose(0, 2, 3, 1).reshape(n_phys, ATT_WIDTH, PAGE)
    scores, par = _sample_scores(page_table, qi_s, w_s, kinew, kidx_t)
    o_s = _sample_attention(page_table, pad_rows(qs.astype(F32).reshape(n_b, n_s, ATT_WIDTH)), scores, par,
                            pad_rows(ks.reshape(n_b, n_s, ATT_WIDTH)), pad_rows(vs.reshape(n_b, n_s, ATT_WIDTH)),
                            tri[:S_TIE, :S_TIE].astype(F32), ck_t, cv_t)
    o_s = o_s[:, :n_s].reshape(n_tok, ATT_WIDTH).astype(BF16)
    h1s, h1bs, combs = _mix_ln_route(xs, None, ys_s, o_s, gas, gbs, wso, wao, wo, g1, b1, wrh, wrl, br,
                                     tm=n_tok, head_rows=0)
    y_sample = _moe_ln(h1s, h1bs, combs, wgu, wd, g2, b2, tm=n_tok, head_rows=0)

    def heads(a, lead):
        return a.reshape((depth,) + lead + (N_HEADS, HEAD_DIM))

    return (y_prompt.reshape(1, seq, D_MODEL),
            y_sample.reshape(n_b, n_s, D_MODEL),
            heads(k[PADF:], (1, t_len)), heads(v[PADF:], (1, t_len)),
            kiw[PADF:, :IDX_DIM].reshape(depth, 1, t_len, IDX_DIM),
            st[0].reshape(depth, 1, SSM_GROUPS, SSM_STATE), st[1].reshape(depth, 1, SSM_GROUPS, SSM_STATE),
            heads(ks, (n_b, n_s)), heads(vs, (n_b, n_s)),
            kiws[:, :IDX_DIM].reshape(depth, n_b, n_s, IDX_DIM),
            hre_s.reshape(depth, n_b, SSM_GROUPS, SSM_STATE), him_s.reshape(depth, n_b, SSM_GROUPS, SSM_STATE))
```
